```python
import math
import jax, jax.numpy as jnp
from jax import lax
import numpy as np

D_MODEL = 1024
BATCH = 8
SEQ = 4096
DEPTH = 2

CHUNK = 64
Q_BLOCK = 128
DSA_Q_BLOCK = 64
ROPE_THETA = 10000.0
HEAD_DIM = 64
LN_EPS = 1e-5
DIFF_HEADS = 4
DIFF_V_DIM = 2 * HEAD_DIM
DSA_HEADS = 8
IDX_HEADS = 4
DSA_TOPK_MAX = 256
IDX_W_SCALE = (IDX_HEADS * HEAD_DIM) ** -0.5
FOX_HEADS = 8
N_BRANCH = 3
BRANCH_WIDTH = 512
N_EXPERTS = 32
N_GROUPS = 4
EXPERTS_PER_GROUP = N_EXPERTS // N_GROUPS
TOP_K = 2
D_EXPERT = 512
MOE_BLOCK = 128
DEEPNORM_ALPHA = (2 * DEPTH) ** 0.25
DEEPNORM_BETA = (8 * DEPTH) ** -0.25
DIFF_QK_WIDTH = DIFF_HEADS * 2 * HEAD_DIM
DIFF_V_WIDTH = DIFF_HEADS * DIFF_V_DIM
DSA_WIDTH = DSA_HEADS * HEAD_DIM
IDX_Q_WIDTH = IDX_HEADS * HEAD_DIM
FOX_WIDTH = FOX_HEADS * HEAD_DIM
IN_SPLITS = (DIFF_QK_WIDTH, DIFF_QK_WIDTH, DIFF_V_WIDTH,
             DSA_WIDTH, DSA_WIDTH, DSA_WIDTH,
             IDX_Q_WIDTH, HEAD_DIM, IDX_HEADS,
             FOX_WIDTH, FOX_WIDTH, FOX_WIDTH, FOX_HEADS,
             N_BRANCH * D_MODEL)
IN_WIDTH = sum(IN_SPLITS)

kernel_name = 'hybrid_diff_dsa_fox_grouped_moe'


def layer_norm(x, g, b):
    xf = x.astype(jnp.float32)
    mu = jnp.mean(xf, axis=-1, keepdims=True)
    var = jnp.mean(jnp.square(xf - mu), axis=-1, keepdims=True)
    return ((xf - mu) * lax.rsqrt(var + LN_EPS) * g + b).astype(x.dtype)


def rms_norm(x, g):
    xf = x.astype(jnp.float32)
    y = xf * lax.rsqrt(jnp.mean(jnp.square(xf), axis=-1, keepdims=True) + LN_EPS)
    return (y * g).astype(x.dtype)


def rope_tables(positions, dim, dtype):
    inv = ROPE_THETA ** (-jnp.arange(0, dim, 2, dtype=jnp.float32) / dim)
    ang = positions.astype(jnp.float32)[..., None] * inv
    return jnp.cos(ang).astype(dtype), jnp.sin(ang).astype(dtype)


def apply_rope(x, cos, sin):
    shp = cos.shape[:2] + (1,) * (x.ndim - 3) + cos.shape[-1:]
    c, s = cos.reshape(shp), sin.reshape(shp)
    half = x.shape[-1] // 2
    x1, x2 = x[..., :half], x[..., half:]
    return jnp.concatenate([x1 * c - x2 * s, x2 * c + x1 * s], axis=-1)


def to_blocks(x, size):
    B, S = x.shape[:2]
    return jnp.moveaxis(x.reshape((B, S // size, size) + x.shape[2:]), 1, 0)


def from_blocks(x):
    x = jnp.moveaxis(x, 0, 1)
    return x.reshape((x.shape[0], x.shape[1] * x.shape[2]) + x.shape[3:])


def chunk_visible(qpos, kpos):
    return kpos[None, :] < (qpos[:, None] // CHUNK + 1) * CHUNK


def diff_attention(q, k, v, lam, norm_g, lam_init):
    B, S = q.shape[:2]
    nb = S // Q_BLOCK
    scale = HEAD_DIM ** -0.5
    kpos = jnp.arange(S)

    def block(args):
        qb, i = args
        vis = chunk_visible(i * Q_BLOCK + jnp.arange(Q_BLOCK), kpos)
        sc = jnp.einsum('bqhmd,bkhmd->bhmqk', qb, k).astype(jnp.float32) * scale
        p = jax.nn.softmax(jnp.where(vis, sc, -jnp.inf), axis=-1)
        a = p[:, :, 0] - lam * p[:, :, 1]
        return jnp.einsum('bhqk,bkhd->bqhd', a.astype(v.dtype), v)

    o = from_blocks(lax.map(block, (to_blocks(q, Q_BLOCK), jnp.arange(nb))))
    o = rms_norm(o, norm_g) * (1.0 - lam_init)
    return o.reshape(B, S, DIFF_V_WIDTH)


def dsa_attention(q, k, v, iq, ik, iw):
    B, S = q.shape[:2]
    topk = min(DSA_TOPK_MAX, S // 4)
    nb = S // DSA_Q_BLOCK
    scale = HEAD_DIM ** -0.5
    kpos = jnp.arange(S)
    gather = jax.vmap(lambda t, idx: t[idx])

    def block(args):
        qb, iqb, iwb, i = args
        vis = chunk_visible(i * DSA_Q_BLOCK + jnp.arange(DSA_Q_BLOCK), kpos)
        logits = jnp.einsum('bqhd,bkd->bqhk', iqb, ik).astype(jnp.float32)
        score = jnp.einsum('bqh,bqhk->bqk', iwb.astype(jnp.float32), jax.nn.relu(logits))
        score = jnp.where(vis, score, -jnp.inf)
        top_val, top_idx = lax.top_k(score, topk)
        valid = jnp.isfinite(top_val)
        kg = gather(k, top_idx)
        vg = gather(v, top_idx)
        sc = jnp.einsum('bqhd,bqkhd->bhqk', qb, kg).astype(jnp.float32) * scale
        p = jax.nn.softmax(jnp.where(valid[:, None], sc, -jnp.inf), axis=-1)
        return jnp.einsum('bhqk,bqkhd->bqhd', p.astype(v.dtype), vg)

    o = lax.map(block, (to_blocks(q, DSA_Q_BLOCK), to_blocks(iq, DSA_Q_BLOCK),
                        to_blocks(iw, DSA_Q_BLOCK), jnp.arange(nb)))
    return from_blocks(o).reshape(B, S, DSA_WIDTH)


def forgetting_attention(q, k, v, log_f):
    B, S = q.shape[:2]
    nb = S // Q_BLOCK
    scale = HEAD_DIM ** -0.5
    kpos = jnp.arange(S)
    c = lax.cumsum(log_f, axis=1)
    cT = jnp.transpose(c, (0, 2, 1))

    def block(args):
        qb, cqb, i = args
        qpos = i * Q_BLOCK + jnp.arange(Q_BLOCK)
        causal = kpos[None, :] <= qpos[:, None]
        sc = jnp.einsum('bqhd,bkhd->bhqk', qb, k).astype(jnp.float32) * scale
        sc = sc + jnp.transpose(cqb, (0, 2, 1))[..., None] - cT[:, :, None, :]
        p = jax.nn.softmax(jnp.where(causal, sc, -jnp.inf), axis=-1)
        return jnp.einsum('bhqk,bkhd->bqhd', p.astype(v.dtype), v)

    o = lax.map(block, (to_blocks(q, Q_BLOCK), to_blocks(c, Q_BLOCK), jnp.arange(nb)))
    return from_blocks(o).reshape(B, S, FOX_WIDTH)


def hybrid_mixer(x, cos, sin, w_in, b_forget, diff_lam, diff_norm_g, w_br, w_o, lam_init):
    B, S, D = x.shape
    z = x @ w_in
    split_at = np.cumsum(IN_SPLITS)[:-1].tolist()
    dq, dk, dv, sq, sk, sv, iq, ik, iw, fq, fk, fv, ff, g = jnp.split(z, split_at, axis=-1)

    dq = apply_rope(dq.reshape(B, S, DIFF_HEADS, 2, HEAD_DIM), cos, sin)
    dk = apply_rope(dk.reshape(B, S, DIFF_HEADS, 2, HEAD_DIM), cos, sin)
    lamf = diff_lam.astype(jnp.float32)
    lam = (jnp.exp(jnp.sum(lamf[0] * lamf[1])) - jnp.exp(jnp.sum(lamf[2] * lamf[3]))
           + lam_init)
    o_diff = diff_attention(dq, dk, dv.reshape(B, S, DIFF_HEADS, DIFF_V_DIM),
                            lam, diff_norm_g, lam_init)

    sq = apply_rope(sq.reshape(B, S, DSA_HEADS, HEAD_DIM), cos, sin)
    sk = apply_rope(sk.reshape(B, S, DSA_HEADS, HEAD_DIM), cos, sin)
    iq = apply_rope(iq.reshape(B, S, IDX_HEADS, HEAD_DIM), cos, sin)
    ik = apply_rope(ik, cos, sin)
    o_dsa = dsa_attention(sq, sk, sv.reshape(B, S, DSA_HEADS, HEAD_DIM),
                          iq, ik, iw * IDX_W_SCALE)

    log_f = jax.nn.log_sigmoid(ff.astype(jnp.float32) + b_forget)
    o_fox = forgetting_attention(fq.reshape(B, S, FOX_HEADS, HEAD_DIM),
                                 fk.reshape(B, S, FOX_HEADS, HEAD_DIM),
                                 fv.reshape(B, S, FOX_HEADS, HEAD_DIM), log_f)

    gates = jax.nn.sigmoid(g.astype(jnp.float32)).astype(x.dtype).reshape(B, S, N_BRANCH, D)
    y = (gates[:, :, 0] * (o_diff @ w_br[0])
         + gates[:, :, 1] * (o_dsa @ w_br[1])
         + gates[:, :, 2] * (o_fox @ w_br[2]))
    return y @ w_o


def grouped_moe(x, w_router, b_router, w_gate, w_up, w_down):
    B, S, D = x.shape
    T = B * S
    A = T * TOP_K
    xt = x.reshape(T, D)
    aff = jax.nn.sigmoid((xt @ w_router).astype(jnp.float32))
    sel = (aff + b_router.astype(jnp.float32)).reshape(T, N_GROUPS, EXPERTS_PER_GROUP)
    grp_score = jnp.sum(lax.top_k(sel, 2)[0], axis=-1)
    g_idx = jnp.argmax(grp_score, axis=-1).astype(jnp.int32)
    in_grp = jnp.take_along_axis(sel, g_idx[:, None, None], axis=1)[:, 0]
    _, loc = lax.top_k(in_grp, TOP_K)
    e_idx = g_idx[:, None] * EXPERTS_PER_GROUP + loc
    wts = jnp.take_along_axis(aff, e_idx, axis=1)
    wts = wts / jnp.sum(wts, axis=-1, keepdims=True)

    flat_e = e_idx.reshape(A)
    order = jnp.argsort(flat_e)
    e_sorted = flat_e[order]
    tok_sorted = order // TOP_K
    w_sorted = wts.reshape(A)[order]
    counts = jnp.bincount(flat_e, length=N_EXPERTS)
    padded = (counts + MOE_BLOCK - 1) // MOE_BLOCK * MOE_BLOCK
    pad_end = jnp.cumsum(padded)
    pad_start = pad_end - padded
    start = jnp.cumsum(counts) - counts
    dest = pad_start[e_sorted] + jnp.arange(A) - start[e_sorted]
    R = -(-A // MOE_BLOCK) * MOE_BLOCK + N_EXPERTS * MOE_BLOCK
    nblk = R // MOE_BLOCK
    row_tok = jnp.full((R,), T, dtype=tok_sorted.dtype).at[dest].set(tok_sorted)
    x_rows = jnp.concatenate([xt, jnp.zeros((1, D), xt.dtype)], axis=0)[row_tok]
    x_rows = x_rows.reshape(nblk, MOE_BLOCK, D)
    blk_e = jnp.minimum(jnp.searchsorted(pad_end, jnp.arange(nblk) * MOE_BLOCK, side='right'),
                        N_EXPERTS - 1)

    def expert_block(args):
        xb, e = args
        h = jax.nn.silu(xb @ w_gate[e]) * (xb @ w_up[e])
        return h @ w_down[e]

    y_rows = lax.map(expert_block, (x_rows, blk_e)).reshape(R, D)
    y = y_rows[dest] * w_sorted[:, None].astype(x.dtype)
    return jnp.zeros((T, D), x.dtype).at[tok_sorted].add(y).reshape(B, S, D)


def setup_inputs(seed: int = 0) -> dict:
    key = jax.random.key(seed)
    ks = jax.random.split(key, 20)

    def nrm(k, shape, scale):
        return jax.random.normal(k, shape, jnp.float32) * scale

    x = nrm(ks[0], (BATCH, SEQ, D_MODEL), 1.0)
    offset = jax.random.randint(ks[1], (BATCH, 1), 0, SEQ, dtype=jnp.int32)
    positions = offset + jnp.arange(SEQ, dtype=jnp.int32)[None, :]
    return {
        'x': x,
        'positions': positions,
        'ln_in_g': 1.0 + nrm(ks[2], (D_MODEL,), 0.02),
        'ln_in_b': nrm(ks[3], (D_MODEL,), 0.02),
        'w_in': nrm(ks[4], (DEPTH, D_MODEL, IN_WIDTH), D_MODEL ** -0.5),
        'b_forget': 3.0 + nrm(ks[5], (DEPTH, FOX_HEADS), 0.5),
        'diff_lam': nrm(ks[6], (DEPTH, 4, HEAD_DIM), 0.1),
        'diff_norm_g': 1.0 + nrm(ks[7], (DEPTH, DIFF_V_DIM), 0.02),
        'w_br': nrm(ks[8], (DEPTH, N_BRANCH, BRANCH_WIDTH, D_MODEL),
                    BRANCH_WIDTH ** -0.5 * DEEPNORM_BETA),
        'w_o': nrm(ks[9], (DEPTH, D_MODEL, D_MODEL), D_MODEL ** -0.5 * DEEPNORM_BETA),
        'ln1_g': 1.0 + nrm(ks[10], (DEPTH, D_MODEL), 0.02),
        'ln1_b': nrm(ks[11], (DEPTH, D_MODEL), 0.02),
        'w_router': nrm(ks[12], (D_MODEL, N_EXPERTS), D_MODEL ** -0.5),
        'b_router': nrm(ks[13], (N_EXPERTS,), 0.01),
        'w_gate': nrm(ks[14], (DEPTH, N_EXPERTS, D_MODEL, D_EXPERT), D_MODEL ** -0.5),
        'w_up': nrm(ks[15], (DEPTH, N_EXPERTS, D_MODEL, D_EXPERT), D_MODEL ** -0.5),
        'w_down': nrm(ks[16], (DEPTH, N_EXPERTS, D_EXPERT, D_MODEL),
                      D_EXPERT ** -0.5 * DEEPNORM_BETA),
        'ln2_g': 1.0 + nrm(ks[17], (DEPTH, D_MODEL), 0.02),
        'ln2_b': nrm(ks[18], (DEPTH, D_MODEL), 0.02),
    }


def reference(x, positions, ln_in_g, ln_in_b, w_in, b_forget, diff_lam, diff_norm_g,
              w_br, w_o, ln1_g, ln1_b, w_router, b_router, w_gate, w_up, w_down,
              ln2_g, ln2_b):
    cos, sin = rope_tables(positions, HEAD_DIM, x.dtype)
    h = layer_norm(x, ln_in_g, ln_in_b)
    for l in range(DEPTH):
        lam_init = 0.8 - 0.6 * math.exp(-0.3 * l)
        m = hybrid_mixer(h, cos, sin, w_in[l], b_forget[l], diff_lam[l], diff_norm_g[l],
                         w_br[l], w_o[l], lam_init)
        h = layer_norm(DEEPNORM_ALPHA * h + m, ln1_g[l], ln1_b[l])
        f = grouped_moe(h, w_router, b_router, w_gate[l], w_up[l], w_down[l])
        h = layer_norm(DEEPNORM_ALPHA * h + f, ln2_g[l], ln2_b[l])
    return h
```

```python
import functools
import math

import jax
import jax.numpy as jnp
from jax import lax
from jax.experimental import pallas as pl
from jax.experimental.pallas import tpu as pltpu

F32 = jnp.float32
BF16 = jnp.bfloat16

CHUNK = 64
HEAD_DIM = 64
ROPE_THETA = 10000.0
LN_EPS = 1e-5
DIFF_HEADS = 4
DSA_HEADS = 8
IDX_HEADS = 4
DSA_TOPK_MAX = 256
FOX_HEADS = 8
N_BRANCH = 3
BRANCH_WIDTH = 512
N_EXPERTS = 32
N_GROUPS = 4
EXPERTS_PER_GROUP = N_EXPERTS // N_GROUPS
TOP_K = 2

LANES = 128
NEG = -1e30
INT_MIN = -2 ** 31

QK_DQ, QK_DK, QK_SQ, QK_SK, QK_IQ, QK_IK = 0, 512, 1024, 1536, 2048, 2304
QK_WIDTH = 2560
PV_DV, PV_SV, PV_FQ, PV_FK, PV_FV = 0, 512, 1024, 1536, 2048
PV_WIDTH = 2560
SM_IW, SM_FF = 0, 4
VMEM_LIMIT = 56 * 1024 * 1024


def _cparams(sem):
    return pltpu.CompilerParams(dimension_semantics=sem, vmem_limit_bytes=VMEM_LIMIT)


def _ln_rows(x, g, b):
    mu = jnp.mean(x, axis=-1, keepdims=True)
    xc = x - mu
    var = jnp.mean(xc * xc, axis=-1, keepdims=True)
    return xc * lax.rsqrt(var + LN_EPS) * g + b


def _ln_body(x_ref, g_ref, b_ref, o32_ref, o16_ref):
    y = _ln_rows(x_ref[...], g_ref[...], b_ref[...])
    o32_ref[...] = y
    o16_ref[...] = y.astype(BF16)


def _layer_norm(x, g, b, tm=512):
    T, D = x.shape
    return pl.pallas_call(
        _ln_body,
        out_shape=(jax.ShapeDtypeStruct((T, D), F32), jax.ShapeDtypeStruct((T, D), BF16)),
        grid=(T // tm,),
        in_specs=[pl.BlockSpec((tm, D), lambda i: (i, 0)),
                  pl.BlockSpec((1, D), lambda i: (0, 0)),
                  pl.BlockSpec((1, D), lambda i: (0, 0))],
        out_specs=(pl.BlockSpec((tm, D), lambda i: (i, 0)), pl.BlockSpec((tm, D), lambda i: (i, 0))),
        compiler_params=_cparams(("arbitrary",)),
        name="layer_norm",
    )(x, g.reshape(1, D), b.reshape(1, D))


def _proj_body(mode, width, h_ref, w_ref, *rest):
    if mode == "rope":
        wrot_ref, cc_ref, ss_ref, o_ref = rest
    else:
        (o_ref,) = rest
    h = h_ref[...]
    tn = 512 if width % 512 == 0 else width
    for j in range(width // tn):
        cs = slice(j * tn, (j + 1) * tn)
        z = jnp.dot(h, w_ref[:, cs], preferred_element_type=F32)
        if mode == "rope":
            zr = jnp.dot(h, wrot_ref[:, cs], preferred_element_type=F32)
            cc = cc_ref[...]
            ss = ss_ref[...]
            for i in range(tn // LANES):
                ls = slice(i * LANES, (i + 1) * LANES)
                o_ref[:, j * tn + i * LANES:j * tn + (i + 1) * LANES] = (
                    z[:, ls] * cc + zr[:, ls] * ss).astype(o_ref.dtype)
        elif mode == "sigmoid":
            o_ref[:, cs] = (1.0 / (1.0 + jnp.exp(-z))).astype(o_ref.dtype)
        else:
            o_ref[:, cs] = z.astype(o_ref.dtype)


def _project(h16, w, mode, out_dtype, tables=None, wrot=None, tm=512):
    T, D = h16.shape
    width = w.shape[1]
    row = lambda i: (i, 0)
    fixed = lambda i: (0, 0)
    in_specs = [pl.BlockSpec((tm, D), row), pl.BlockSpec((D, width), fixed)]
    args = [h16, w]
    if mode == "rope":
        cc, ss = tables
        in_specs += [pl.BlockSpec((D, width), fixed), pl.BlockSpec((tm, LANES), row),
                     pl.BlockSpec((tm, LANES), row)]
        args += [wrot, cc, ss]
    return pl.pallas_call(
        functools.partial(_proj_body, mode, width),
        out_shape=jax.ShapeDtypeStruct((T, width), out_dtype),
        grid=(T // tm,),
        in_specs=in_specs,
        out_specs=pl.BlockSpec((tm, width), row),
        compiler_params=_cparams(("arbitrary",)),
        name="proj_" + mode,
    )(*args)


def _cumsum_body(x_ref, bias_ref, tri_ref, c_ref, carry_ref):
    @pl.when(pl.program_id(1) == 0)
    def _():
        carry_ref[...] = jnp.zeros_like(carry_ref)

    x = x_ref[...] + bias_ref[...]
    lf = jnp.minimum(x, 0.0) - jnp.log1p(jnp.exp(-jnp.abs(x)))
    hi = lf.astype(BF16)
    r1 = lf - hi.astype(F32)
    mid = r1.astype(BF16)
    lo = (r1 - mid.astype(F32)).astype(BF16)
    tri = tri_ref[...]
    c = (jnp.dot(tri, hi, preferred_element_type=F32) + jnp.dot(tri, mid, preferred_element_type=F32)
         + jnp.dot(tri, lo, preferred_element_type=F32)) + carry_ref[...]
    c_ref[...] = c
    carry_ref[...] = c[-1:, :]


def _forget_cumsum(small, bias_row, B, S, tb=512):
    T = small.shape[0]
    nb = S // tb
    tri = jnp.tril(jnp.ones((tb, tb), F32)).astype(BF16)
    return pl.pallas_call(
        _cumsum_body,
        out_shape=jax.ShapeDtypeStruct((T, LANES), F32),
        grid=(B, nb),
        in_specs=[pl.BlockSpec((tb, LANES), lambda b, i: (b * nb + i, 0)),
                  pl.BlockSpec((1, LANES), lambda b, i: (0, 0)),
                  pl.BlockSpec((tb, tb), lambda b, i: (0, 0))],
        out_specs=pl.BlockSpec((tb, LANES), lambda b, i: (b * nb + i, 0)),
        scratch_shapes=[pltpu.VMEM((1, LANES), F32)],
        compiler_params=_cparams(("arbitrary", "arbitrary")),
        name="forget_cumsum",
    )(small, bias_row, tri)


def _attn_body(mode, nslab, lam_init, qt_ref, kt_ref, q_ref, k_ref, v_ref, *rest):
    if mode == "diff":
        lam_ref, g_ref, o_ref, m_sc, l_sc, acc_sc = rest
    elif mode == "dsa":
        mask_ref, o_ref, m_sc, l_sc, acc_sc = rest
    else:
        c_ref, ct_ref, o_ref, m_sc, l_sc, acc_sc = rest
    p = pl.program_id(1)
    qi = qt_ref[p]
    ki = kt_ref[p]
    tq = q_ref.shape[0]
    tk = k_ref.shape[0]

    @pl.when(ki == 0)
    def _():
        m_sc[...] = jnp.full_like(m_sc, NEG)
        l_sc[...] = jnp.zeros_like(l_sc)
        acc_sc[...] = jnp.zeros_like(acc_sc)

    lane = lax.broadcasted_iota(jnp.int32, (1, LANES), 1)

    def tile_update(diag):
        if mode == "dsa":
            keep_all = mask_ref[...] != 0
        elif diag:
            row = lax.broadcasted_iota(jnp.int32, (tq, tk), 0)
            col = lax.broadcasted_iota(jnp.int32, (tq, tk), 1)
            if mode == "diff":
                keep_all = (col >> 6) <= (row >> 6)
            else:
                keep_all = col <= row
        for sl in range(nslab):
            ls = slice(sl * LANES, (sl + 1) * LANES)
            q = q_ref[:, ls]
            k = k_ref[:, ls]
            v = v_ref[:, ls]
            for sub in range(2):
                idx = 2 * sl + sub
                half = (lane < HEAD_DIM) if sub == 0 else (lane >= HEAD_DIM)
                qm = jnp.where(half, q, jnp.zeros_like(q))
                s = lax.dot_general(qm, k, (((1,), (1,)), ((), ())), preferred_element_type=F32)
                if mode == "fox":
                    cq = c_ref[:, SM_FF + idx:SM_FF + idx + 1]
                    ck = ct_ref[0, idx:idx + 1, :]
                    s = s + cq - ck
                if mode == "dsa" or diag:
                    s = jnp.where(keep_all, s, NEG)
                m_old = m_sc[idx]
                m_new = jnp.maximum(m_old, jnp.max(s, axis=-1, keepdims=True))
                alpha = jnp.exp(m_old - m_new)
                pr = jnp.exp(s - m_new)
                l_sc[idx] = alpha * l_sc[idx] + jnp.sum(pr, axis=-1, keepdims=True)
                acc_sc[idx] = alpha * acc_sc[idx] + jnp.dot(pr.astype(BF16), v, preferred_element_type=F32)
                m_sc[idx] = m_new

    if mode == "dsa":
        tile_update(False)
    else:
        @pl.when(ki == qi)
        def _():
            tile_update(True)

        @pl.when(ki != qi)
        def _():
            tile_update(False)

    @pl.when(ki == qi)
    def _():
        if mode == "diff":
            lm = lam_ref[...]
            lam = (jnp.exp(jnp.sum(lm[0:1] * lm[1:2], keepdims=True))
                   - jnp.exp(jnp.sum(lm[2:3] * lm[3:4], keepdims=True)) + lam_init)
        for sl in range(nslab):
            o0 = acc_sc[2 * sl] / l_sc[2 * sl]
            o1 = acc_sc[2 * sl + 1] / l_sc[2 * sl + 1]
            if mode == "diff":
                o = o0 - lam * o1
                o = o * lax.rsqrt(jnp.mean(o * o, axis=-1, keepdims=True) + LN_EPS)
                o = (o * g_ref[...]) * (1.0 - lam_init)
            else:
                o = jnp.where(lane < HEAD_DIM, o0, o1)
            o_ref[:, sl * LANES:(sl + 1) * LANES] = o.astype(o_ref.dtype)


def _attention(mode, qarr, qcol, karr, kcol, varr, vcol, B, S, extra, lam_init=0.0, tile=512):
    nslab = 4
    width = nslab * LANES
    nq = S // tile
    pairs = [(qi, ki) for qi in range(nq) for ki in range(qi + 1)]
    qt = jnp.asarray([pq for pq, _ in pairs], jnp.int32)
    kt = jnp.asarray([pk for _, pk in pairs], jnp.int32)
    T = B * S
    qmap = lambda b, p, qt, kt: (b * nq + qt[p], qcol // width)
    kmap = lambda b, p, qt, kt: (b * nq + kt[p], kcol // width)
    vmap = lambda b, p, qt, kt: (b * nq + kt[p], vcol // width)
    in_specs = [pl.BlockSpec((tile, width), qmap), pl.BlockSpec((tile, width), kmap),
                pl.BlockSpec((tile, width), vmap)]
    args = [qarr, karr, varr]
    if mode == "diff":
        lam4, g = extra
        in_specs += [pl.BlockSpec((4, HEAD_DIM), lambda b, p, qt, kt: (0, 0)),
                     pl.BlockSpec((1, LANES), lambda b, p, qt, kt: (0, 0))]
        args += [lam4, g.reshape(1, LANES)]
    elif mode == "dsa":
        (mask,) = extra
        in_specs += [pl.BlockSpec((tile, tile), lambda b, p, qt, kt: (b * nq + qt[p], kt[p]))]
        args += [mask]
    else:
        c, ct = extra
        in_specs += [pl.BlockSpec((tile, LANES), lambda b, p, qt, kt: (b * nq + qt[p], 0)),
                     pl.BlockSpec((1, 8, tile), lambda b, p, qt, kt: (b, 0, kt[p]))]
        args += [c, ct]
    return pl.pallas_call(
        functools.partial(_attn_body, mode, nslab, lam_init),
        out_shape=jax.ShapeDtypeStruct((T, width), BF16),
        grid_spec=pltpu.PrefetchScalarGridSpec(
            num_scalar_prefetch=2,
            grid=(B, len(pairs)),
            in_specs=in_specs,
            out_specs=pl.BlockSpec((tile, width), lambda b, p, qt, kt: (b * nq + qt[p], 0)),
            scratch_shapes=[pltpu.VMEM((2 * nslab, tile, 1), F32), pltpu.VMEM((2 * nslab, tile, 1), F32),
                            pltpu.VMEM((2 * nslab, tile, LANES), F32)]),
        compiler_params=_cparams(("arbitrary", "arbitrary")),
        name="attn_" + mode,
    )(qt, kt, *args)


def _select_body(topk, iq_ref, ik_ref, iw_ref, tri_ref, mask_ref, keys_ref):
    qi = pl.program_id(1)
    tq = iq_ref.shape[0]
    tk = tq
    nkb = qi + 1
    q0 = qi * tq
    lane = lax.broadcasted_iota(jnp.int32, (1, iq_ref.shape[1]), 1)
    iq = iq_ref[...]
    qms = [jnp.where((lane >> 6) == h, iq, jnp.zeros_like(iq)) for h in range(IDX_HEADS)]
    iw = iw_ref[...]
    row_chunk = (q0 + lax.broadcasted_iota(jnp.int32, (tq, tk), 0)) >> 6
    col_iota = lax.broadcasted_iota(jnp.int32, (tq, tk), 1)

    mask_ref[...] = jnp.zeros_like(mask_ref)

    def score_block(j, carry):
        k0 = pl.multiple_of(j * tk, tk)
        ik = ik_ref[pl.ds(k0, tk), :]
        score = jnp.zeros((tq, tk), F32)
        for h in range(IDX_HEADS):
            lg = lax.dot_general(qms[h], ik, (((1,), (1,)), ((), ())), preferred_element_type=F32)
            score = score + iw[:, SM_IW + h:SM_IW + h + 1] * jnp.maximum(lg, 0.0)
        score = jnp.where(score == 0.0, 0.0, score)
        bits = lax.bitcast_convert_type(score, jnp.int32)
        key = bits ^ ((bits >> 31) & jnp.int32(0x7FFFFFFF))
        vis = ((k0 + col_iota) >> 6) <= row_chunk
        keys_ref[:, pl.ds(k0, tk)] = jnp.where(vis, key, jnp.int32(INT_MIN))
        return carry

    lax.fori_loop(0, nkb, score_block, 0)

    def count_ge(cand):
        def body(j, acc):
            k0 = pl.multiple_of(j * tk, tk)
            blk = keys_ref[:, pl.ds(k0, tk)]
            ge = jnp.where(blk >= cand, 1, 0).astype(jnp.int32)
            for i in range(tk // LANES):
                acc = acc + ge[:, i * LANES:(i + 1) * LANES]
            return acc
        acc = lax.fori_loop(0, nkb, body, jnp.zeros((tq, LANES), jnp.int32))
        return jnp.sum(acc, axis=1, keepdims=True)

    def bit_step(i, lo):
        cand = lo + (jnp.int32(1) << (31 - i))
        return jnp.where(count_ge(cand) >= topk, cand, lo)

    thr = lax.fori_loop(0, 32, bit_step, jnp.full((tq, 1), INT_MIN, jnp.int32))
    thr = jnp.maximum(thr, jnp.int32(INT_MIN + 1))
    need = (topk - count_ge(thr + 1)).astype(F32)
    tri = tri_ref[...]

    def select_block(j, carry):
        k0 = pl.multiple_of(j * tk, tk)
        key = keys_ref[:, pl.ds(k0, tk)]
        eq = key == thr
        incl = jnp.dot(jnp.where(eq, 1.0, 0.0).astype(BF16), tri, preferred_element_type=F32)
        sel = (key > thr) | (eq & ((carry + incl) <= need))
        mask_ref[:, pl.ds(k0, tk)] = jnp.where(sel, 1, 0).astype(jnp.int8)
        return carry + incl[:, tk - 1:tk]

    lax.fori_loop(0, nkb, select_block, jnp.zeros((tq, 1), F32))


def _dsa_select(qk, small, B, S, topk, tq=256):
    T = B * S
    nq = S // tq
    tri = jnp.triu(jnp.ones((tq, tq), F32)).astype(BF16)
    return pl.pallas_call(
        functools.partial(_select_body, topk),
        out_shape=jax.ShapeDtypeStruct((T, S), jnp.int8),
        grid=(B, nq),
        in_specs=[pl.BlockSpec((tq, 256), lambda b, i: (b * nq + i, QK_IQ // 256)),
                  pl.BlockSpec((S, 256), lambda b, i: (b, QK_IK // 256)),
                  pl.BlockSpec((tq, LANES), lambda b, i: (b * nq + i, 0)),
                  pl.BlockSpec((tq, tq), lambda b, i: (0, 0))],
        out_specs=pl.BlockSpec((tq, S), lambda b, i: (b * nq + i, 0)),
        scratch_shapes=[pltpu.VMEM((tq, S), jnp.int32)],
        compiler_params=_cparams(("arbitrary", "arbitrary")),
        name="dsa_select",
    )(qk, qk, small, tri)


def _merge_body(alpha, od_ref, os_ref, of_ref, gate_ref, h_ref, wbr_ref, wo_ref, g_ref, b_ref, wr_ref,
                h32_ref, h16_ref, aff_ref):
    D = h_ref.shape[1]
    y = None
    for i, o_ref in enumerate((od_ref, os_ref, of_ref)):
        br = jnp.dot(o_ref[...], wbr_ref[i], preferred_element_type=F32)
        t = gate_ref[:, i * D:(i + 1) * D].astype(F32) * br
        y = t if y is None else y + t
    m = jnp.dot(y.astype(BF16), wo_ref[...], preferred_element_type=F32)
    hn = _ln_rows(alpha * h_ref[...] + m, g_ref[...], b_ref[...])
    h32_ref[...] = hn
    h16 = hn.astype(BF16)
    h16_ref[...] = h16
    logits = lax.dot_general(wr_ref[...], h16, (((1,), (1,)), ((), ())), preferred_element_type=F32)
    aff_ref[...] = 1.0 / (1.0 + jnp.exp(-logits))


def _merge(alpha, od, os_, of, gates, h32, wbr, wo, g, b, wr_t, tm=512):
    T, D = h32.shape
    bw = od.shape[1]
    row = lambda i: (i, 0)
    fixed2 = lambda i: (0, 0)
    return pl.pallas_call(
        functools.partial(_merge_body, alpha),
        out_shape=(jax.ShapeDtypeStruct((T, D), F32), jax.ShapeDtypeStruct((T, D), BF16),
                   jax.ShapeDtypeStruct((N_EXPERTS, T), F32)),
        grid=(T // tm,),
        in_specs=[pl.BlockSpec((tm, bw), row), pl.BlockSpec((tm, bw), row), pl.BlockSpec((tm, bw), row),
                  pl.BlockSpec((tm, N_BRANCH * D), row), pl.BlockSpec((tm, D), row),
                  pl.BlockSpec((N_BRANCH, bw, D), lambda i: (0, 0, 0)), pl.BlockSpec((D, D), fixed2),
                  pl.BlockSpec((1, D), fixed2), pl.BlockSpec((1, D), fixed2),
                  pl.BlockSpec((N_EXPERTS, D), fixed2)],
        out_specs=(pl.BlockSpec((tm, D), row), pl.BlockSpec((tm, D), row),
                   pl.BlockSpec((N_EXPERTS, tm), lambda i: (0, i))),
        compiler_params=_cparams(("arbitrary",)),
        name="merge",
    )(od, os_, of, gates, h32, wbr, wo, g.reshape(1, D), b.reshape(1, D), wr_t)


def _route_body(aff_ref, bias_ref, e_ref, w_ref):
    aff = aff_ref[...]
    sel = aff + bias_ref[...]
    sub = lax.broadcasted_iota(jnp.int32, (EXPERTS_PER_GROUP, aff.shape[1]), 0)
    big = jnp.int32(EXPERTS_PER_GROUP)
    best = None
    for gidx in range(N_GROUPS):
        rs = slice(gidx * EXPERTS_PER_GROUP, (gidx + 1) * EXPERTS_PER_GROUP)
        s8 = sel[rs]
        a8 = aff[rs]
        m1 = jnp.max(s8, axis=0, keepdims=True)
        i1 = jnp.min(jnp.where(s8 == m1, sub, big), axis=0, keepdims=True)
        rest = jnp.where(sub == i1, -jnp.inf, s8)
        m2 = jnp.max(rest, axis=0, keepdims=True)
        i2 = jnp.min(jnp.where(rest == m2, sub, big), axis=0, keepdims=True)
        a1 = jnp.sum(jnp.where(sub == i1, a8, 0.0), axis=0, keepdims=True)
        a2 = jnp.sum(jnp.where(sub == i2, a8, 0.0), axis=0, keepdims=True)
        score = m1 + m2
        cand = (score, i1 + gidx * EXPERTS_PER_GROUP, i2 + gidx * EXPERTS_PER_GROUP, a1, a2)
        if best is None:
            best = cand
        else:
            take = score > best[0]
            best = tuple(jnp.where(take, c, o) for c, o in zip(cand, best))
    _, e1, e2, a1, a2 = best
    tot = a1 + a2
    e_ref[0:1, :] = e1
    e_ref[1:2, :] = e2
    w_ref[0:1, :] = a1 / tot
    w_ref[1:2, :] = a2 / tot


def _route(aff_t, b_router, tn=2048):
    E, T = aff_t.shape
    tn = min(tn, T)
    return pl.pallas_call(
        _route_body,
        out_shape=(jax.ShapeDtypeStruct((TOP_K, T), jnp.int32), jax.ShapeDtypeStruct((TOP_K, T), F32)),
        grid=(T // tn,),
        in_specs=[pl.BlockSpec((E, tn), lambda i: (0, i)), pl.BlockSpec((E, 1), lambda i: (0, 0))],
        out_specs=(pl.BlockSpec((TOP_K, tn), lambda i: (0, i)), pl.BlockSpec((TOP_K, tn), lambda i: (0, i))),
        compiler_params=_cparams(("arbitrary",)),
        name="route",
    )(aff_t, b_router.reshape(E, 1))


def _expert_body(be_ref, bc_ref, tok_ref, slot_ref, x_hbm, wg_ref, wu_ref, wd_ref, y_hbm,
                 xbuf, ybuf, gsem, ssem):
    blk = pl.program_id(0)
    cnt = bc_ref[blk]

    @pl.when(blk == 0)
    def _():
        xbuf[...] = jnp.zeros_like(xbuf)

    def gather_copy(r):
        return pltpu.make_async_copy(x_hbm.at[pl.ds(tok_ref[0, 0, r], 1), :], xbuf.at[pl.ds(r, 1), :], gsem)

    def scatter_copy(r):
        return pltpu.make_async_copy(ybuf.at[pl.ds(r, 1), :], y_hbm.at[pl.ds(slot_ref[0, 0, r], 1), :], ssem)

    @pl.when(cnt > 0)
    def _():
        def g_start(r, c):
            gather_copy(r).start()
            return c

        def g_wait(r, c):
            gather_copy(r).wait()
            return c

        lax.fori_loop(0, cnt, g_start, 0)
        lax.fori_loop(0, cnt, g_wait, 0)
        x = xbuf[...].astype(BF16)
        gate = jnp.dot(x, wg_ref[0], preferred_element_type=F32)
        up = jnp.dot(x, wu_ref[0], preferred_element_type=F32)
        hmid = (gate * (1.0 / (1.0 + jnp.exp(-gate)))) * up
        ybuf[...] = jnp.dot(hmid.astype(BF16), wd_ref[0], preferred_element_type=F32)

        def s_start(r, c):
            scatter_copy(r).start()
            return c

        def s_wait(r, c):
            scatter_copy(r).wait()
            return c

        lax.fori_loop(0, cnt, s_start, 0)
        lax.fori_loop(0, cnt, s_wait, 0)


def _experts(h32, blk_e, blk_cnt, tok_pad, slot_pad, wg, wu, wd, bm):
    T, D = h32.shape
    nb = blk_e.shape[0]
    de = wg.shape[2]
    return pl.pallas_call(
        _expert_body,
        out_shape=jax.ShapeDtypeStruct((TOP_K * T, D), F32),
        grid_spec=pltpu.PrefetchScalarGridSpec(
            num_scalar_prefetch=2,
            grid=(nb,),
            in_specs=[pl.BlockSpec((1, 1, bm), lambda i, be, bc: (i, 0, 0), memory_space=pltpu.SMEM),
                      pl.BlockSpec((1, 1, bm), lambda i, be, bc: (i, 0, 0), memory_space=pltpu.SMEM),
                      pl.BlockSpec(memory_space=pl.ANY),
                      pl.BlockSpec((1, D, de), lambda i, be, bc: (be[i], 0, 0)),
                      pl.BlockSpec((1, D, de), lambda i, be, bc: (be[i], 0, 0)),
                      pl.BlockSpec((1, de, D), lambda i, be, bc: (be[i], 0, 0))],
            out_specs=pl.BlockSpec(memory_space=pl.ANY),
            scratch_shapes=[pltpu.VMEM((bm, D), F32), pltpu.VMEM((bm, D), F32),
                            pltpu.SemaphoreType.DMA, pltpu.SemaphoreType.DMA]),
        compiler_params=_cparams(("arbitrary",)),
        name="experts",
    )(blk_e, blk_cnt, tok_pad, slot_pad, h32, wg, wu, wd)


def _dispatch_tables(e_t, T, bm):
    A = TOP_K * T
    flat_e = e_t.T.reshape(A)
    order = jnp.argsort(flat_e, stable=True).astype(jnp.int32)
    counts = jnp.bincount(flat_e, length=N_EXPERTS).astype(jnp.int32)
    start = jnp.cumsum(counts) - counts
    nblk_e = (counts + bm - 1) // bm
    blk_end = jnp.cumsum(nblk_e)
    nb = A // bm + N_EXPERTS
    bidx = jnp.arange(nb, dtype=jnp.int32)
    blk_e = jnp.minimum(jnp.searchsorted(blk_end, bidx, side="right"), N_EXPERTS - 1).astype(jnp.int32)
    off = (bidx - (blk_end - nblk_e)[blk_e]) * bm
    blk_cnt = jnp.where(bidx < blk_end[-1], jnp.clip(counts[blk_e] - off, 0, bm), 0).astype(jnp.int32)
    pos = jnp.clip((start[blk_e] + off)[:, None] + jnp.arange(bm, dtype=jnp.int32)[None, :], 0, A - 1)
    slot_pad = order[pos]
    tok_pad = slot_pad // TOP_K
    return blk_e, blk_cnt, tok_pad.reshape(nb, 1, bm), slot_pad.reshape(nb, 1, bm)


def _combine_body(alpha, h_ref, y_ref, w_ref, g_ref, b_ref, o32_ref, o16_ref):
    D = h_ref.shape[1]
    w = w_ref[...]
    f = w[:, 0:1] * y_ref[:, 0:D] + w[:, 1:2] * y_ref[:, D:2 * D]
    hn = _ln_rows(alpha * h_ref[...] + f, g_ref[...], b_ref[...])
    o32_ref[...] = hn
    o16_ref[...] = hn.astype(BF16)


def _combine(alpha, h32, y2, w_rows, g, b, tm=512):
    T, D = h32.shape
    row = lambda i: (i, 0)
    fixed = lambda i: (0, 0)
    return pl.pallas_call(
        functools.partial(_combine_body, alpha),
        out_shape=(jax.ShapeDtypeStruct((T, D), F32), jax.ShapeDtypeStruct((T, D), BF16)),
        grid=(T // tm,),
        in_specs=[pl.BlockSpec((tm, D), row), pl.BlockSpec((tm, TOP_K * D), row),
                  pl.BlockSpec((tm, TOP_K), row), pl.BlockSpec((1, D), fixed), pl.BlockSpec((1, D), fixed)],
        out_specs=(pl.BlockSpec((tm, D), row), pl.BlockSpec((tm, D), row)),
        compiler_params=_cparams(("arbitrary",)),
        name="combine",
    )(h32, y2.reshape(T, TOP_K * D), w_rows, g.reshape(1, D), b.reshape(1, D))


def _rotate_half_columns(w):
    d = w.shape[0]
    w4 = w.reshape(d, -1, 2, HEAD_DIM // 2)
    return jnp.concatenate([-w4[:, :, 1:2], w4[:, :, 0:1]], axis=2).reshape(w.shape)


def _prep_in_weights(w_in_l):
    qk_w = DIFF_HEADS * 2 * HEAD_DIM
    sizes = (qk_w, qk_w, DIFF_HEADS * 2 * HEAD_DIM, 512, 512, 512, IDX_HEADS * HEAD_DIM, HEAD_DIM, IDX_HEADS,
             512, 512, 512, FOX_HEADS, N_BRANCH * w_in_l.shape[0])
    offs = [0]
    for s in sizes:
        offs.append(offs[-1] + s)
    dq, dk, dv, sq, sk, sv, iq, ik, iw, fq, fk, fv, ff, g = (w_in_l[:, offs[i]:offs[i + 1]] for i in range(len(sizes)))
    scale = HEAD_DIM ** -0.5
    idx_scale = (IDX_HEADS * HEAD_DIM) ** -0.5
    w_rope = jnp.concatenate([dq * scale, dk, sq * scale, sk, iq, ik, ik, ik, ik], axis=1)
    w_rot = _rotate_half_columns(w_rope)
    w_plain = jnp.concatenate([dv, sv, fq * scale, fk, fv], axis=1)
    pad = jnp.zeros((w_in_l.shape[0], LANES - IDX_HEADS - FOX_HEADS), w_in_l.dtype)
    w_small = jnp.concatenate([iw * idx_scale, ff, pad], axis=1)
    return (w_rope.astype(BF16), w_rot.astype(BF16), w_plain.astype(BF16), g.astype(BF16), w_small.astype(BF16))


def _rope_tables(positions):
    inv = ROPE_THETA ** (-jnp.arange(0, HEAD_DIM, 2, dtype=F32) / HEAD_DIM)
    ang = positions.astype(F32).reshape(-1, 1) * inv
    c, s = jnp.cos(ang), jnp.sin(ang)
    return jnp.tile(c, (1, LANES // (HEAD_DIM // 2))), jnp.tile(s, (1, LANES // (HEAD_DIM // 2)))


def kernel(x, positions, ln_in_g, ln_in_b, w_in, b_forget, diff_lam, diff_norm_g, w_br, w_o, ln1_g, ln1_b,
           w_router, b_router, w_gate, w_up, w_down, ln2_g, ln2_b):
    B, S, D = x.shape
    T = B * S
    depth = w_in.shape[0]
    alpha = (2 * depth) ** 0.25
    topk = min(DSA_TOPK_MAX, S // 4)
    bm = 256
    cc, ss = _rope_tables(positions)
    wr_t = w_router.T.astype(BF16)
    h32, h16 = _layer_norm(x.reshape(T, D), ln_in_g, ln_in_b)
    for l in range(depth):
        lam_init = 0.8 - 0.6 * math.exp(-0.3 * l)
        w_rope, w_rot, w_plain, w_g, w_small = _prep_in_weights(w_in[l])
        qk = _project(h16, w_rope, "rope", BF16, tables=(cc, ss), wrot=w_rot)
        pv = _project(h16, w_plain, "plain", BF16)
        gates = _project(h16, w_g, "sigmoid", BF16)
        small = _project(h16, w_small, "f32", F32)

        o_diff = _attention("diff", qk, QK_DQ, qk, QK_DK, pv, PV_DV, B, S,
                            (diff_lam[l], diff_norm_g[l]), lam_init=lam_init)
        mask = _dsa_select(qk, small, B, S, topk)
        o_dsa = _attention("dsa", qk, QK_SQ, qk, QK_SK, pv, PV_SV, B, S, (mask,))
        bias_row = jnp.zeros((1, LANES), F32).at[0, SM_FF:SM_FF + FOX_HEADS].set(b_forget[l])
        c = _forget_cumsum(small, bias_row, B, S)
        ct = jnp.transpose(c[:, SM_FF:SM_FF + FOX_HEADS].reshape(B, S, FOX_HEADS), (0, 2, 1))
        o_fox = _attention("fox", pv, PV_FQ, pv, PV_FK, pv, PV_FV, B, S, (c, ct))

        h32, h16, aff_t = _merge(alpha, o_diff, o_dsa, o_fox, gates, h32, w_br[l].astype(BF16),
                                 w_o[l].astype(BF16), ln1_g[l], ln1_b[l], wr_t)
        e_t, w_t = _route(aff_t, b_router)
        blk_e, blk_cnt, tok_pad, slot_pad = _dispatch_tables(e_t, T, bm)
        y2 = _experts(h32, blk_e, blk_cnt, tok_pad, slot_pad, w_gate[l].astype(BF16), w_up[l].astype(BF16),
                      w_down[l].astype(BF16), bm)
        h32, h16 = _combine(alpha, h32, y2, w_t.T, ln2_g[l], ln2_b[l])
    return h32.reshape(B, S, D)
```

```python
import functools
import math

import jax
import jax.numpy as jnp
from jax import lax
from jax.experimental import pallas as pl
from jax.experimental.pallas import tpu as pltpu

F32 = jnp.float32
BF16 = jnp.bfloat16

CHUNK = 64
HEAD_DIM = 64
ROPE_THETA = 10000.0
LN_EPS = 1e-5
DIFF_HEADS = 4
DSA_HEADS = 8
IDX_HEADS = 4
DSA_TOPK_MAX = 256
FOX_HEADS = 8
N_BRANCH = 3
BRANCH_WIDTH = 512
N_EXPERTS = 32
N_GROUPS = 4
EXPERTS_PER_GROUP = N_EXPERTS // N_GROUPS
TOP_K = 2

LANES = 128
NEG = -1e30
INT_MIN = -2 ** 31

QK_DQ, QK_DK, QK_SQ, QK_SK, QK_IQ, QK_IK = 0, 512, 1024, 1536, 2048, 2304
QK_WIDTH = 2560
FQK_FQ, FQK_FK = 0, 512
VT_DV, VT_SV, VT_FV = 0, 512, 1024
SM_IW, SM_FF = 0, 4
VMEM_LIMIT = 56 * 1024 * 1024


def _cparams(sem):
    return pltpu.CompilerParams(dimension_semantics=sem, vmem_limit_bytes=VMEM_LIMIT)


def _ln_rows(x, g, b):
    mu = jnp.mean(x, axis=-1, keepdims=True)
    xc = x - mu
    var = jnp.mean(xc * xc, axis=-1, keepdims=True)
    return xc * lax.rsqrt(var + LN_EPS) * g + b


def _ln_body(x_ref, g_ref, b_ref, o32_ref, o16_ref):
    y = _ln_rows(x_ref[...], g_ref[...], b_ref[...])
    o32_ref[...] = y
    o16_ref[...] = y.astype(BF16)


def _layer_norm(x, g, b, tm=512):
    T, D = x.shape
    return pl.pallas_call(
        _ln_body,
        out_shape=(jax.ShapeDtypeStruct((T, D), F32), jax.ShapeDtypeStruct((T, D), BF16)),
        grid=(T // tm,),
        in_specs=[pl.BlockSpec((tm, D), lambda i: (i, 0)),
                  pl.BlockSpec((1, D), lambda i: (0, 0)),
                  pl.BlockSpec((1, D), lambda i: (0, 0))],
        out_specs=(pl.BlockSpec((tm, D), lambda i: (i, 0)), pl.BlockSpec((tm, D), lambda i: (i, 0))),
        compiler_params=_cparams(("arbitrary",)),
        name="layer_norm",
    )(x, g.reshape(1, D), b.reshape(1, D))


def _proj_body(mode, width, h_ref, w_ref, *rest):
    if mode == "rope":
        wrot_ref, cc_ref, ss_ref, o_ref = rest
    else:
        (o_ref,) = rest
    h = h_ref[...]
    tn = 512 if width % 512 == 0 else width
    for j in range(width // tn):
        cs = slice(j * tn, (j + 1) * tn)
        if mode == "transposed":
            o_ref[cs, :] = lax.dot_general(w_ref[cs, :], h, (((1,), (1,)), ((), ())),
                                           preferred_element_type=F32).astype(o_ref.dtype)
            continue
        z = jnp.dot(h, w_ref[:, cs], preferred_element_type=F32)
        if mode == "rope":
            zr = jnp.dot(h, wrot_ref[:, cs], preferred_element_type=F32)
            cc = cc_ref[...]
            ss = ss_ref[...]
            for i in range(tn // LANES):
                ls = slice(i * LANES, (i + 1) * LANES)
                o_ref[:, j * tn + i * LANES:j * tn + (i + 1) * LANES] = (
                    z[:, ls] * cc + zr[:, ls] * ss).astype(o_ref.dtype)
        elif mode == "sigmoid":
            o_ref[:, cs] = (1.0 / (1.0 + jnp.exp(-z))).astype(o_ref.dtype)
        else:
            o_ref[:, cs] = z.astype(o_ref.dtype)


def _project(h16, w, mode, out_dtype, tables=None, wrot=None, tm=512):
    T, D = h16.shape
    transposed = mode == "transposed"
    width = w.shape[0] if transposed else w.shape[1]
    row = lambda i: (i, 0)
    fixed = lambda i: (0, 0)
    in_specs = [pl.BlockSpec((tm, D), row), pl.BlockSpec(w.shape, fixed)]
    args = [h16, w]
    if mode == "rope":
        cc, ss = tables
        in_specs += [pl.BlockSpec((D, width), fixed), pl.BlockSpec((tm, LANES), row),
                     pl.BlockSpec((tm, LANES), row)]
        args += [wrot, cc, ss]
    return pl.pallas_call(
        functools.partial(_proj_body, mode, width),
        out_shape=jax.ShapeDtypeStruct((width, T) if transposed else (T, width), out_dtype),
        grid=(T // tm,),
        in_specs=in_specs,
        out_specs=pl.BlockSpec((width, tm), lambda i: (0, i)) if transposed else pl.BlockSpec((tm, width), row),
        compiler_params=_cparams(("arbitrary",)),
        name="proj_" + mode,
    )(*args)


def _cumsum_body(x_ref, bias_ref, tri_ref, c_ref, crep_ref, carry_ref):
    @pl.when(pl.program_id(1) == 0)
    def _():
        carry_ref[...] = jnp.zeros_like(carry_ref)

    x = x_ref[...] + bias_ref[...]
    lf = jnp.minimum(x, 0.0) - jnp.log1p(jnp.exp(-jnp.abs(x)))
    hi = lf.astype(BF16)
    r1 = lf - hi.astype(F32)
    mid = r1.astype(BF16)
    lo = (r1 - mid.astype(F32)).astype(BF16)
    tri = tri_ref[...]
    c = (jnp.dot(tri, hi, preferred_element_type=F32) + jnp.dot(tri, mid, preferred_element_type=F32)
         + jnp.dot(tri, lo, preferred_element_type=F32)) + carry_ref[...]
    c_ref[...] = c
    carry_ref[...] = c[-1:, :]
    for h in range(FOX_HEADS):
        crep_ref[:, h * LANES:(h + 1) * LANES] = jnp.broadcast_to(c[:, SM_FF + h:SM_FF + h + 1], (c.shape[0], LANES))


def _forget_cumsum(small, bias_row, B, S, tb=512):
    T = small.shape[0]
    nb = S // tb
    tri = jnp.tril(jnp.ones((tb, tb), F32)).astype(BF16)
    return pl.pallas_call(
        _cumsum_body,
        out_shape=(jax.ShapeDtypeStruct((T, LANES), F32), jax.ShapeDtypeStruct((T, FOX_HEADS * LANES), F32)),
        grid=(B, nb),
        in_specs=[pl.BlockSpec((tb, LANES), lambda b, i: (b * nb + i, 0)),
                  pl.BlockSpec((1, LANES), lambda b, i: (0, 0)),
                  pl.BlockSpec((tb, tb), lambda b, i: (0, 0))],
        out_specs=(pl.BlockSpec((tb, LANES), lambda b, i: (b * nb + i, 0)),
                   pl.BlockSpec((tb, FOX_HEADS * LANES), lambda b, i: (b * nb + i, 0))),
        scratch_shapes=[pltpu.VMEM((1, LANES), F32)],
        compiler_params=_cparams(("arbitrary", "arbitrary")),
        name="forget_cumsum",
    )(small, bias_row, tri)


def _attn_body(mode, nslab, lam_init, qt_ref, kt_ref, q_ref, k_ref, vt_ref, *rest):
    if mode == "diff":
        lam_ref, g_ref, o_ref, m_sc, l_sc, acc_sc = rest
    elif mode == "dsa":
        mask_ref, o_ref, m_sc, l_sc, acc_sc = rest
    else:
        cq_ref, ck_ref, o_ref, m_sc, l_sc, acc_sc = rest
    p = pl.program_id(1)
    qi = qt_ref[p]
    ki = kt_ref[p]
    tq = q_ref.shape[0]
    tk = k_ref.shape[0]

    @pl.when(ki == 0)
    def _():
        m_sc[...] = jnp.full_like(m_sc, NEG)
        l_sc[...] = jnp.zeros_like(l_sc)
        acc_sc[...] = jnp.zeros_like(acc_sc)

    lane = lax.broadcasted_iota(jnp.int32, (1, LANES), 1)

    def tile_update(diag):
        if mode == "dsa":
            keep_all = mask_ref[...] != 0
        elif diag:
            krow = lax.broadcasted_iota(jnp.int32, (tk, tq), 0)
            qcol = lax.broadcasted_iota(jnp.int32, (tk, tq), 1)
            if mode == "diff":
                keep_all = (krow >> 6) <= (qcol >> 6)
            else:
                keep_all = krow <= qcol
        for sl in range(nslab):
            ls = slice(sl * LANES, (sl + 1) * LANES)
            q = q_ref[:, ls]
            k = k_ref[:, ls]
            vt = vt_ref[ls, :]
            for sub in range(2):
                idx = 2 * sl + sub
                half = (lane < HEAD_DIM) if sub == 0 else (lane >= HEAD_DIM)
                qm = jnp.where(half, q, jnp.zeros_like(q))
                s = lax.dot_general(k, qm, (((1,), (1,)), ((), ())), preferred_element_type=F32)
                if mode == "fox":
                    ck = ck_ref[:, idx * LANES:(idx + 1) * LANES]
                    s = s + cq_ref[0, idx:idx + 1, :] - jnp.concatenate([ck] * (tq // LANES), axis=1)
                if mode == "dsa" or diag:
                    s = jnp.where(keep_all, s, NEG)
                m_old = m_sc[idx:idx + 1, :]
                m_new = jnp.maximum(m_old, jnp.max(s, axis=0, keepdims=True))
                alpha = jnp.exp(m_old - m_new)
                pr = jnp.exp(s - m_new)
                l_sc[idx:idx + 1, :] = alpha * l_sc[idx:idx + 1, :] + jnp.sum(pr, axis=0, keepdims=True)
                acc_sc[idx] = alpha * acc_sc[idx] + jnp.dot(vt, pr.astype(BF16), preferred_element_type=F32)
                m_sc[idx:idx + 1, :] = m_new

    if mode == "dsa":
        tile_update(False)
    else:
        @pl.when(ki == qi)
        def _():
            tile_update(True)

        @pl.when(ki != qi)
        def _():
            tile_update(False)

    @pl.when(ki == qi)
    def _():
        if mode == "diff":
            lm = lam_ref[...]
            lam = (jnp.exp(jnp.sum(lm[0:1] * lm[1:2], keepdims=True))
                   - jnp.exp(jnp.sum(lm[2:3] * lm[3:4], keepdims=True)) + lam_init)
        feat = lax.broadcasted_iota(jnp.int32, (LANES, 1), 0)
        for sl in range(nslab):
            o0 = acc_sc[2 * sl] / l_sc[2 * sl:2 * sl + 1, :]
            o1 = acc_sc[2 * sl + 1] / l_sc[2 * sl + 1:2 * sl + 2, :]
            if mode == "diff":
                o = (o0 - lam * o1).T
                o = o * lax.rsqrt(jnp.mean(o * o, axis=-1, keepdims=True) + LN_EPS)
                o = (o * g_ref[...]) * (1.0 - lam_init)
            else:
                o = jnp.where(feat < HEAD_DIM, o0, o1).T
            o_ref[:, sl * LANES:(sl + 1) * LANES] = o.astype(o_ref.dtype)


def _attention(mode, qarr, qcol, karr, kcol, vt, vrow, B, S, extra, lam_init=0.0, tile=512):
    nslab = 4
    width = nslab * LANES
    nq = S // tile
    pairs = [(qi, ki) for qi in range(nq) for ki in range(qi + 1)]
    qt = jnp.asarray([pq for pq, _ in pairs], jnp.int32)
    kt = jnp.asarray([pk for _, pk in pairs], jnp.int32)
    T = B * S
    qmap = lambda b, p, qt, kt: (b * nq + qt[p], qcol // width)
    kmap = lambda b, p, qt, kt: (b * nq + kt[p], kcol // width)
    vmap = lambda b, p, qt, kt: (vrow // width, b * nq + kt[p])
    in_specs = [pl.BlockSpec((tile, width), qmap), pl.BlockSpec((tile, width), kmap),
                pl.BlockSpec((width, tile), vmap)]
    args = [qarr, karr, vt]
    if mode == "diff":
        lam4, g = extra
        in_specs += [pl.BlockSpec((4, HEAD_DIM), lambda b, p, qt, kt: (0, 0)),
                     pl.BlockSpec((1, LANES), lambda b, p, qt, kt: (0, 0))]
        args += [lam4, g.reshape(1, LANES)]
    elif mode == "dsa":
        (mask_t,) = extra
        in_specs += [pl.BlockSpec((tile, tile), lambda b, p, qt, kt: (b * nq + kt[p], qt[p]))]
        args += [mask_t]
    else:
        ct, crep = extra
        in_specs += [pl.BlockSpec((1, FOX_HEADS, tile), lambda b, p, qt, kt: (b, 0, qt[p])),
                     pl.BlockSpec((tile, FOX_HEADS * LANES), lambda b, p, qt, kt: (b * nq + kt[p], 0))]
        args += [ct, crep]
    return pl.pallas_call(
        functools.partial(_attn_body, mode, nslab, lam_init),
        out_shape=jax.ShapeDtypeStruct((T, width), BF16),
        grid_spec=pltpu.PrefetchScalarGridSpec(
            num_scalar_prefetch=2,
            grid=(B, len(pairs)),
            in_specs=in_specs,
            out_specs=pl.BlockSpec((tile, width), lambda b, p, qt, kt: (b * nq + qt[p], 0)),
            scratch_shapes=[pltpu.VMEM((2 * nslab, tile), F32), pltpu.VMEM((2 * nslab, tile), F32),
                            pltpu.VMEM((2 * nslab, LANES, tile), F32)]),
        compiler_params=_cparams(("arbitrary", "arbitrary")),
        name="attn_" + mode,
    )(qt, kt, *args)


def _select_body(topk, iq_ref, ik_ref, iw_ref, tri_ref, mask_ref, keys_ref):
    qi = pl.program_id(1)
    tq = iq_ref.shape[0]
    tk = tq
    nkb = qi + 1
    q0 = qi * tq
    lane = lax.broadcasted_iota(jnp.int32, (1, iq_ref.shape[1]), 1)
    iq = iq_ref[...]
    qms = [jnp.where((lane >> 6) == h, iq, jnp.zeros_like(iq)) for h in range(IDX_HEADS)]
    iw = iw_ref[...]
    row_chunk = (q0 + lax.broadcasted_iota(jnp.int32, (tq, tk), 0)) >> 6
    col_iota = lax.broadcasted_iota(jnp.int32, (tq, tk), 1)

    mask_ref[...] = jnp.zeros_like(mask_ref)

    def score_block(j, carry):
        k0 = pl.multiple_of(j * tk, tk)
        ik = ik_ref[pl.ds(k0, tk), :]
        score = jnp.zeros((tq, tk), F32)
        for h in range(IDX_HEADS):
            lg = lax.dot_general(qms[h], ik, (((1,), (1,)), ((), ())), preferred_element_type=F32)
            score = score + iw[:, SM_IW + h:SM_IW + h + 1] * jnp.maximum(lg, 0.0)
        score = jnp.where(score == 0.0, 0.0, score)
        bits = lax.bitcast_convert_type(score, jnp.int32)
        key = bits ^ ((bits >> 31) & jnp.int32(0x7FFFFFFF))
        vis = ((k0 + col_iota) >> 6) <= row_chunk
        keys_ref[:, pl.ds(k0, tk)] = jnp.where(vis, key, jnp.int32(INT_MIN))
        return carry

    lax.fori_loop(0, nkb, score_block, 0)

    def count_ge(cand):
        def body(j, acc):
            k0 = pl.multiple_of(j * tk, tk)
            blk = keys_ref[:, pl.ds(k0, tk)]
            ge = jnp.where(blk >= cand, 1, 0).astype(jnp.int32)
            for i in range(tk // LANES):
                acc = acc + ge[:, i * LANES:(i + 1) * LANES]
            return acc
        acc = lax.fori_loop(0, nkb, body, jnp.zeros((tq, LANES), jnp.int32))
        return jnp.sum(acc, axis=1, keepdims=True)

    def bit_step(i, lo):
        cand = lo + (jnp.int32(1) << (31 - i))
        return jnp.where(count_ge(cand) >= topk, cand, lo)

    thr = lax.fori_loop(0, 32, bit_step, jnp.full((tq, 1), INT_MIN, jnp.int32))
    thr = jnp.maximum(thr, jnp.int32(INT_MIN + 1))
    need = (topk - count_ge(thr + 1)).astype(F32)
    tri = tri_ref[...]

    def select_block(j, carry):
        k0 = pl.multiple_of(j * tk, tk)
        key = keys_ref[:, pl.ds(k0, tk)]
        eq = key == thr
        incl = jnp.dot(jnp.where(eq, 1.0, 0.0).astype(BF16), tri, preferred_element_type=F32)
        sel = (key > thr) | (eq & ((carry + incl) <= need))
        mask_ref[:, pl.ds(k0, tk)] = jnp.where(sel, 1, 0).astype(jnp.int8)
        return carry + incl[:, tk - 1:tk]

    lax.fori_loop(0, nkb, select_block, jnp.zeros((tq, 1), F32))


def _dsa_select(qk, small, B, S, topk, tq=256):
    T = B * S
    nq = S // tq
    tri = jnp.triu(jnp.ones((tq, tq), F32)).astype(BF16)
    return pl.pallas_call(
        functools.partial(_select_body, topk),
        out_shape=jax.ShapeDtypeStruct((T, S), jnp.int8),
        grid=(B, nq),
        in_specs=[pl.BlockSpec((tq, 256), lambda b, i: (b * nq + i, QK_IQ // 256)),
                  pl.BlockSpec((S, 256), lambda b, i: (b, QK_IK // 256)),
                  pl.BlockSpec((tq, LANES), lambda b, i: (b * nq + i, 0)),
                  pl.BlockSpec((tq, tq), lambda b, i: (0, 0))],
        out_specs=pl.BlockSpec((tq, S), lambda b, i: (b * nq + i, 0)),
        scratch_shapes=[pltpu.VMEM((tq, S), jnp.int32)],
        compiler_params=_cparams(("arbitrary", "arbitrary")),
        name="dsa_select",
    )(qk, qk, small, tri)


def _merge_body(alpha, od_ref, os_ref, of_ref, gate_ref, h_ref, wbr_ref, wo_ref, g_ref, b_ref, wr_ref,
                h32_ref, h16_ref, aff_ref):
    D = h_ref.shape[1]
    y = None
    for i, o_ref in enumerate((od_ref, os_ref, of_ref)):
        br = jnp.dot(o_ref[...], wbr_ref[i], preferred_element_type=F32)
        t = gate_ref[:, i * D:(i + 1) * D].astype(F32) * br
        y = t if y is None else y + t
    m = jnp.dot(y.astype(BF16), wo_ref[...], preferred_element_type=F32)
    hn = _ln_rows(alpha * h_ref[...] + m, g_ref[...], b_ref[...])
    h32_ref[...] = hn
    h16 = hn.astype(BF16)
    h16_ref[...] = h16
    logits = lax.dot_general(wr_ref[...], h16, (((1,), (1,)), ((), ())), preferred_element_type=F32)
    aff_ref[...] = 1.0 / (1.0 + jnp.exp(-logits))


def _merge(alpha, od, os_, of, gates, h32, wbr, wo, g, b, wr_t, tm=512):
    T, D = h32.shape
    bw = od.shape[1]
    row = lambda i: (i, 0)
    fixed2 = lambda i: (0, 0)
    return pl.pallas_call(
        functools.partial(_merge_body, alpha),
        out_shape=(jax.ShapeDtypeStruct((T, D), F32), jax.ShapeDtypeStruct((T, D), BF16),
                   jax.ShapeDtypeStruct((N_EXPERTS, T), F32)),
        grid=(T // tm,),
        in_specs=[pl.BlockSpec((tm, bw), row), pl.BlockSpec((tm, bw), row), pl.BlockSpec((tm, bw), row),
                  pl.BlockSpec((tm, N_BRANCH * D), row), pl.BlockSpec((tm, D), row),
                  pl.BlockSpec((N_BRANCH, bw, D), lambda i: (0, 0, 0)), pl.BlockSpec((D, D), fixed2),
                  pl.BlockSpec((1, D), fixed2), pl.BlockSpec((1, D), fixed2),
                  pl.BlockSpec((N_EXPERTS, D), fixed2)],
        out_specs=(pl.BlockSpec((tm, D), row), pl.BlockSpec((tm, D), row),
                   pl.BlockSpec((N_EXPERTS, tm), lambda i: (0, i))),
        compiler_params=_cparams(("arbitrary",)),
        name="merge",
    )(od, os_, of, gates, h32, wbr, wo, g.reshape(1, D), b.reshape(1, D), wr_t)


def _route_body(aff_ref, bias_ref, e_ref, w_ref):
    aff = aff_ref[...]
    sel = aff + bias_ref[...]
    sub = lax.broadcasted_iota(jnp.int32, (EXPERTS_PER_GROUP, aff.shape[1]), 0)
    big = jnp.int32(EXPERTS_PER_GROUP)
    best = None
    for gidx in range(N_GROUPS):
        rs = slice(gidx * EXPERTS_PER_GROUP, (gidx + 1) * EXPERTS_PER_GROUP)
        s8 = sel[rs]
        a8 = aff[rs]
        m1 = jnp.max(s8, axis=0, keepdims=True)
        i1 = jnp.min(jnp.where(s8 == m1, sub, big), axis=0, keepdims=True)
        rest = jnp.where(sub == i1, -jnp.inf, s8)
        m2 = jnp.max(rest, axis=0, keepdims=True)
        i2 = jnp.min(jnp.where(rest == m2, sub, big), axis=0, keepdims=True)
        a1 = jnp.sum(jnp.where(sub == i1, a8, 0.0), axis=0, keepdims=True)
        a2 = jnp.sum(jnp.where(sub == i2, a8, 0.0), axis=0, keepdims=True)
        score = m1 + m2
        cand = (score, i1 + gidx * EXPERTS_PER_GROUP, i2 + gidx * EXPERTS_PER_GROUP, a1, a2)
        if best is None:
            best = cand
        else:
            take = score > best[0]
            best = tuple(jnp.where(take, c, o) for c, o in zip(cand, best))
    _, e1, e2, a1, a2 = best
    tot = a1 + a2
    e_ref[0:1, :] = e1
    e_ref[1:2, :] = e2
    w_ref[0:1, :] = a1 / tot
    w_ref[1:2, :] = a2 / tot


def _route(aff_t, b_router, tn=2048):
    E, T = aff_t.shape
    tn = min(tn, T)
    return pl.pallas_call(
        _route_body,
        out_shape=(jax.ShapeDtypeStruct((TOP_K, T), jnp.int32), jax.ShapeDtypeStruct((TOP_K, T), F32)),
        grid=(T // tn,),
        in_specs=[pl.BlockSpec((E, tn), lambda i: (0, i)), pl.BlockSpec((E, 1), lambda i: (0, 0))],
        out_specs=(pl.BlockSpec((TOP_K, tn), lambda i: (0, i)), pl.BlockSpec((TOP_K, tn), lambda i: (0, i))),
        compiler_params=_cparams(("arbitrary",)),
        name="route",
    )(aff_t, b_router.reshape(E, 1))


def _expert_body(be_ref, bc_ref, tok_ref, slot_ref, x_hbm, wg_ref, wu_ref, wd_ref, y_hbm,
                 xbuf, ybuf, gsem, ssem):
    blk = pl.program_id(0)
    cnt = bc_ref[blk]

    @pl.when(blk == 0)
    def _():
        xbuf[...] = jnp.zeros_like(xbuf)

    def gather_copy(r):
        return pltpu.make_async_copy(x_hbm.at[pl.ds(tok_ref[0, 0, r], 1), :], xbuf.at[pl.ds(r, 1), :], gsem)

    def scatter_copy(r):
        return pltpu.make_async_copy(ybuf.at[pl.ds(r, 1), :], y_hbm.at[pl.ds(slot_ref[0, 0, r], 1), :], ssem)

    @pl.when(cnt > 0)
    def _():
        def g_start(r, c):
            gather_copy(r).start()
            return c

        def g_wait(r, c):
            gather_copy(r).wait()
            return c

        lax.fori_loop(0, cnt, g_start, 0)
        lax.fori_loop(0, cnt, g_wait, 0)
        x = xbuf[...].astype(BF16)
        gate = jnp.dot(x, wg_ref[0], preferred_element_type=F32)
        up = jnp.dot(x, wu_ref[0], preferred_element_type=F32)
        hmid = (gate * (1.0 / (1.0 + jnp.exp(-gate)))) * up
        ybuf[...] = jnp.dot(hmid.astype(BF16), wd_ref[0], preferred_element_type=F32)

        def s_start(r, c):
            scatter_copy(r).start()
            return c

        def s_wait(r, c):
            scatter_copy(r).wait()
            return c

        lax.fori_loop(0, cnt, s_start, 0)
        lax.fori_loop(0, cnt, s_wait, 0)


def _experts(h32, blk_e, blk_cnt, tok_pad, slot_pad, wg, wu, wd, bm):
    T, D = h32.shape
    nb = blk_e.shape[0]
    de = wg.shape[2]
    return pl.pallas_call(
        _expert_body,
        out_shape=jax.ShapeDtypeStruct((TOP_K * T, D), F32),
        grid_spec=pltpu.PrefetchScalarGridSpec(
            num_scalar_prefetch=2,
            grid=(nb,),
            in_specs=[pl.BlockSpec((1, 1, bm), lambda i, be, bc: (i, 0, 0), memory_space=pltpu.SMEM),
                      pl.BlockSpec((1, 1, bm), lambda i, be, bc: (i, 0, 0), memory_space=pltpu.SMEM),
                      pl.BlockSpec(memory_space=pl.ANY),
                      pl.BlockSpec((1, D, de), lambda i, be, bc: (be[i], 0, 0)),
                      pl.BlockSpec((1, D, de), lambda i, be, bc: (be[i], 0, 0)),
                      pl.BlockSpec((1, de, D), lambda i, be, bc: (be[i], 0, 0))],
            out_specs=pl.BlockSpec(memory_space=pl.ANY),
            scratch_shapes=[pltpu.VMEM((bm, D), F32), pltpu.VMEM((bm, D), F32),
                            pltpu.SemaphoreType.DMA, pltpu.SemaphoreType.DMA]),
        compiler_params=_cparams(("arbitrary",)),
        name="experts",
    )(blk_e, blk_cnt, tok_pad, slot_pad, h32, wg, wu, wd)


def _dispatch_tables(e_t, T, bm):
    A = TOP_K * T
    flat_e = e_t.T.reshape(A)
    order = jnp.argsort(flat_e, stable=True).astype(jnp.int32)
    counts = jnp.bincount(flat_e, length=N_EXPERTS).astype(jnp.int32)
    start = jnp.cumsum(counts) - counts
    nblk_e = (counts + bm - 1) // bm
    blk_end = jnp.cumsum(nblk_e)
    nb = A // bm + N_EXPERTS
    bidx = jnp.arange(nb, dtype=jnp.int32)
    blk_e = jnp.minimum(jnp.searchsorted(blk_end, bidx, side="right"), N_EXPERTS - 1).astype(jnp.int32)
    off = (bidx - (blk_end - nblk_e)[blk_e]) * bm
    blk_cnt = jnp.where(bidx < blk_end[-1], jnp.clip(counts[blk_e] - off, 0, bm), 0).astype(jnp.int32)
    pos = jnp.clip((start[blk_e] + off)[:, None] + jnp.arange(bm, dtype=jnp.int32)[None, :], 0, A - 1)
    slot_pad = order[pos]
    tok_pad = slot_pad // TOP_K
    return blk_e, blk_cnt, tok_pad.reshape(nb, 1, bm), slot_pad.reshape(nb, 1, bm)


def _combine_body(alpha, h_ref, y_ref, w_ref, g_ref, b_ref, o32_ref, o16_ref):
    D = h_ref.shape[1]
    w = w_ref[...]
    f = w[:, 0:1] * y_ref[:, 0:D] + w[:, 1:2] * y_ref[:, D:2 * D]
    hn = _ln_rows(alpha * h_ref[...] + f, g_ref[...], b_ref[...])
    o32_ref[...] = hn
    o16_ref[...] = hn.astype(BF16)


def _combine(alpha, h32, y2, w_rows, g, b, tm=512):
    T, D = h32.shape
    row = lambda i: (i, 0)
    fixed = lambda i: (0, 0)
    return pl.pallas_call(
        functools.partial(_combine_body, alpha),
        out_shape=(jax.ShapeDtypeStruct((T, D), F32), jax.ShapeDtypeStruct((T, D), BF16)),
        grid=(T // tm,),
        in_specs=[pl.BlockSpec((tm, D), row), pl.BlockSpec((tm, TOP_K * D), row),
                  pl.BlockSpec((tm, TOP_K), row), pl.BlockSpec((1, D), fixed), pl.BlockSpec((1, D), fixed)],
        out_specs=(pl.BlockSpec((tm, D), row), pl.BlockSpec((tm, D), row)),
        compiler_params=_cparams(("arbitrary",)),
        name="combine",
    )(h32, y2.reshape(T, TOP_K * D), w_rows, g.reshape(1, D), b.reshape(1, D))


def _rotate_half_columns(w):
    d = w.shape[0]
    w4 = w.reshape(d, -1, 2, HEAD_DIM // 2)
    return jnp.concatenate([-w4[:, :, 1:2], w4[:, :, 0:1]], axis=2).reshape(w.shape)


def _prep_in_weights(w_in_l):
    qk_w = DIFF_HEADS * 2 * HEAD_DIM
    sizes = (qk_w, qk_w, DIFF_HEADS * 2 * HEAD_DIM, 512, 512, 512, IDX_HEADS * HEAD_DIM, HEAD_DIM, IDX_HEADS,
             512, 512, 512, FOX_HEADS, N_BRANCH * w_in_l.shape[0])
    offs = [0]
    for s in sizes:
        offs.append(offs[-1] + s)
    dq, dk, dv, sq, sk, sv, iq, ik, iw, fq, fk, fv, ff, g = (w_in_l[:, offs[i]:offs[i + 1]] for i in range(len(sizes)))
    scale = HEAD_DIM ** -0.5
    idx_scale = (IDX_HEADS * HEAD_DIM) ** -0.5
    w_rope = jnp.concatenate([dq * scale, dk, sq * scale, sk, iq, ik, ik, ik, ik], axis=1)
    w_rot = _rotate_half_columns(w_rope)
    w_plain = jnp.concatenate([fq * scale, fk], axis=1)
    w_vt = jnp.concatenate([dv, sv, fv], axis=1).T
    pad = jnp.zeros((w_in_l.shape[0], LANES - IDX_HEADS - FOX_HEADS), w_in_l.dtype)
    w_small = jnp.concatenate([iw * idx_scale, ff, pad], axis=1)
    return (w_rope.astype(BF16), w_rot.astype(BF16), w_plain.astype(BF16), w_vt.astype(BF16), g.astype(BF16),
            w_small.astype(BF16))


def _rope_tables(positions):
    inv = ROPE_THETA ** (-jnp.arange(0, HEAD_DIM, 2, dtype=F32) / HEAD_DIM)
    ang = positions.astype(F32).reshape(-1, 1) * inv
    c, s = jnp.cos(ang), jnp.sin(ang)
    return jnp.tile(c, (1, LANES // (HEAD_DIM // 2))), jnp.tile(s, (1, LANES // (HEAD_DIM // 2)))


def kernel(x, positions, ln_in_g, ln_in_b, w_in, b_forget, diff_lam, diff_norm_g, w_br, w_o, ln1_g, ln1_b,
           w_router, b_router, w_gate, w_up, w_down, ln2_g, ln2_b):
    B, S, D = x.shape
    T = B * S
    depth = w_in.shape[0]
    alpha = (2 * depth) ** 0.25
    topk = min(DSA_TOPK_MAX, S // 4)
    bm = 256
    cc, ss = _rope_tables(positions)
    wr_t = w_router.T.astype(BF16)
    h32, h16 = _layer_norm(x.reshape(T, D), ln_in_g, ln_in_b)
    for l in range(depth):
        lam_init = 0.8 - 0.6 * math.exp(-0.3 * l)
        w_rope, w_rot, w_plain, w_vt, w_g, w_small = _prep_in_weights(w_in[l])
        qk = _project(h16, w_rope, "rope", BF16, tables=(cc, ss), wrot=w_rot)
        fqk = _project(h16, w_plain, "plain", BF16)
        vt = _project(h16, w_vt, "transposed", BF16)
        gates = _project(h16, w_g, "sigmoid", BF16)
        small = _project(h16, w_small, "f32", F32)

        o_diff = _attention("diff", qk, QK_DQ, qk, QK_DK, vt, VT_DV, B, S,
                            (diff_lam[l], diff_norm_g[l]), lam_init=lam_init)
        mask = _dsa_select(qk, small, B, S, topk)
        mask_t = jnp.transpose(mask.reshape(B, S, S), (0, 2, 1)).reshape(T, S)
        o_dsa = _attention("dsa", qk, QK_SQ, qk, QK_SK, vt, VT_SV, B, S, (mask_t,))
        bias_row = jnp.zeros((1, LANES), F32).at[0, SM_FF:SM_FF + FOX_HEADS].set(b_forget[l])
        c, crep = _forget_cumsum(small, bias_row, B, S)
        ct = jnp.transpose(c[:, SM_FF:SM_FF + FOX_HEADS].reshape(B, S, FOX_HEADS), (0, 2, 1))
        o_fox = _attention("fox", fqk, FQK_FQ, fqk, FQK_FK, vt, VT_FV, B, S, (ct, crep))

        h32, h16, aff_t = _merge(alpha, o_diff, o_dsa, o_fox, gates, h32, w_br[l].astype(BF16),
                                 w_o[l].astype(BF16), ln1_g[l], ln1_b[l], wr_t)
        e_t, w_t = _route(aff_t, b_router)
        blk_e, blk_cnt, tok_pad, slot_pad = _dispatch_tables(e_t, T, bm)
        y2 = _experts(h32, blk_e, blk_cnt, tok_pad, slot_pad, w_gate[l].astype(BF16), w_up[l].astype(BF16),
                      w_down[l].astype(BF16), bm)
        h32, h16 = _combine(alpha, h32, y2, w_t.T, ln2_g[l], ln2_b[l])
    return h32.reshape(B, S, D)
```

```python
import functools
import math

import jax
import jax.numpy as jnp
from jax import lax
from jax.experimental import pallas as pl
from jax.experimental.pallas import tpu as pltpu

F32 = jnp.float32
BF16 = jnp.bfloat16

CHUNK = 64
HEAD_DIM = 64
ROPE_THETA = 10000.0
LN_EPS = 1e-5
DIFF_HEADS = 4
DSA_HEADS = 8
IDX_HEADS = 4
DSA_TOPK_MAX = 256
FOX_HEADS = 8
N_BRANCH = 3
BRANCH_WIDTH = 512
N_EXPERTS = 32
N_GROUPS = 4
EXPERTS_PER_GROUP = N_EXPERTS // N_GROUPS
TOP_K = 2

LANES = 128
NEG = -1e30
INT_MIN = -2 ** 31

QK_DQ, QK_DK, QK_SQ, QK_SK, QK_IQ, QK_IK = 0, 512, 1024, 1536, 2048, 2304
QK_WIDTH = 2560
FQK_FQ, FQK_FK = 0, 512
VT_DV, VT_SV, VT_FV = 0, 512, 1024
SM_IW, SM_FF = 0, 4
VMEM_LIMIT = 56 * 1024 * 1024


def _cparams(sem):
    return pltpu.CompilerParams(dimension_semantics=sem, vmem_limit_bytes=VMEM_LIMIT)


def _ln_rows(x, g, b):
    mu = jnp.mean(x, axis=-1, keepdims=True)
    xc = x - mu
    var = jnp.mean(xc * xc, axis=-1, keepdims=True)
    return xc * lax.rsqrt(var + LN_EPS) * g + b


def _ln_body(x_ref, g_ref, b_ref, o32_ref, o16_ref):
    y = _ln_rows(x_ref[...], g_ref[...], b_ref[...])
    o32_ref[...] = y
    o16_ref[...] = y.astype(BF16)


def _layer_norm(x, g, b, tm=512):
    T, D = x.shape
    return pl.pallas_call(
        _ln_body,
        out_shape=(jax.ShapeDtypeStruct((T, D), F32), jax.ShapeDtypeStruct((T, D), BF16)),
        grid=(T // tm,),
        in_specs=[pl.BlockSpec((tm, D), lambda i: (i, 0)),
                  pl.BlockSpec((1, D), lambda i: (0, 0)),
                  pl.BlockSpec((1, D), lambda i: (0, 0))],
        out_specs=(pl.BlockSpec((tm, D), lambda i: (i, 0)), pl.BlockSpec((tm, D), lambda i: (i, 0))),
        compiler_params=_cparams(("arbitrary",)),
        name="layer_norm",
    )(x, g.reshape(1, D), b.reshape(1, D))


def _proj_body(mode, width, h_ref, w_ref, *rest):
    if mode == "rope":
        wrot_ref, cc_ref, ss_ref, o_ref = rest
    else:
        (o_ref,) = rest
    h = h_ref[...]
    tn = 512 if width % 512 == 0 else width
    for j in range(width // tn):
        cs = slice(j * tn, (j + 1) * tn)
        if mode == "transposed":
            o_ref[cs, :] = lax.dot_general(w_ref[cs, :], h, (((1,), (1,)), ((), ())),
                                           preferred_element_type=F32).astype(o_ref.dtype)
            continue
        z = jnp.dot(h, w_ref[:, cs], preferred_element_type=F32)
        if mode == "rope":
            zr = jnp.dot(h, wrot_ref[:, cs], preferred_element_type=F32)
            cc = cc_ref[...]
            ss = ss_ref[...]
            for i in range(tn // LANES):
                ls = slice(i * LANES, (i + 1) * LANES)
                o_ref[:, j * tn + i * LANES:j * tn + (i + 1) * LANES] = (
                    z[:, ls] * cc + zr[:, ls] * ss).astype(o_ref.dtype)
        elif mode == "sigmoid":
            o_ref[:, cs] = (1.0 / (1.0 + jnp.exp(-z))).astype(o_ref.dtype)
        else:
            o_ref[:, cs] = z.astype(o_ref.dtype)


def _project(h16, w, mode, out_dtype, tables=None, wrot=None, tm=512):
    T, D = h16.shape
    transposed = mode == "transposed"
    width = w.shape[0] if transposed else w.shape[1]
    row = lambda i: (i, 0)
    fixed = lambda i: (0, 0)
    in_specs = [pl.BlockSpec((tm, D), row), pl.BlockSpec(w.shape, fixed)]
    args = [h16, w]
    if mode == "rope":
        cc, ss = tables
        in_specs += [pl.BlockSpec((D, width), fixed), pl.BlockSpec((tm, LANES), row),
                     pl.BlockSpec((tm, LANES), row)]
        args += [wrot, cc, ss]
    return pl.pallas_call(
        functools.partial(_proj_body, mode, width),
        out_shape=jax.ShapeDtypeStruct((width, T) if transposed else (T, width), out_dtype),
        grid=(T // tm,),
        in_specs=in_specs,
        out_specs=pl.BlockSpec((width, tm), lambda i: (0, i)) if transposed else pl.BlockSpec((tm, width), row),
        compiler_params=_cparams(("arbitrary",)),
        name="proj_" + mode,
    )(*args)


def _cumsum_body(x_ref, bias_ref, tri_ref, c_ref, crep_ref, carry_ref):
    @pl.when(pl.program_id(1) == 0)
    def _():
        carry_ref[...] = jnp.zeros_like(carry_ref)

    x = x_ref[...] + bias_ref[...]
    lf = jnp.minimum(x, 0.0) - jnp.log1p(jnp.exp(-jnp.abs(x)))
    hi = lf.astype(BF16)
    r1 = lf - hi.astype(F32)
    mid = r1.astype(BF16)
    lo = (r1 - mid.astype(F32)).astype(BF16)
    tri = tri_ref[...]
    c = (jnp.dot(tri, hi, preferred_element_type=F32) + jnp.dot(tri, mid, preferred_element_type=F32)
         + jnp.dot(tri, lo, preferred_element_type=F32)) + carry_ref[...]
    c_ref[...] = c
    carry_ref[...] = c[-1:, :]
    for h in range(FOX_HEADS):
        crep_ref[:, h * LANES:(h + 1) * LANES] = jnp.broadcast_to(c[:, SM_FF + h:SM_FF + h + 1], (c.shape[0], LANES))


def _forget_cumsum(small, bias_row, B, S, tb=512):
    T = small.shape[0]
    nb = S // tb
    tri = jnp.tril(jnp.ones((tb, tb), F32)).astype(BF16)
    return pl.pallas_call(
        _cumsum_body,
        out_shape=(jax.ShapeDtypeStruct((T, LANES), F32), jax.ShapeDtypeStruct((T, FOX_HEADS * LANES), F32)),
        grid=(B, nb),
        in_specs=[pl.BlockSpec((tb, LANES), lambda b, i: (b * nb + i, 0)),
                  pl.BlockSpec((1, LANES), lambda b, i: (0, 0)),
                  pl.BlockSpec((tb, tb), lambda b, i: (0, 0))],
        out_specs=(pl.BlockSpec((tb, LANES), lambda b, i: (b * nb + i, 0)),
                   pl.BlockSpec((tb, FOX_HEADS * LANES), lambda b, i: (b * nb + i, 0))),
        scratch_shapes=[pltpu.VMEM((1, LANES), F32)],
        compiler_params=_cparams(("arbitrary", "arbitrary")),
        name="forget_cumsum",
    )(small, bias_row, tri)


def _attn_body(mode, nslab, lam_init, qt_ref, kt_ref, q_ref, k_ref, vt_ref, *rest):
    if mode == "diff":
        lam_ref, g_ref, o_ref, m_sc, l_sc, acc_sc = rest
    elif mode == "dsa":
        mask_ref, o_ref, m_sc, l_sc, acc_sc = rest
    else:
        cq_ref, ck_ref, o_ref, m_sc, l_sc, acc_sc = rest
    p = pl.program_id(1)
    qi = qt_ref[p]
    ki = kt_ref[p]
    tq = q_ref.shape[0]
    tk = k_ref.shape[0]

    @pl.when(ki == 0)
    def _():
        m_sc[...] = jnp.full_like(m_sc, NEG)
        l_sc[...] = jnp.zeros_like(l_sc)
        acc_sc[...] = jnp.zeros_like(acc_sc)

    lane = lax.broadcasted_iota(jnp.int32, (1, LANES), 1)

    def tile_update(diag):
        if mode == "dsa":
            keep_all = mask_ref[...] != 0
        elif diag:
            krow = lax.broadcasted_iota(jnp.int32, (tk, tq), 0)
            qcol = lax.broadcasted_iota(jnp.int32, (tk, tq), 1)
            if mode == "diff":
                keep_all = (krow >> 6) <= (qcol >> 6)
            else:
                keep_all = krow <= qcol
        for sl in range(nslab):
            ls = slice(sl * LANES, (sl + 1) * LANES)
            q = q_ref[:, ls]
            k = k_ref[:, ls]
            vt = vt_ref[ls, :]
            for sub in range(2):
                idx = 2 * sl + sub
                half = (lane < HEAD_DIM) if sub == 0 else (lane >= HEAD_DIM)
                qm = jnp.where(half, q, jnp.zeros_like(q))
                s = lax.dot_general(k, qm, (((1,), (1,)), ((), ())), preferred_element_type=F32)
                if mode == "fox":
                    ck = ck_ref[:, idx * LANES:(idx + 1) * LANES]
                    s = s + cq_ref[0, idx:idx + 1, :] - jnp.concatenate([ck] * (tq // LANES), axis=1)
                if mode == "dsa" or diag:
                    s = jnp.where(keep_all, s, NEG)
                m_old = m_sc[idx:idx + 1, :]
                m_new = jnp.maximum(m_old, jnp.max(s, axis=0, keepdims=True))
                alpha = jnp.exp(m_old - m_new)
                pr = jnp.exp(s - m_new)
                l_sc[idx:idx + 1, :] = alpha * l_sc[idx:idx + 1, :] + jnp.sum(pr, axis=0, keepdims=True)
                acc_sc[idx] = alpha * acc_sc[idx] + jnp.dot(vt, pr.astype(BF16), preferred_element_type=F32)
                m_sc[idx:idx + 1, :] = m_new

    if mode == "dsa":
        tile_update(False)
    else:
        @pl.when(ki == qi)
        def _():
            tile_update(True)

        @pl.when(ki != qi)
        def _():
            tile_update(False)

    @pl.when(ki == qi)
    def _():
        if mode == "diff":
            lm = lam_ref[...]
            lam = (jnp.exp(jnp.sum(lm[0:1] * lm[1:2], keepdims=True))
                   - jnp.exp(jnp.sum(lm[2:3] * lm[3:4], keepdims=True)) + lam_init)
        feat = lax.broadcasted_iota(jnp.int32, (LANES, 1), 0)
        for sl in range(nslab):
            o0 = acc_sc[2 * sl] / l_sc[2 * sl:2 * sl + 1, :]
            o1 = acc_sc[2 * sl + 1] / l_sc[2 * sl + 1:2 * sl + 2, :]
            if mode == "diff":
                o = (o0 - lam * o1).T
                o = o * lax.rsqrt(jnp.mean(o * o, axis=-1, keepdims=True) + LN_EPS)
                o = (o * g_ref[...]) * (1.0 - lam_init)
            else:
                o = jnp.where(feat < HEAD_DIM, o0, o1).T
            o_ref[:, sl * LANES:(sl + 1) * LANES] = o.astype(o_ref.dtype)


def _attention(mode, qarr, qcol, karr, kcol, vt, vrow, B, S, extra, lam_init=0.0, tile=512):
    nslab = 4
    width = nslab * LANES
    nq = S // tile
    pairs = [(qi, ki) for qi in range(nq) for ki in range(qi + 1)]
    qt = jnp.asarray([pq for pq, _ in pairs], jnp.int32)
    kt = jnp.asarray([pk for _, pk in pairs], jnp.int32)
    T = B * S
    qmap = lambda b, p, qt, kt: (b * nq + qt[p], qcol // width)
    kmap = lambda b, p, qt, kt: (b * nq + kt[p], kcol // width)
    vmap = lambda b, p, qt, kt: (vrow // width, b * nq + kt[p])
    in_specs = [pl.BlockSpec((tile, width), qmap), pl.BlockSpec((tile, width), kmap),
                pl.BlockSpec((width, tile), vmap)]
    args = [qarr, karr, vt]
    if mode == "diff":
        lam4, g = extra
        in_specs += [pl.BlockSpec((4, HEAD_DIM), lambda b, p, qt, kt: (0, 0)),
                     pl.BlockSpec((1, LANES), lambda b, p, qt, kt: (0, 0))]
        args += [lam4, g.reshape(1, LANES)]
    elif mode == "dsa":
        (mask_t,) = extra
        in_specs += [pl.BlockSpec((tile, tile), lambda b, p, qt, kt: (b * nq + kt[p], qt[p]))]
        args += [mask_t]
    else:
        ct, crep = extra
        in_specs += [pl.BlockSpec((1, FOX_HEADS, tile), lambda b, p, qt, kt: (b, 0, qt[p])),
                     pl.BlockSpec((tile, FOX_HEADS * LANES), lambda b, p, qt, kt: (b * nq + kt[p], 0))]
        args += [ct, crep]
    return pl.pallas_call(
        functools.partial(_attn_body, mode, nslab, lam_init),
        out_shape=jax.ShapeDtypeStruct((T, width), BF16),
        grid_spec=pltpu.PrefetchScalarGridSpec(
            num_scalar_prefetch=2,
            grid=(B, len(pairs)),
            in_specs=in_specs,
            out_specs=pl.BlockSpec((tile, width), lambda b, p, qt, kt: (b * nq + qt[p], 0)),
            scratch_shapes=[pltpu.VMEM((2 * nslab, tile), F32), pltpu.VMEM((2 * nslab, tile), F32),
                            pltpu.VMEM((2 * nslab, LANES, tile), F32)]),
        compiler_params=_cparams(("arbitrary", "arbitrary")),
        name="attn_" + mode,
    )(qt, kt, *args)


def _select_body(topk, iq_ref, ik_ref, iwt_ref, tri_ref, mask_ref, keys_ref):
    qi = pl.program_id(1)
    tq = iq_ref.shape[0]
    tk = tq
    nkb = qi + 1
    q0 = qi * tq
    lane = lax.broadcasted_iota(jnp.int32, (1, iq_ref.shape[1]), 1)
    iq = iq_ref[...]
    qms = [jnp.where((lane >> 6) == h, iq, jnp.zeros_like(iq)) for h in range(IDX_HEADS)]
    iwt = iwt_ref[...]
    q_chunk = (q0 + lax.broadcasted_iota(jnp.int32, (tk, tq), 1)) >> 6
    k_iota = lax.broadcasted_iota(jnp.int32, (tk, tq), 0)

    mask_ref[...] = jnp.zeros_like(mask_ref)

    def score_block(j, carry):
        k0 = pl.multiple_of(j * tk, tk)
        ik = ik_ref[pl.ds(k0, tk), :]
        score = jnp.zeros((tk, tq), F32)
        for h in range(IDX_HEADS):
            lg = lax.dot_general(ik, qms[h], (((1,), (1,)), ((), ())), preferred_element_type=F32)
            score = score + iwt[SM_IW + h:SM_IW + h + 1, :] * jnp.maximum(lg, 0.0)
        score = jnp.where(score == 0.0, 0.0, score)
        bits = lax.bitcast_convert_type(score, jnp.int32)
        key = bits ^ ((bits >> 31) & jnp.int32(0x7FFFFFFF))
        vis = ((k0 + k_iota) >> 6) <= q_chunk
        keys_ref[pl.ds(k0, tk), :] = jnp.where(vis, key, jnp.int32(INT_MIN))
        return carry

    lax.fori_loop(0, nkb, score_block, 0)

    def count_ge(cand):
        def body(j, acc):
            k0 = pl.multiple_of(j * tk, tk)
            ge = jnp.where(keys_ref[pl.ds(k0, tk), :] >= cand, 1, 0).astype(jnp.int32)
            for r in range(tk // 8):
                acc = acc + ge[r * 8:(r + 1) * 8, :]
            return acc
        acc = lax.fori_loop(0, nkb, body, jnp.zeros((8, tq), jnp.int32))
        return jnp.sum(acc, axis=0, keepdims=True)

    def bit_step(i, lo):
        cand = lo + (jnp.int32(1) << (31 - i))
        return jnp.where(count_ge(cand) >= topk, cand, lo)

    thr = lax.fori_loop(0, 32, bit_step, jnp.full((1, tq), INT_MIN, jnp.int32))
    thr = jnp.maximum(thr, jnp.int32(INT_MIN + 1))
    need = (topk - count_ge(thr + 1)).astype(F32)
    tri = tri_ref[...]

    def select_block(j, carry):
        k0 = pl.multiple_of(j * tk, tk)
        key = keys_ref[pl.ds(k0, tk), :]
        eq = key == thr
        incl = jnp.dot(tri, jnp.where(eq, 1.0, 0.0).astype(BF16), preferred_element_type=F32)
        sel = (key > thr) | (eq & ((carry + incl) <= need))
        mask_ref[pl.ds(k0, tk), :] = jnp.where(sel, 1, 0).astype(jnp.int8)
        return carry + incl[tk - 1:tk, :]

    lax.fori_loop(0, nkb, select_block, jnp.zeros((1, tq), F32))


def _dsa_select(qk, small_t, B, S, topk, tq=256):
    T = B * S
    nq = S // tq
    tri = jnp.tril(jnp.ones((tq, tq), F32)).astype(BF16)
    return pl.pallas_call(
        functools.partial(_select_body, topk),
        out_shape=jax.ShapeDtypeStruct((T, S), jnp.int8),
        grid=(B, nq),
        in_specs=[pl.BlockSpec((tq, 256), lambda b, i: (b * nq + i, QK_IQ // 256)),
                  pl.BlockSpec((S, 256), lambda b, i: (b, QK_IK // 256)),
                  pl.BlockSpec((8, tq), lambda b, i: (0, b * nq + i)),
                  pl.BlockSpec((tq, tq), lambda b, i: (0, 0))],
        out_specs=pl.BlockSpec((S, tq), lambda b, i: (b, i)),
        scratch_shapes=[pltpu.VMEM((S, tq), jnp.int32)],
        compiler_params=_cparams(("arbitrary", "arbitrary")),
        name="dsa_select",
    )(qk, qk, small_t, tri)


def _merge_body(alpha, od_ref, os_ref, of_ref, gate_ref, h_ref, wbr_ref, wo_ref, g_ref, b_ref, wr_ref,
                h32_ref, h16_ref, aff_ref):
    D = h_ref.shape[1]
    y = None
    for i, o_ref in enumerate((od_ref, os_ref, of_ref)):
        br = jnp.dot(o_ref[...], wbr_ref[i], preferred_element_type=F32)
        t = gate_ref[:, i * D:(i + 1) * D].astype(F32) * br
        y = t if y is None else y + t
    m = jnp.dot(y.astype(BF16), wo_ref[...], preferred_element_type=F32)
    hn = _ln_rows(alpha * h_ref[...] + m, g_ref[...], b_ref[...])
    h32_ref[...] = hn
    h16 = hn.astype(BF16)
    h16_ref[...] = h16
    logits = lax.dot_general(wr_ref[...], h16, (((1,), (1,)), ((), ())), preferred_element_type=F32)
    aff_ref[...] = 1.0 / (1.0 + jnp.exp(-logits))


def _merge(alpha, od, os_, of, gates, h32, wbr, wo, g, b, wr_t, tm=512):
    T, D = h32.shape
    bw = od.shape[1]
    row = lambda i: (i, 0)
    fixed2 = lambda i: (0, 0)
    return pl.pallas_call(
        functools.partial(_merge_body, alpha),
        out_shape=(jax.ShapeDtypeStruct((T, D), F32), jax.ShapeDtypeStruct((T, D), BF16),
                   jax.ShapeDtypeStruct((N_EXPERTS, T), F32)),
        grid=(T // tm,),
        in_specs=[pl.BlockSpec((tm, bw), row), pl.BlockSpec((tm, bw), row), pl.BlockSpec((tm, bw), row),
                  pl.BlockSpec((tm, N_BRANCH * D), row), pl.BlockSpec((tm, D), row),
                  pl.BlockSpec((N_BRANCH, bw, D), lambda i: (0, 0, 0)), pl.BlockSpec((D, D), fixed2),
                  pl.BlockSpec((1, D), fixed2), pl.BlockSpec((1, D), fixed2),
                  pl.BlockSpec((N_EXPERTS, D), fixed2)],
        out_specs=(pl.BlockSpec((tm, D), row), pl.BlockSpec((tm, D), row),
                   pl.BlockSpec((N_EXPERTS, tm), lambda i: (0, i))),
        compiler_params=_cparams(("arbitrary",)),
        name="merge",
    )(od, os_, of, gates, h32, wbr, wo, g.reshape(1, D), b.reshape(1, D), wr_t)


def _route_body(aff_ref, bias_ref, e_ref, w_ref):
    aff = aff_ref[...]
    sel = aff + bias_ref[...]
    sub = lax.broadcasted_iota(jnp.int32, (EXPERTS_PER_GROUP, aff.shape[1]), 0)
    big = jnp.int32(EXPERTS_PER_GROUP)
    best = None
    for gidx in range(N_GROUPS):
        rs = slice(gidx * EXPERTS_PER_GROUP, (gidx + 1) * EXPERTS_PER_GROUP)
        s8 = sel[rs]
        a8 = aff[rs]
        m1 = jnp.max(s8, axis=0, keepdims=True)
        i1 = jnp.min(jnp.where(s8 == m1, sub, big), axis=0, keepdims=True)
        rest = jnp.where(sub == i1, -jnp.inf, s8)
        m2 = jnp.max(rest, axis=0, keepdims=True)
        i2 = jnp.min(jnp.where(rest == m2, sub, big), axis=0, keepdims=True)
        a1 = jnp.sum(jnp.where(sub == i1, a8, 0.0), axis=0, keepdims=True)
        a2 = jnp.sum(jnp.where(sub == i2, a8, 0.0), axis=0, keepdims=True)
        score = m1 + m2
        cand = (score, i1 + gidx * EXPERTS_PER_GROUP, i2 + gidx * EXPERTS_PER_GROUP, a1, a2)
        if best is None:
            best = cand
        else:
            take = score > best[0]
            best = tuple(jnp.where(take, c, o) for c, o in zip(cand, best))
    _, e1, e2, a1, a2 = best
    tot = a1 + a2
    e_ref[0:1, :] = e1
    e_ref[1:2, :] = e2
    w_ref[0:1, :] = a1 / tot
    w_ref[1:2, :] = a2 / tot


def _route(aff_t, b_router, tn=2048):
    E, T = aff_t.shape
    tn = min(tn, T)
    return pl.pallas_call(
        _route_body,
        out_shape=(jax.ShapeDtypeStruct((TOP_K, T), jnp.int32), jax.ShapeDtypeStruct((TOP_K, T), F32)),
        grid=(T // tn,),
        in_specs=[pl.BlockSpec((E, tn), lambda i: (0, i)), pl.BlockSpec((E, 1), lambda i: (0, 0))],
        out_specs=(pl.BlockSpec((TOP_K, tn), lambda i: (0, i)), pl.BlockSpec((TOP_K, tn), lambda i: (0, i))),
        compiler_params=_cparams(("arbitrary",)),
        name="route",
    )(aff_t, b_router.reshape(E, 1))


def _expert_body(be_ref, bc_ref, tok0_ref, tokn_ref, slot_ref, x_hbm, wg_ref, wu_ref, wd_ref, y_hbm,
                 xbuf, ybuf, gsem, ssem):
    i = pl.program_id(0)
    nb = pl.num_programs(0)
    cur = i & 1
    bm = xbuf.shape[1]
    n_real = y_hbm.shape[0] - 2 * bm

    def start_gather(tok_ref, slot):
        for r in range(bm):
            pltpu.make_async_copy(x_hbm.at[pl.ds(tok_ref[0, 0, r], 1), :], xbuf.at[slot, pl.ds(r, 1), :],
                                  gsem.at[slot]).start()

    def wait_gather(slot):
        pltpu.make_async_copy(x_hbm.at[pl.ds(0, bm), :], xbuf.at[slot], gsem.at[slot]).wait()

    def wait_scatter(slot):
        pltpu.make_async_copy(ybuf.at[slot], y_hbm.at[pl.ds(0, bm), :], ssem.at[slot]).wait()

    @pl.when(i == 0)
    def _():
        ybuf[...] = jnp.zeros_like(ybuf)
        for s in range(2):
            spare = pltpu.make_async_copy(ybuf.at[s], y_hbm.at[pl.ds(n_real + s * bm, bm), :], ssem.at[s])
            spare.start()
            spare.wait()
        start_gather(tok0_ref, 0)

    @pl.when(i + 1 < nb)
    def _():
        start_gather(tokn_ref, 1 - cur)

    wait_gather(cur)

    @pl.when(i >= 2)
    def _():
        wait_scatter(cur)

    @pl.when(bc_ref[i] > 0)
    def _():
        x = xbuf[cur].astype(BF16)
        gate = jnp.dot(x, wg_ref[0], preferred_element_type=F32)
        up = jnp.dot(x, wu_ref[0], preferred_element_type=F32)
        hmid = (gate * (1.0 / (1.0 + jnp.exp(-gate)))) * up
        ybuf[cur] = jnp.dot(hmid.astype(BF16), wd_ref[0], preferred_element_type=F32)

    for r in range(bm):
        pltpu.make_async_copy(ybuf.at[cur, pl.ds(r, 1), :], y_hbm.at[pl.ds(slot_ref[0, 0, r], 1), :],
                              ssem.at[cur]).start()

    @pl.when(i == nb - 1)
    def _():
        wait_scatter(cur)

        @pl.when(nb > 1)
        def _():
            wait_scatter(1 - cur)


def _experts(h32, blk_e, blk_cnt, tok_pad, slot_pad, wg, wu, wd, bm):
    T, D = h32.shape
    nb = blk_e.shape[0]
    de = wg.shape[2]
    smem_blk = lambda f: pl.BlockSpec((1, 1, bm), f, memory_space=pltpu.SMEM)
    return pl.pallas_call(
        _expert_body,
        out_shape=jax.ShapeDtypeStruct((TOP_K * T + 2 * bm, D), F32),
        grid_spec=pltpu.PrefetchScalarGridSpec(
            num_scalar_prefetch=2,
            grid=(nb,),
            in_specs=[smem_blk(lambda i, be, bc: (0, 0, 0)),
                      smem_blk(lambda i, be, bc: (jnp.minimum(i + 1, nb - 1), 0, 0)),
                      smem_blk(lambda i, be, bc: (i, 0, 0)),
                      pl.BlockSpec(memory_space=pl.ANY),
                      pl.BlockSpec((1, D, de), lambda i, be, bc: (be[i], 0, 0)),
                      pl.BlockSpec((1, D, de), lambda i, be, bc: (be[i], 0, 0)),
                      pl.BlockSpec((1, de, D), lambda i, be, bc: (be[i], 0, 0))],
            out_specs=pl.BlockSpec(memory_space=pl.ANY),
            scratch_shapes=[pltpu.VMEM((2, bm, D), F32), pltpu.VMEM((2, bm, D), F32),
                            pltpu.SemaphoreType.DMA((2,)), pltpu.SemaphoreType.DMA((2,))]),
        compiler_params=_cparams(("arbitrary",)),
        name="experts",
    )(blk_e, blk_cnt, tok_pad, tok_pad, slot_pad, h32, wg, wu, wd)


def _dispatch_tables(e_t, T, bm):
    A = TOP_K * T
    flat_e = e_t.T.reshape(A)
    order = jnp.argsort(flat_e, stable=True).astype(jnp.int32)
    counts = jnp.bincount(flat_e, length=N_EXPERTS).astype(jnp.int32)
    start = jnp.cumsum(counts) - counts
    nblk_e = (counts + bm - 1) // bm
    blk_end = jnp.cumsum(nblk_e)
    nb = A // bm + N_EXPERTS
    bidx = jnp.arange(nb, dtype=jnp.int32)
    blk_e = jnp.minimum(jnp.searchsorted(blk_end, bidx, side="right"), N_EXPERTS - 1).astype(jnp.int32)
    off = (bidx - (blk_end - nblk_e)[blk_e]) * bm
    blk_cnt = jnp.where(bidx < blk_end[-1], jnp.clip(counts[blk_e] - off, 0, bm), 0).astype(jnp.int32)
    r = jnp.arange(bm, dtype=jnp.int32)[None, :]
    real = r < blk_cnt[:, None]
    a = order[jnp.clip((start[blk_e] + off)[:, None] + r, 0, A - 1)]
    tok_pad = jnp.where(real, a // TOP_K, 0)
    slot_pad = jnp.where(real, (a % TOP_K) * T + a // TOP_K, A + (bidx % 2)[:, None] * bm + r)
    return blk_e, blk_cnt, tok_pad.reshape(nb, 1, bm), slot_pad.reshape(nb, 1, bm)


def _combine_body(alpha, h_ref, y0_ref, y1_ref, w_ref, g_ref, b_ref, o32_ref, o16_ref):
    w = w_ref[...]
    f = w[:, 0:1] * y0_ref[...] + w[:, 1:2] * y1_ref[...]
    hn = _ln_rows(alpha * h_ref[...] + f, g_ref[...], b_ref[...])
    o32_ref[...] = hn
    o16_ref[...] = hn.astype(BF16)


def _combine(alpha, h32, y2, w_rows, g, b, tm=512):
    T, D = h32.shape
    nt = T // tm
    row = lambda i: (i, 0)
    fixed = lambda i: (0, 0)
    return pl.pallas_call(
        functools.partial(_combine_body, alpha),
        out_shape=(jax.ShapeDtypeStruct((T, D), F32), jax.ShapeDtypeStruct((T, D), BF16)),
        grid=(nt,),
        in_specs=[pl.BlockSpec((tm, D), row), pl.BlockSpec((tm, D), row),
                  pl.BlockSpec((tm, D), lambda i: (nt + i, 0)),
                  pl.BlockSpec((tm, TOP_K), row), pl.BlockSpec((1, D), fixed), pl.BlockSpec((1, D), fixed)],
        out_specs=(pl.BlockSpec((tm, D), row), pl.BlockSpec((tm, D), row)),
        compiler_params=_cparams(("arbitrary",)),
        name="combine",
    )(h32, y2, y2, w_rows, g.reshape(1, D), b.reshape(1, D))


def _rotate_half_columns(w):
    d = w.shape[0]
    w4 = w.reshape(d, -1, 2, HEAD_DIM // 2)
    return jnp.concatenate([-w4[:, :, 1:2], w4[:, :, 0:1]], axis=2).reshape(w.shape)


def _prep_in_weights(w_in_l):
    qk_w = DIFF_HEADS * 2 * HEAD_DIM
    sizes = (qk_w, qk_w, DIFF_HEADS * 2 * HEAD_DIM, 512, 512, 512, IDX_HEADS * HEAD_DIM, HEAD_DIM, IDX_HEADS,
             512, 512, 512, FOX_HEADS, N_BRANCH * w_in_l.shape[0])
    offs = [0]
    for s in sizes:
        offs.append(offs[-1] + s)
    dq, dk, dv, sq, sk, sv, iq, ik, iw, fq, fk, fv, ff, g = (w_in_l[:, offs[i]:offs[i + 1]] for i in range(len(sizes)))
    scale = HEAD_DIM ** -0.5
    idx_scale = (IDX_HEADS * HEAD_DIM) ** -0.5
    w_rope = jnp.concatenate([dq * scale, dk, sq * scale, sk, iq, ik, ik, ik, ik], axis=1)
    w_rot = _rotate_half_columns(w_rope)
    w_plain = jnp.concatenate([fq * scale, fk], axis=1)
    w_vt = jnp.concatenate([dv, sv, fv], axis=1).T
    pad = jnp.zeros((w_in_l.shape[0], LANES - IDX_HEADS - FOX_HEADS), w_in_l.dtype)
    w_small = jnp.concatenate([iw * idx_scale, ff, pad], axis=1)
    return (w_rope.astype(BF16), w_rot.astype(BF16), w_plain.astype(BF16), w_vt.astype(BF16), g.astype(BF16),
            w_small.astype(BF16))


def _rope_tables(positions):
    inv = ROPE_THETA ** (-jnp.arange(0, HEAD_DIM, 2, dtype=F32) / HEAD_DIM)
    ang = positions.astype(F32).reshape(-1, 1) * inv
    c, s = jnp.cos(ang), jnp.sin(ang)
    return jnp.tile(c, (1, LANES // (HEAD_DIM // 2))), jnp.tile(s, (1, LANES // (HEAD_DIM // 2)))


def kernel(x, positions, ln_in_g, ln_in_b, w_in, b_forget, diff_lam, diff_norm_g, w_br, w_o, ln1_g, ln1_b,
           w_router, b_router, w_gate, w_up, w_down, ln2_g, ln2_b):
    B, S, D = x.shape
    T = B * S
    depth = w_in.shape[0]
    alpha = (2 * depth) ** 0.25
    topk = min(DSA_TOPK_MAX, S // 4)
    bm = 256
    cc, ss = _rope_tables(positions)
    wr_t = w_router.T.astype(BF16)
    h32, h16 = _layer_norm(x.reshape(T, D), ln_in_g, ln_in_b)
    for l in range(depth):
        lam_init = 0.8 - 0.6 * math.exp(-0.3 * l)
        w_rope, w_rot, w_plain, w_vt, w_g, w_small = _prep_in_weights(w_in[l])
        qk = _project(h16, w_rope, "rope", BF16, tables=(cc, ss), wrot=w_rot)
        fqk = _project(h16, w_plain, "plain", BF16)
        vt = _project(h16, w_vt, "transposed", BF16)
        gates = _project(h16, w_g, "sigmoid", BF16)
        small = _project(h16, w_small, "f32", F32)

        o_diff = _attention("diff", qk, QK_DQ, qk, QK_DK, vt, VT_DV, B, S,
                            (diff_lam[l], diff_norm_g[l]), lam_init=lam_init)
        small_t = _project(h16, w_small.T, "transposed", F32)
        mask_t = _dsa_select(qk, small_t, B, S, topk)
        o_dsa = _attention("dsa", qk, QK_SQ, qk, QK_SK, vt, VT_SV, B, S, (mask_t,))
        bias_row = jnp.zeros((1, LANES), F32).at[0, SM_FF:SM_FF + FOX_HEADS].set(b_forget[l])
        c, crep = _forget_cumsum(small, bias_row, B, S)
        ct = jnp.transpose(c[:, SM_FF:SM_FF + FOX_HEADS].reshape(B, S, FOX_HEADS), (0, 2, 1))
        o_fox = _attention("fox", fqk, FQK_FQ, fqk, FQK_FK, vt, VT_FV, B, S, (ct, crep))

        h32, h16, aff_t = _merge(alpha, o_diff, o_dsa, o_fox, gates, h32, w_br[l].astype(BF16),
                                 w_o[l].astype(BF16), ln1_g[l], ln1_b[l], wr_t)
        e_t, w_t = _route(aff_t, b_router)
        blk_e, blk_cnt, tok_pad, slot_pad = _dispatch_tables(e_t, T, bm)
        y2 = _experts(h32, blk_e, blk_cnt, tok_pad, slot_pad, w_gate[l].astype(BF16), w_up[l].astype(BF16),
                      w_down[l].astype(BF16), bm)
        h32, h16 = _combine(alpha, h32, y2, w_t.T, ln2_g[l], ln2_b[l])
    return h32.reshape(B, S, D)
```

```python
import functools
import math

import jax
import jax.numpy as jnp
from jax import lax
from jax.experimental import pallas as pl
from jax.experimental.pallas import tpu as pltpu

F32 = jnp.float32
BF16 = jnp.bfloat16

CHUNK = 64
HEAD_DIM = 64
ROPE_THETA = 10000.0
LN_EPS = 1e-5
DIFF_HEADS = 4
DSA_HEADS = 8
IDX_HEADS = 4
DSA_TOPK_MAX = 256
FOX_HEADS = 8
N_BRANCH = 3
BRANCH_WIDTH = 512
N_EXPERTS = 32
N_GROUPS = 4
EXPERTS_PER_GROUP = N_EXPERTS // N_GROUPS
TOP_K = 2

LANES = 128
NEG = -1e30
INT_MIN = -2 ** 31
HALF = 1 << 15

QK_DQ, QK_DK, QK_SQ, QK_SK, QK_IQ, QK_IK = 0, 512, 1024, 1536, 2048, 2304
QK_WIDTH = 2560
FQK_FQ, FQK_FK = 0, 512
VT_DV, VT_SV, VT_FV = 0, 512, 1024
SM_IW, SM_FF = 0, 4
VMEM_LIMIT = 56 * 1024 * 1024


def _cparams(sem):
    return pltpu.CompilerParams(dimension_semantics=sem, vmem_limit_bytes=VMEM_LIMIT)


def _ln_rows(x, g, b):
    mu = jnp.mean(x, axis=-1, keepdims=True)
    xc = x - mu
    var = jnp.mean(xc * xc, axis=-1, keepdims=True)
    return xc * lax.rsqrt(var + LN_EPS) * g + b


def _ln_body(x_ref, g_ref, b_ref, o32_ref, o16_ref):
    y = _ln_rows(x_ref[...], g_ref[...], b_ref[...])
    o32_ref[...] = y
    o16_ref[...] = y.astype(BF16)


def _layer_norm(x, g, b, tm=512):
    T, D = x.shape
    return pl.pallas_call(
        _ln_body,
        out_shape=(jax.ShapeDtypeStruct((T, D), F32), jax.ShapeDtypeStruct((T, D), BF16)),
        grid=(T // tm,),
        in_specs=[pl.BlockSpec((tm, D), lambda i: (i, 0)),
                  pl.BlockSpec((1, D), lambda i: (0, 0)),
                  pl.BlockSpec((1, D), lambda i: (0, 0))],
        out_specs=(pl.BlockSpec((tm, D), lambda i: (i, 0)), pl.BlockSpec((tm, D), lambda i: (i, 0))),
        compiler_params=_cparams(("arbitrary",)),
        name="layer_norm",
    )(x, g.reshape(1, D), b.reshape(1, D))


def _proj_body(mode, width, h_ref, w_ref, *rest):
    if mode == "rope":
        wrot_ref, cc_ref, ss_ref, o_ref = rest
    else:
        (o_ref,) = rest
    h = h_ref[...]
    tn = 512 if width % 512 == 0 else width
    for j in range(width // tn):
        cs = slice(j * tn, (j + 1) * tn)
        if mode == "transposed":
            o_ref[cs, :] = lax.dot_general(w_ref[cs, :], h, (((1,), (1,)), ((), ())),
                                           preferred_element_type=F32).astype(o_ref.dtype)
            continue
        z = jnp.dot(h, w_ref[:, cs], preferred_element_type=F32)
        if mode == "rope":
            zr = jnp.dot(h, wrot_ref[:, cs], preferred_element_type=F32)
            cc = cc_ref[...]
            ss = ss_ref[...]
            for i in range(tn // LANES):
                ls = slice(i * LANES, (i + 1) * LANES)
                o_ref[:, j * tn + i * LANES:j * tn + (i + 1) * LANES] = (
                    z[:, ls] * cc + zr[:, ls] * ss).astype(o_ref.dtype)
        elif mode == "sigmoid":
            o_ref[:, cs] = (1.0 / (1.0 + jnp.exp(-z))).astype(o_ref.dtype)
        else:
            o_ref[:, cs] = z.astype(o_ref.dtype)


def _project(h16, w, mode, out_dtype, tables=None, wrot=None, tm=512):
    T, D = h16.shape
    transposed = mode == "transposed"
    width = w.shape[0] if transposed else w.shape[1]
    row = lambda i: (i, 0)
    fixed = lambda i: (0, 0)
    in_specs = [pl.BlockSpec((tm, D), row), pl.BlockSpec(w.shape, fixed)]
    args = [h16, w]
    if mode == "rope":
        cc, ss = tables
        in_specs += [pl.BlockSpec((D, width), fixed), pl.BlockSpec((tm, LANES), row),
                     pl.BlockSpec((tm, LANES), row)]
        args += [wrot, cc, ss]
    return pl.pallas_call(
        functools.partial(_proj_body, mode, width),
        out_shape=jax.ShapeDtypeStruct((width, T) if transposed else (T, width), out_dtype),
        grid=(T // tm,),
        in_specs=in_specs,
        out_specs=pl.BlockSpec((width, tm), lambda i: (0, i)) if transposed else pl.BlockSpec((tm, width), row),
        compiler_params=_cparams(("arbitrary",)),
        name="proj_" + mode,
    )(*args)


def _cumsum_body(x_ref, bias_ref, tri_ref, c_ref, crep_ref, carry_ref):
    @pl.when(pl.program_id(1) == 0)
    def _():
        carry_ref[...] = jnp.zeros_like(carry_ref)

    x = x_ref[...] + bias_ref[...]
    lf = jnp.minimum(x, 0.0) - jnp.log1p(jnp.exp(-jnp.abs(x)))
    hi = lf.astype(BF16)
    r1 = lf - hi.astype(F32)
    mid = r1.astype(BF16)
    lo = (r1 - mid.astype(F32)).astype(BF16)
    tri = tri_ref[...]
    c = (jnp.dot(tri, hi, preferred_element_type=F32) + jnp.dot(tri, mid, preferred_element_type=F32)
         + jnp.dot(tri, lo, preferred_element_type=F32)) + carry_ref[...]
    c_ref[...] = c
    carry_ref[...] = c[-1:, :]
    for h in range(FOX_HEADS):
        crep_ref[:, h * LANES:(h + 1) * LANES] = jnp.broadcast_to(c[:, SM_FF + h:SM_FF + h + 1], (c.shape[0], LANES))


def _forget_cumsum(small, bias_row, B, S, tb=512):
    T = small.shape[0]
    nb = S // tb
    tri = jnp.tril(jnp.ones((tb, tb), F32)).astype(BF16)
    return pl.pallas_call(
        _cumsum_body,
        out_shape=(jax.ShapeDtypeStruct((T, LANES), F32), jax.ShapeDtypeStruct((T, FOX_HEADS * LANES), F32)),
        grid=(B, nb),
        in_specs=[pl.BlockSpec((tb, LANES), lambda b, i: (b * nb + i, 0)),
                  pl.BlockSpec((1, LANES), lambda b, i: (0, 0)),
                  pl.BlockSpec((tb, tb), lambda b, i: (0, 0))],
        out_specs=(pl.BlockSpec((tb, LANES), lambda b, i: (b * nb + i, 0)),
                   pl.BlockSpec((tb, FOX_HEADS * LANES), lambda b, i: (b * nb + i, 0))),
        scratch_shapes=[pltpu.VMEM((1, LANES), F32)],
        compiler_params=_cparams(("arbitrary", "arbitrary")),
        name="forget_cumsum",
    )(small, bias_row, tri)


def _attn_body(mode, nslab, lam_init, qt_ref, kt_ref, q_ref, k_ref, vt_ref, *rest):
    if mode == "diff":
        lam_ref, g_ref, o_ref, m_sc, acc_sc = rest
    elif mode == "dsa":
        mask_ref, o_ref, m_sc, acc_sc = rest
    else:
        cq_ref, ck_ref, o_ref, m_sc, acc_sc = rest
    p = pl.program_id(1)
    qi = qt_ref[p]
    ki = kt_ref[p]
    tq = q_ref.shape[0]
    tk = k_ref.shape[0]

    @pl.when(ki == 0)
    def _():
        m_sc[...] = jnp.full_like(m_sc, NEG)
        acc_sc[...] = jnp.zeros_like(acc_sc)

    lane = lax.broadcasted_iota(jnp.int32, (1, LANES), 1)

    def tile_update(diag):
        if mode == "dsa":
            keep_all = mask_ref[...] != 0
        elif diag:
            krow = lax.broadcasted_iota(jnp.int32, (tk, tq), 0)
            qcol = lax.broadcasted_iota(jnp.int32, (tk, tq), 1)
            if mode == "diff":
                keep_all = (krow >> 6) <= (qcol >> 6)
            else:
                keep_all = krow <= qcol
        def scores(idx):
            ls = slice((idx // 2) * LANES, (idx // 2 + 1) * LANES)
            q = q_ref[:, ls]
            half = (lane < HEAD_DIM) if idx % 2 == 0 else (lane >= HEAD_DIM)
            qm = jnp.where(half, q, jnp.zeros_like(q))
            return lax.dot_general(k_ref[:, ls], qm, (((1,), (1,)), ((), ())), preferred_element_type=F32)

        ones_rows = jnp.ones((8, tk), BF16)
        s_next = scores(0)
        for idx in range(2 * nslab):
            s = s_next
            if idx + 1 < 2 * nslab:
                s_next = scores(idx + 1)
            vt = jnp.concatenate([vt_ref[(idx // 2) * LANES:(idx // 2 + 1) * LANES, :], ones_rows], axis=0)
            if mode == "fox":
                ck = ck_ref[:, idx * LANES:(idx + 1) * LANES]
                s = s - jnp.concatenate([ck] * (tq // LANES), axis=1)
            if mode == "dsa" or diag:
                s = jnp.where(keep_all, s, NEG)
            m_old = m_sc[idx:idx + 1, :]
            mx = jnp.max(s, axis=0, keepdims=True)
            if mode == "fox":
                cq = cq_ref[0, idx:idx + 1, :]
                m_new = jnp.maximum(m_old, mx + cq)
                shift = cq - m_new
            else:
                m_new = jnp.maximum(m_old, mx)
                shift = -m_new
            alpha = jnp.exp(m_old - m_new)
            pr = jnp.exp(s + shift).astype(BF16)
            acc_sc[idx] = alpha * acc_sc[idx] + jnp.dot(vt, pr, preferred_element_type=F32)
            m_sc[idx:idx + 1, :] = m_new

    if mode == "dsa":
        tile_update(False)
    else:
        @pl.when(ki == qi)
        def _():
            tile_update(True)

        @pl.when(ki != qi)
        def _():
            tile_update(False)

    @pl.when(ki == qi)
    def _():
        if mode == "diff":
            lm = lam_ref[...]
            lam = (jnp.exp(jnp.sum(lm[0:1] * lm[1:2], keepdims=True))
                   - jnp.exp(jnp.sum(lm[2:3] * lm[3:4], keepdims=True)) + lam_init)
        feat = lax.broadcasted_iota(jnp.int32, (LANES, 1), 0)
        for sl in range(nslab):
            o0 = acc_sc[2 * sl, 0:LANES, :] / acc_sc[2 * sl, LANES:LANES + 1, :]
            o1 = acc_sc[2 * sl + 1, 0:LANES, :] / acc_sc[2 * sl + 1, LANES:LANES + 1, :]
            if mode == "diff":
                o = (o0 - lam * o1).T
                o = o * lax.rsqrt(jnp.mean(o * o, axis=-1, keepdims=True) + LN_EPS)
                o = (o * g_ref[...]) * (1.0 - lam_init)
            else:
                o = jnp.where(feat < HEAD_DIM, o0, o1).T
            o_ref[:, sl * LANES:(sl + 1) * LANES] = o.astype(o_ref.dtype)


def _attention(mode, qarr, qcol, karr, kcol, vt, vrow, B, S, extra, lam_init=0.0, tile=512):
    nslab = 4
    width = nslab * LANES
    nq = S // tile
    pairs = [(qi, ki) for qi in range(nq) for ki in range(qi + 1)]
    qt = jnp.asarray([pq for pq, _ in pairs], jnp.int32)
    kt = jnp.asarray([pk for _, pk in pairs], jnp.int32)
    T = B * S
    qmap = lambda b, p, qt, kt: (b * nq + qt[p], qcol // width)
    kmap = lambda b, p, qt, kt: (b * nq + kt[p], kcol // width)
    vmap = lambda b, p, qt, kt: (vrow // width, b * nq + kt[p])
    in_specs = [pl.BlockSpec((tile, width), qmap), pl.BlockSpec((tile, width), kmap),
                pl.BlockSpec((width, tile), vmap)]
    args = [qarr, karr, vt]
    if mode == "diff":
        lam4, g = extra
        in_specs += [pl.BlockSpec((4, HEAD_DIM), lambda b, p, qt, kt: (0, 0)),
                     pl.BlockSpec((1, LANES), lambda b, p, qt, kt: (0, 0))]
        args += [lam4, g.reshape(1, LANES)]
    elif mode == "dsa":
        (mask_t,) = extra
        in_specs += [pl.BlockSpec((tile, tile), lambda b, p, qt, kt: (b * nq + kt[p], qt[p]))]
        args += [mask_t]
    else:
        ct, crep = extra
        in_specs += [pl.BlockSpec((1, FOX_HEADS, tile), lambda b, p, qt, kt: (b, 0, qt[p])),
                     pl.BlockSpec((tile, FOX_HEADS * LANES), lambda b, p, qt, kt: (b * nq + kt[p], 0))]
        args += [ct, crep]
    return pl.pallas_call(
        functools.partial(_attn_body, mode, nslab, lam_init),
        out_shape=jax.ShapeDtypeStruct((T, width), BF16),
        grid_spec=pltpu.PrefetchScalarGridSpec(
            num_scalar_prefetch=2,
            grid=(B, len(pairs)),
            in_specs=in_specs,
            out_specs=pl.BlockSpec((tile, width), lambda b, p, qt, kt: (b * nq + qt[p], 0)),
            scratch_shapes=[pltpu.VMEM((2 * nslab, tile), F32),
                            pltpu.VMEM((2 * nslab, LANES + 8, tile), F32)]),
        compiler_params=_cparams(("arbitrary", "arbitrary")),
        name="attn_" + mode,
    )(qt, kt, *args)


def _select_body(topk, iq_ref, ik_ref, iwt_ref, tri_ref, mask_ref, keys_ref, hi_ref, lo_ref):
    qi = pl.program_id(1)
    tq = iq_ref.shape[0]
    tk = tq
    nkb = qi + 1
    q0 = qi * tq
    lane = lax.broadcasted_iota(jnp.int32, (1, iq_ref.shape[1]), 1)
    iq = iq_ref[...]
    qms = [jnp.where((lane >> 6) == h, iq, jnp.zeros_like(iq)) for h in range(IDX_HEADS)]
    iwt = iwt_ref[...]
    q_chunk = (q0 + lax.broadcasted_iota(jnp.int32, (tk, tq), 1)) >> 6
    k_iota = lax.broadcasted_iota(jnp.int32, (tk, tq), 0)

    mask_ref[...] = jnp.zeros_like(mask_ref)

    def score_block(j, carry):
        k0 = pl.multiple_of(j * tk, tk)
        ik = ik_ref[pl.ds(k0, tk), :]
        score = jnp.zeros((tk, tq), F32)
        for h in range(IDX_HEADS):
            lg = lax.dot_general(ik, qms[h], (((1,), (1,)), ((), ())), preferred_element_type=F32)
            score = score + iwt[SM_IW + h:SM_IW + h + 1, :] * jnp.maximum(lg, 0.0)
        score = jnp.where(score == 0.0, 0.0, score)
        bits = lax.bitcast_convert_type(score, jnp.int32)
        key = bits ^ ((bits >> 31) & jnp.int32(0x7FFFFFFF))
        vis = ((k0 + k_iota) >> 6) <= q_chunk
        key = jnp.where(vis, key, jnp.int32(INT_MIN))
        keys_ref[pl.ds(k0, tk), :] = key
        hi_ref[pl.ds(k0, tk), :] = (key >> 16).astype(jnp.int16)
        lo_ref[pl.ds(k0, tk), :] = ((key & jnp.int32(0xFFFF)) - HALF).astype(jnp.int16)
        return carry

    lax.fori_loop(0, nkb, score_block, 0)

    def count16(ref, pred):
        def body(j, acc):
            k0 = pl.multiple_of(j * tk, tk)
            hit = jnp.where(pred(ref[pl.ds(k0, tk), :]), jnp.int16(1), jnp.int16(0))
            for r in range(tk // 16):
                acc = acc + hit[r * 16:(r + 1) * 16, :]
            return acc
        acc = lax.fori_loop(0, nkb, body, jnp.zeros((16, tq), jnp.int16))
        return jnp.sum(acc.astype(jnp.int32), axis=0, keepdims=True)

    def search16(ref, base):
        def bit_step(i, lo):
            cand = lo + (jnp.int32(1) << (15 - i))
            c16 = cand.astype(jnp.int16)
            return jnp.where(base + count16(ref, lambda blk: blk >= c16) >= topk, cand, lo)
        return lax.fori_loop(0, 16, bit_step, jnp.full((1, tq), -HALF, jnp.int32))

    thr_hi = search16(hi_ref, 0)
    thr_hi16 = thr_hi.astype(jnp.int16)
    above = count16(hi_ref, lambda blk: blk > thr_hi16)

    def keep_low_of_threshold_rows(j, carry):
        k0 = pl.multiple_of(j * tk, tk)
        lo_ref[pl.ds(k0, tk), :] = jnp.where(hi_ref[pl.ds(k0, tk), :] == thr_hi16, lo_ref[pl.ds(k0, tk), :],
                                             jnp.int16(-HALF))
        return carry

    lax.fori_loop(0, nkb, keep_low_of_threshold_rows, 0)
    thr_lo = search16(lo_ref, above)
    thr_lo16 = thr_lo.astype(jnp.int16)
    above = above + count16(lo_ref, lambda blk: blk > thr_lo16)
    thr = (thr_hi << 16) + (thr_lo + HALF)
    thr = jnp.maximum(thr, jnp.int32(INT_MIN + 1))
    need = (topk - above).astype(F32)
    tri = tri_ref[...]

    def select_block(j, carry):
        k0 = pl.multiple_of(j * tk, tk)
        key = keys_ref[pl.ds(k0, tk), :]
        eq = key == thr
        incl = jnp.dot(tri, jnp.where(eq, 1.0, 0.0).astype(BF16), preferred_element_type=F32)
        sel = (key > thr) | (eq & ((carry + incl) <= need))
        mask_ref[pl.ds(k0, tk), :] = jnp.where(sel, 1, 0).astype(jnp.int8)
        return carry + incl[tk - 1:tk, :]

    lax.fori_loop(0, nkb, select_block, jnp.zeros((1, tq), F32))


def _dsa_select(qk, small_t, B, S, topk, tq=256):
    T = B * S
    nq = S // tq
    tri = jnp.tril(jnp.ones((tq, tq), F32)).astype(BF16)
    return pl.pallas_call(
        functools.partial(_select_body, topk),
        out_shape=jax.ShapeDtypeStruct((T, S), jnp.int8),
        grid=(B, nq),
        in_specs=[pl.BlockSpec((tq, 256), lambda b, i: (b * nq + i, QK_IQ // 256)),
                  pl.BlockSpec((S, 256), lambda b, i: (b, QK_IK // 256)),
                  pl.BlockSpec((8, tq), lambda b, i: (0, b * nq + i)),
                  pl.BlockSpec((tq, tq), lambda b, i: (0, 0))],
        out_specs=pl.BlockSpec((S, tq), lambda b, i: (b, i)),
        scratch_shapes=[pltpu.VMEM((S, tq), jnp.int32), pltpu.VMEM((S, tq), jnp.int16),
                        pltpu.VMEM((S, tq), jnp.int16)],
        compiler_params=_cparams(("arbitrary", "arbitrary")),
        name="dsa_select",
    )(qk, qk, small_t, tri)


def _merge_body(alpha, od_ref, os_ref, of_ref, gate_ref, h_ref, wbr_ref, wo_ref, g_ref, b_ref, wr_ref,
                h32_ref, h16_ref, aff_ref):
    D = h_ref.shape[1]
    y = None
    for i, o_ref in enumerate((od_ref, os_ref, of_ref)):
        br = jnp.dot(o_ref[...], wbr_ref[i], preferred_element_type=F32)
        t = gate_ref[:, i * D:(i + 1) * D].astype(F32) * br
        y = t if y is None else y + t
    m = jnp.dot(y.astype(BF16), wo_ref[...], preferred_element_type=F32)
    hn = _ln_rows(alpha * h_ref[...] + m, g_ref[...], b_ref[...])
    h32_ref[...] = hn
    h16 = hn.astype(BF16)
    h16_ref[...] = h16
    logits = lax.dot_general(wr_ref[...], h16, (((1,), (1,)), ((), ())), preferred_element_type=F32)
    aff_ref[...] = 1.0 / (1.0 + jnp.exp(-logits))


def _merge(alpha, od, os_, of, gates, h32, wbr, wo, g, b, wr_t, tm=512):
    T, D = h32.shape
    bw = od.shape[1]
    row = lambda i: (i, 0)
    fixed2 = lambda i: (0, 0)
    return pl.pallas_call(
        functools.partial(_merge_body, alpha),
        out_shape=(jax.ShapeDtypeStruct((T, D), F32), jax.ShapeDtypeStruct((T, D), BF16),
                   jax.ShapeDtypeStruct((N_EXPERTS, T), F32)),
        grid=(T // tm,),
        in_specs=[pl.BlockSpec((tm, bw), row), pl.BlockSpec((tm, bw), row), pl.BlockSpec((tm, bw), row),
                  pl.BlockSpec((tm, N_BRANCH * D), row), pl.BlockSpec((tm, D), row),
                  pl.BlockSpec((N_BRANCH, bw, D), lambda i: (0, 0, 0)), pl.BlockSpec((D, D), fixed2),
                  pl.BlockSpec((1, D), fixed2), pl.BlockSpec((1, D), fixed2),
                  pl.BlockSpec((N_EXPERTS, D), fixed2)],
        out_specs=(pl.BlockSpec((tm, D), row), pl.BlockSpec((tm, D), row),
                   pl.BlockSpec((N_EXPERTS, tm), lambda i: (0, i))),
        compiler_params=_cparams(("arbitrary",)),
        name="merge",
    )(od, os_, of, gates, h32, wbr, wo, g.reshape(1, D), b.reshape(1, D), wr_t)


def _route_body(aff_ref, bias_ref, e_ref, w_ref):
    aff = aff_ref[...]
    sel = aff + bias_ref[...]
    sub = lax.broadcasted_iota(jnp.int32, (EXPERTS_PER_GROUP, aff.shape[1]), 0)
    big = jnp.int32(EXPERTS_PER_GROUP)
    best = None
    for gidx in range(N_GROUPS):
        rs = slice(gidx * EXPERTS_PER_GROUP, (gidx + 1) * EXPERTS_PER_GROUP)
        s8 = sel[rs]
        a8 = aff[rs]
        m1 = jnp.max(s8, axis=0, keepdims=True)
        i1 = jnp.min(jnp.where(s8 == m1, sub, big), axis=0, keepdims=True)
        rest = jnp.where(sub == i1, -jnp.inf, s8)
        m2 = jnp.max(rest, axis=0, keepdims=True)
        i2 = jnp.min(jnp.where(rest == m2, sub, big), axis=0, keepdims=True)
        a1 = jnp.sum(jnp.where(sub == i1, a8, 0.0), axis=0, keepdims=True)
        a2 = jnp.sum(jnp.where(sub == i2, a8, 0.0), axis=0, keepdims=True)
        score = m1 + m2
        cand = (score, i1 + gidx * EXPERTS_PER_GROUP, i2 + gidx * EXPERTS_PER_GROUP, a1, a2)
        if best is None:
            best = cand
        else:
            take = score > best[0]
            best = tuple(jnp.where(take, c, o) for c, o in zip(cand, best))
    _, e1, e2, a1, a2 = best
    tot = a1 + a2
    e_ref[0:1, :] = e1
    e_ref[1:2, :] = e2
    w_ref[0:1, :] = a1 / tot
    w_ref[1:2, :] = a2 / tot


def _route(aff_t, b_router, tn=2048):
    E, T = aff_t.shape
    tn = min(tn, T)
    return pl.pallas_call(
        _route_body,
        out_shape=(jax.ShapeDtypeStruct((TOP_K, T), jnp.int32), jax.ShapeDtypeStruct((TOP_K, T), F32)),
        grid=(T // tn,),
        in_specs=[pl.BlockSpec((E, tn), lambda i: (0, i)), pl.BlockSpec((E, 1), lambda i: (0, 0))],
        out_specs=(pl.BlockSpec((TOP_K, tn), lambda i: (0, i)), pl.BlockSpec((TOP_K, tn), lambda i: (0, i))),
        compiler_params=_cparams(("arbitrary",)),
        name="route",
    )(aff_t, b_router.reshape(E, 1))


def _expert_body(be_ref, bc_ref, tok0_ref, tokn_ref, slot_ref, x_hbm, wg_ref, wu_ref, wd_ref, y_hbm,
                 xbuf, ybuf, gsem, ssem):
    i = pl.program_id(0)
    nb = pl.num_programs(0)
    cur = i & 1
    bm = xbuf.shape[1]
    n_real = y_hbm.shape[0] - 2 * bm

    def start_gather(tok_ref, slot):
        for r in range(bm):
            pltpu.make_async_copy(x_hbm.at[pl.ds(tok_ref[0, 0, r], 1), :], xbuf.at[slot, pl.ds(r, 1), :],
                                  gsem.at[slot]).start()

    def wait_gather(slot):
        pltpu.make_async_copy(x_hbm.at[pl.ds(0, bm), :], xbuf.at[slot], gsem.at[slot]).wait()

    def wait_scatter(slot):
        pltpu.make_async_copy(ybuf.at[slot], y_hbm.at[pl.ds(0, bm), :], ssem.at[slot]).wait()

    @pl.when(i == 0)
    def _():
        ybuf[...] = jnp.zeros_like(ybuf)
        for s in range(2):
            spare = pltpu.make_async_copy(ybuf.at[s], y_hbm.at[pl.ds(n_real + s * bm, bm), :], ssem.at[s])
            spare.start()
            spare.wait()
        start_gather(tok0_ref, 0)

    @pl.when(i + 1 < nb)
    def _():
        start_gather(tokn_ref, 1 - cur)

    wait_gather(cur)

    @pl.when(i >= 2)
    def _():
        wait_scatter(cur)

    @pl.when(bc_ref[i] > 0)
    def _():
        x = xbuf[cur].astype(BF16)
        gate = jnp.dot(x, wg_ref[0], preferred_element_type=F32)
        up = jnp.dot(x, wu_ref[0], preferred_element_type=F32)
        hmid = (gate * (1.0 / (1.0 + jnp.exp(-gate)))) * up
        ybuf[cur] = jnp.dot(hmid.astype(BF16), wd_ref[0], preferred_element_type=F32)

    for r in range(bm):
        pltpu.make_async_copy(ybuf.at[cur, pl.ds(r, 1), :], y_hbm.at[pl.ds(slot_ref[0, 0, r], 1), :],
                              ssem.at[cur]).start()

    @pl.when(i == nb - 1)
    def _():
        wait_scatter(cur)

        @pl.when(nb > 1)
        def _():
            wait_scatter(1 - cur)


def _experts(h32, blk_e, blk_cnt, tok_pad, slot_pad, wg, wu, wd, bm):
    T, D = h32.shape
    nb = blk_e.shape[0]
    de = wg.shape[2]
    smem_blk = lambda f: pl.BlockSpec((1, 1, bm), f, memory_space=pltpu.SMEM)
    return pl.pallas_call(
        _expert_body,
        out_shape=jax.ShapeDtypeStruct((TOP_K * T + 2 * bm, D), F32),
        grid_spec=pltpu.PrefetchScalarGridSpec(
            num_scalar_prefetch=2,
            grid=(nb,),
            in_specs=[smem_blk(lambda i, be, bc: (0, 0, 0)),
                      smem_blk(lambda i, be, bc: (jnp.minimum(i + 1, nb - 1), 0, 0)),
                      smem_blk(lambda i, be, bc: (i, 0, 0)),
                      pl.BlockSpec(memory_space=pl.ANY),
                      pl.BlockSpec((1, D, de), lambda i, be, bc: (be[i], 0, 0)),
                      pl.BlockSpec((1, D, de), lambda i, be, bc: (be[i], 0, 0)),
                      pl.BlockSpec((1, de, D), lambda i, be, bc: (be[i], 0, 0))],
            out_specs=pl.BlockSpec(memory_space=pl.ANY),
            scratch_shapes=[pltpu.VMEM((2, bm, D), F32), pltpu.VMEM((2, bm, D), F32),
                            pltpu.SemaphoreType.DMA((2,)), pltpu.SemaphoreType.DMA((2,))]),
        compiler_params=_cparams(("arbitrary",)),
        name="experts",
    )(blk_e, blk_cnt, tok_pad, tok_pad, slot_pad, h32, wg, wu, wd)


def _dispatch_tables(e_t, T, bm):
    A = TOP_K * T
    flat_e = e_t.T.reshape(A)
    order = jnp.argsort(flat_e, stable=True).astype(jnp.int32)
    counts = jnp.sum(flat_e[None, :] == jnp.arange(N_EXPERTS, dtype=jnp.int32)[:, None], axis=1).astype(jnp.int32)
    start = jnp.cumsum(counts) - counts
    nblk_e = (counts + bm - 1) // bm
    blk_end = jnp.cumsum(nblk_e)
    nb = A // bm + N_EXPERTS
    bidx = jnp.arange(nb, dtype=jnp.int32)
    blk_e = jnp.minimum(jnp.sum(bidx[:, None] >= blk_end[None, :], axis=1), N_EXPERTS - 1).astype(jnp.int32)
    off = (bidx - (blk_end - nblk_e)[blk_e]) * bm
    blk_cnt = jnp.where(bidx < blk_end[-1], jnp.clip(counts[blk_e] - off, 0, bm), 0).astype(jnp.int32)
    r = jnp.arange(bm, dtype=jnp.int32)[None, :]
    real = r < blk_cnt[:, None]
    a = order[jnp.clip((start[blk_e] + off)[:, None] + r, 0, A - 1)]
    tok_pad = jnp.where(real, a // TOP_K, 0)
    slot_pad = jnp.where(real, (a % TOP_K) * T + a // TOP_K, A + (bidx % 2)[:, None] * bm + r)
    return blk_e, blk_cnt, tok_pad.reshape(nb, 1, bm), slot_pad.reshape(nb, 1, bm)


def _combine_body(alpha, h_ref, y0_ref, y1_ref, w_ref, g_ref, b_ref, o32_ref, o16_ref):
    w = w_ref[...]
    f = w[:, 0:1] * y0_ref[...] + w[:, 1:2] * y1_ref[...]
    hn = _ln_rows(alpha * h_ref[...] + f, g_ref[...], b_ref[...])
    o32_ref[...] = hn
    o16_ref[...] = hn.astype(BF16)


def _combine(alpha, h32, y2, w_rows, g, b, tm=512):
    T, D = h32.shape
    nt = T // tm
    row = lambda i: (i, 0)
    fixed = lambda i: (0, 0)
    return pl.pallas_call(
        functools.partial(_combine_body, alpha),
        out_shape=(jax.ShapeDtypeStruct((T, D), F32), jax.ShapeDtypeStruct((T, D), BF16)),
        grid=(nt,),
        in_specs=[pl.BlockSpec((tm, D), row), pl.BlockSpec((tm, D), row),
                  pl.BlockSpec((tm, D), lambda i: (nt + i, 0)),
                  pl.BlockSpec((tm, TOP_K), row), pl.BlockSpec((1, D), fixed), pl.BlockSpec((1, D), fixed)],
        out_specs=(pl.BlockSpec((tm, D), row), pl.BlockSpec((tm, D), row)),
        compiler_params=_cparams(("arbitrary",)),
        name="combine",
    )(h32, y2, y2, w_rows, g.reshape(1, D), b.reshape(1, D))


def _rotate_half_columns(w):
    d = w.shape[0]
    w4 = w.reshape(d, -1, 2, HEAD_DIM // 2)
    return jnp.concatenate([-w4[:, :, 1:2], w4[:, :, 0:1]], axis=2).reshape(w.shape)


def _prep_in_weights(w_in_l):
    qk_w = DIFF_HEADS * 2 * HEAD_DIM
    sizes = (qk_w, qk_w, DIFF_HEADS * 2 * HEAD_DIM, 512, 512, 512, IDX_HEADS * HEAD_DIM, HEAD_DIM, IDX_HEADS,
             512, 512, 512, FOX_HEADS, N_BRANCH * w_in_l.shape[0])
    offs = [0]
    for s in sizes:
        offs.append(offs[-1] + s)
    dq, dk, dv, sq, sk, sv, iq, ik, iw, fq, fk, fv, ff, g = (w_in_l[:, offs[i]:offs[i + 1]] for i in range(len(sizes)))
    scale = HEAD_DIM ** -0.5
    idx_scale = (IDX_HEADS * HEAD_DIM) ** -0.5
    w_rope = jnp.concatenate([dq * scale, dk, sq * scale, sk, iq, ik, ik, ik, ik], axis=1)
    w_rot = _rotate_half_columns(w_rope)
    w_plain = jnp.concatenate([fq * scale, fk], axis=1)
    w_vt = jnp.concatenate([dv, sv, fv], axis=1).T
    pad = jnp.zeros((w_in_l.shape[0], LANES - IDX_HEADS - FOX_HEADS), w_in_l.dtype)
    w_small = jnp.concatenate([iw * idx_scale, ff, pad], axis=1)
    return (w_rope.astype(BF16), w_rot.astype(BF16), w_plain.astype(BF16), w_vt.astype(BF16), g.astype(BF16),
            w_small.astype(BF16))


def _rope_tables(positions):
    inv = ROPE_THETA ** (-jnp.arange(0, HEAD_DIM, 2, dtype=F32) / HEAD_DIM)
    ang = positions.astype(F32).reshape(-1, 1) * inv
    c, s = jnp.cos(ang), jnp.sin(ang)
    return jnp.tile(c, (1, LANES // (HEAD_DIM // 2))), jnp.tile(s, (1, LANES // (HEAD_DIM // 2)))


def kernel(x, positions, ln_in_g, ln_in_b, w_in, b_forget, diff_lam, diff_norm_g, w_br, w_o, ln1_g, ln1_b,
           w_router, b_router, w_gate, w_up, w_down, ln2_g, ln2_b):
    B, S, D = x.shape
    T = B * S
    depth = w_in.shape[0]
    alpha = (2 * depth) ** 0.25
    topk = min(DSA_TOPK_MAX, S // 4)
    bm = 256
    cc, ss = _rope_tables(positions)
    wr_t = w_router.T.astype(BF16)
    h32, h16 = _layer_norm(x.reshape(T, D), ln_in_g, ln_in_b)
    for l in range(depth):
        lam_init = 0.8 - 0.6 * math.exp(-0.3 * l)
        w_rope, w_rot, w_plain, w_vt, w_g, w_small = _prep_in_weights(w_in[l])
        qk = _project(h16, w_rope, "rope", BF16, tables=(cc, ss), wrot=w_rot)
        fqk = _project(h16, w_plain, "plain", BF16)
        vt = _project(h16, w_vt, "transposed", BF16)
        gates = _project(h16, w_g, "sigmoid", BF16)
        small = _project(h16, w_small, "f32", F32)

        o_diff = _attention("diff", qk, QK_DQ, qk, QK_DK, vt, VT_DV, B, S,
                            (diff_lam[l], diff_norm_g[l]), lam_init=lam_init)
        small_t = _project(h16, w_small.T, "transposed", F32)
        mask_t = _dsa_select(qk, small_t, B, S, topk)
        o_dsa = _attention("dsa", qk, QK_SQ, qk, QK_SK, vt, VT_SV, B, S, (mask_t,))
        bias_row = jnp.zeros((1, LANES), F32).at[0, SM_FF:SM_FF + FOX_HEADS].set(b_forget[l])
        c, crep = _forget_cumsum(small, bias_row, B, S)
        ct = jnp.transpose(c[:, SM_FF:SM_FF + FOX_HEADS].reshape(B, S, FOX_HEADS), (0, 2, 1))
        o_fox = _attention("fox", fqk, FQK_FQ, fqk, FQK_FK, vt, VT_FV, B, S, (ct, crep))

        h32, h16, aff_t = _merge(alpha, o_diff, o_dsa, o_fox, gates, h32, w_br[l].astype(BF16),
                                 w_o[l].astype(BF16), ln1_g[l], ln1_b[l], wr_t)
        e_t, w_t = _route(aff_t, b_router)
        blk_e, blk_cnt, tok_pad, slot_pad = _dispatch_tables(e_t, T, bm)
        y2 = _experts(h32, blk_e, blk_cnt, tok_pad, slot_pad, w_gate[l].astype(BF16), w_up[l].astype(BF16),
                      w_down[l].astype(BF16), bm)
        h32, h16 = _combine(alpha, h32, y2, w_t.T, ln2_g[l], ln2_b[l])
    return h32.reshape(B, S, D)
```

```python
import functools
import math

import jax
import jax.numpy as jnp
from jax import lax
from jax.experimental import pallas as pl
from jax.experimental.pallas import tpu as pltpu

F32 = jnp.float32
BF16 = jnp.bfloat16

CHUNK = 64
HEAD_DIM = 64
ROPE_THETA = 10000.0
LN_EPS = 1e-5
DIFF_HEADS = 4
DSA_HEADS = 8
IDX_HEADS = 4
DSA_TOPK_MAX = 256
FOX_HEADS = 8
N_BRANCH = 3
BRANCH_WIDTH = 512
N_EXPERTS = 32
N_GROUPS = 4
EXPERTS_PER_GROUP = N_EXPERTS // N_GROUPS
TOP_K = 2

LANES = 128
NEG = -1e30
INT_MIN = -2 ** 31
HALF = 1 << 15

QK_DQ, QK_DK, QK_SQ, QK_SK, QK_IQ, QK_IK = 0, 512, 1024, 1536, 2048, 2304
QK_WIDTH = 2560
FQK_FQ, FQK_FK = 0, 512
VT_DV, VT_SV, VT_FV = 0, 512, 1024
SM_IW, SM_FF = 0, 4
VMEM_LIMIT = 56 * 1024 * 1024


def _cparams(sem):
    return pltpu.CompilerParams(dimension_semantics=sem, vmem_limit_bytes=VMEM_LIMIT)


def _ln_rows(x, g, b):
    mu = jnp.mean(x, axis=-1, keepdims=True)
    xc = x - mu
    var = jnp.mean(xc * xc, axis=-1, keepdims=True)
    return xc * lax.rsqrt(var + LN_EPS) * g + b


def _ln_body(x_ref, g_ref, b_ref, o32_ref, o16_ref):
    y = _ln_rows(x_ref[...], g_ref[...], b_ref[...])
    o32_ref[...] = y
    o16_ref[...] = y.astype(BF16)


def _layer_norm(x, g, b, tm=512):
    T, D = x.shape
    return pl.pallas_call(
        _ln_body,
        out_shape=(jax.ShapeDtypeStruct((T, D), F32), jax.ShapeDtypeStruct((T, D), BF16)),
        grid=(T // tm,),
        in_specs=[pl.BlockSpec((tm, D), lambda i: (i, 0)),
                  pl.BlockSpec((1, D), lambda i: (0, 0)),
                  pl.BlockSpec((1, D), lambda i: (0, 0))],
        out_specs=(pl.BlockSpec((tm, D), lambda i: (i, 0)), pl.BlockSpec((tm, D), lambda i: (i, 0))),
        compiler_params=_cparams(("arbitrary",)),
        name="layer_norm",
    )(x, g.reshape(1, D), b.reshape(1, D))


def _proj_body(mode, width, h_ref, w_ref, *rest):
    if mode == "rope":
        wrot_ref, cc_ref, ss_ref, o_ref = rest
    else:
        (o_ref,) = rest
    h = h_ref[...]
    tn = 512 if width % 512 == 0 else width
    for j in range(width // tn):
        cs = slice(j * tn, (j + 1) * tn)
        if mode == "transposed":
            o_ref[cs, :] = lax.dot_general(w_ref[cs, :], h, (((1,), (1,)), ((), ())),
                                           preferred_element_type=F32).astype(o_ref.dtype)
            continue
        z = jnp.dot(h, w_ref[:, cs], preferred_element_type=F32)
        if mode == "rope":
            zr = jnp.dot(h, wrot_ref[:, cs], preferred_element_type=F32)
            cc = cc_ref[...]
            ss = ss_ref[...]
            for i in range(tn // LANES):
                ls = slice(i * LANES, (i + 1) * LANES)
                o_ref[:, j * tn + i * LANES:j * tn + (i + 1) * LANES] = (
                    z[:, ls] * cc + zr[:, ls] * ss).astype(o_ref.dtype)
        elif mode == "sigmoid":
            o_ref[:, cs] = (1.0 / (1.0 + jnp.exp(-z))).astype(o_ref.dtype)
        else:
            o_ref[:, cs] = z.astype(o_ref.dtype)


def _project(h16, w, mode, out_dtype, tables=None, wrot=None, tm=512):
    T, D = h16.shape
    transposed = mode == "transposed"
    width = w.shape[0] if transposed else w.shape[1]
    row = lambda i: (i, 0)
    fixed = lambda i: (0, 0)
    in_specs = [pl.BlockSpec((tm, D), row), pl.BlockSpec(w.shape, fixed)]
    args = [h16, w]
    if mode == "rope":
        cc, ss = tables
        in_specs += [pl.BlockSpec((D, width), fixed), pl.BlockSpec((tm, LANES), row),
                     pl.BlockSpec((tm, LANES), row)]
        args += [wrot, cc, ss]
    return pl.pallas_call(
        functools.partial(_proj_body, mode, width),
        out_shape=jax.ShapeDtypeStruct((width, T) if transposed else (T, width), out_dtype),
        grid=(T // tm,),
        in_specs=in_specs,
        out_specs=pl.BlockSpec((width, tm), lambda i: (0, i)) if transposed else pl.BlockSpec((tm, width), row),
        compiler_params=_cparams(("arbitrary",)),
        name="proj_" + mode,
    )(*args)


def _cumsum_body(x_ref, bias_ref, tri_ref, c_ref, crep_ref, carry_ref):
    @pl.when(pl.program_id(1) == 0)
    def _():
        carry_ref[...] = jnp.zeros_like(carry_ref)

    x = x_ref[...] + bias_ref[...]
    lf = jnp.minimum(x, 0.0) - jnp.log1p(jnp.exp(-jnp.abs(x)))
    hi = lf.astype(BF16)
    r1 = lf - hi.astype(F32)
    mid = r1.astype(BF16)
    lo = (r1 - mid.astype(F32)).astype(BF16)
    tri = tri_ref[...]
    c = (jnp.dot(tri, hi, preferred_element_type=F32) + jnp.dot(tri, mid, preferred_element_type=F32)
         + jnp.dot(tri, lo, preferred_element_type=F32)) + carry_ref[...]
    c_ref[...] = c
    carry_ref[...] = c[-1:, :]
    for h in range(FOX_HEADS):
        crep_ref[:, h * LANES:(h + 1) * LANES] = jnp.broadcast_to(c[:, SM_FF + h:SM_FF + h + 1], (c.shape[0], LANES))


def _forget_cumsum(small, bias_row, B, S, tb=512):
    T = small.shape[0]
    nb = S // tb
    tri = jnp.tril(jnp.ones((tb, tb), F32)).astype(BF16)
    return pl.pallas_call(
        _cumsum_body,
        out_shape=(jax.ShapeDtypeStruct((T, LANES), F32), jax.ShapeDtypeStruct((T, FOX_HEADS * LANES), F32)),
        grid=(B, nb),
        in_specs=[pl.BlockSpec((tb, LANES), lambda b, i: (b * nb + i, 0)),
                  pl.BlockSpec((1, LANES), lambda b, i: (0, 0)),
                  pl.BlockSpec((tb, tb), lambda b, i: (0, 0))],
        out_specs=(pl.BlockSpec((tb, LANES), lambda b, i: (b * nb + i, 0)),
                   pl.BlockSpec((tb, FOX_HEADS * LANES), lambda b, i: (b * nb + i, 0))),
        scratch_shapes=[pltpu.VMEM((1, LANES), F32)],
        compiler_params=_cparams(("arbitrary", "arbitrary")),
        name="forget_cumsum",
    )(small, bias_row, tri)


def _attn_body(mode, nslab, lam_init, qt_ref, kt_ref, q_ref, k_ref, vt_ref, *rest):
    if mode == "diff":
        lam_ref, g_ref, o_ref, m_sc, acc_sc = rest
    elif mode == "dsa":
        mask_ref, o_ref, m_sc, acc_sc = rest
    else:
        cq_ref, ck_ref, o_ref, m_sc, acc_sc = rest
    p = pl.program_id(1)
    qi = qt_ref[p]
    ki = kt_ref[p]
    tq = q_ref.shape[0]
    tk = k_ref.shape[0]

    @pl.when(ki == 0)
    def _():
        m_sc[...] = jnp.full_like(m_sc, NEG)
        acc_sc[...] = jnp.zeros_like(acc_sc)

    lane = lax.broadcasted_iota(jnp.int32, (1, LANES), 1)

    def tile_update(diag):
        if mode == "dsa":
            keep_all = mask_ref[...] != 0
        elif diag:
            krow = lax.broadcasted_iota(jnp.int32, (tk, tq), 0)
            qcol = lax.broadcasted_iota(jnp.int32, (tk, tq), 1)
            if mode == "diff":
                keep_all = (krow >> 6) <= (qcol >> 6)
            else:
                keep_all = krow <= qcol
        def scores(idx):
            ls = slice((idx // 2) * LANES, (idx // 2 + 1) * LANES)
            q = q_ref[:, ls]
            half = (lane < HEAD_DIM) if idx % 2 == 0 else (lane >= HEAD_DIM)
            qm = jnp.where(half, q, jnp.zeros_like(q))
            return lax.dot_general(k_ref[:, ls], qm, (((1,), (1,)), ((), ())), preferred_element_type=F32)

        ones_rows = jnp.ones((8, tk), BF16)
        s_next = scores(0)
        for idx in range(2 * nslab):
            s = s_next
            if idx + 1 < 2 * nslab:
                s_next = scores(idx + 1)
            vt = jnp.concatenate([vt_ref[(idx // 2) * LANES:(idx // 2 + 1) * LANES, :], ones_rows], axis=0)
            if mode == "fox":
                ck = ck_ref[:, idx * LANES:(idx + 1) * LANES]
                s = s - jnp.concatenate([ck] * (tq // LANES), axis=1)
            if mode == "dsa" or diag:
                s = jnp.where(keep_all, s, NEG)
            m_old = m_sc[idx:idx + 1, :]
            mx = jnp.max(s, axis=0, keepdims=True)
            if mode == "fox":
                cq = cq_ref[0, idx:idx + 1, :]
                m_new = jnp.maximum(m_old, mx + cq)
                shift = cq - m_new
            else:
                m_new = jnp.maximum(m_old, mx)
                shift = -m_new
            alpha = jnp.exp(m_old - m_new)
            pr = jnp.exp(s + shift).astype(BF16)
            acc_sc[idx] = alpha * acc_sc[idx] + jnp.dot(vt, pr, preferred_element_type=F32)
            m_sc[idx:idx + 1, :] = m_new

    if mode == "dsa":
        tile_update(False)
    else:
        @pl.when(ki == qi)
        def _():
            tile_update(True)

        @pl.when(ki != qi)
        def _():
            tile_update(False)

    @pl.when(ki == qi)
    def _():
        if mode == "diff":
            lm = lam_ref[...]
            lam = (jnp.exp(jnp.sum(lm[0:1] * lm[1:2], keepdims=True))
                   - jnp.exp(jnp.sum(lm[2:3] * lm[3:4], keepdims=True)) + lam_init)
        feat = lax.broadcasted_iota(jnp.int32, (LANES, 1), 0)
        for sl in range(nslab):
            o0 = acc_sc[2 * sl, 0:LANES, :] / acc_sc[2 * sl, LANES:LANES + 1, :]
            o1 = acc_sc[2 * sl + 1, 0:LANES, :] / acc_sc[2 * sl + 1, LANES:LANES + 1, :]
            if mode == "diff":
                o = (o0 - lam * o1).T
                o = o * lax.rsqrt(jnp.mean(o * o, axis=-1, keepdims=True) + LN_EPS)
                o = (o * g_ref[...]) * (1.0 - lam_init)
            else:
                o = jnp.where(feat < HEAD_DIM, o0, o1).T
            o_ref[:, sl * LANES:(sl + 1) * LANES] = o.astype(o_ref.dtype)


def _attention(mode, qarr, qcol, karr, kcol, vt, vrow, B, S, extra, lam_init=0.0, tile=512):
    nslab = 4
    width = nslab * LANES
    nq = S // tile
    pairs = [(qi, ki) for qi in range(nq) for ki in range(qi + 1)]
    qt = jnp.asarray([pq for pq, _ in pairs], jnp.int32)
    kt = jnp.asarray([pk for _, pk in pairs], jnp.int32)
    T = B * S
    qmap = lambda b, p, qt, kt: (b * nq + qt[p], qcol // width)
    kmap = lambda b, p, qt, kt: (b * nq + kt[p], kcol // width)
    vmap = lambda b, p, qt, kt: (vrow // width, b * nq + kt[p])
    in_specs = [pl.BlockSpec((tile, width), qmap), pl.BlockSpec((tile, width), kmap),
                pl.BlockSpec((width, tile), vmap)]
    args = [qarr, karr, vt]
    if mode == "diff":
        lam4, g = extra
        in_specs += [pl.BlockSpec((4, HEAD_DIM), lambda b, p, qt, kt: (0, 0)),
                     pl.BlockSpec((1, LANES), lambda b, p, qt, kt: (0, 0))]
        args += [lam4, g.reshape(1, LANES)]
    elif mode == "dsa":
        (mask_t,) = extra
        in_specs += [pl.BlockSpec((tile, tile), lambda b, p, qt, kt: (b * nq + kt[p], qt[p]))]
        args += [mask_t]
    else:
        ct, crep = extra
        in_specs += [pl.BlockSpec((1, FOX_HEADS, tile), lambda b, p, qt, kt: (b, 0, qt[p])),
                     pl.BlockSpec((tile, FOX_HEADS * LANES), lambda b, p, qt, kt: (b * nq + kt[p], 0))]
        args += [ct, crep]
    return pl.pallas_call(
        functools.partial(_attn_body, mode, nslab, lam_init),
        out_shape=jax.ShapeDtypeStruct((T, width), BF16),
        grid_spec=pltpu.PrefetchScalarGridSpec(
            num_scalar_prefetch=2,
            grid=(B, len(pairs)),
            in_specs=in_specs,
            out_specs=pl.BlockSpec((tile, width), lambda b, p, qt, kt: (b * nq + qt[p], 0)),
            scratch_shapes=[pltpu.VMEM((2 * nslab, tile), F32),
                            pltpu.VMEM((2 * nslab, LANES + 8, tile), F32)]),
        compiler_params=_cparams(("arbitrary", "arbitrary")),
        name="attn_" + mode,
    )(qt, kt, *args)


def _select_body(topk, iq_ref, ik_ref, iwt_ref, tri_ref, mask_ref, keys_ref, hi_ref, lo_ref):
    qi = pl.program_id(1)
    tq = iq_ref.shape[0]
    tk = tq
    nkb = qi + 1
    q0 = qi * tq
    iq = iq_ref[...]
    qhs = [iq[:, h * HEAD_DIM:(h + 1) * HEAD_DIM] for h in range(IDX_HEADS)]
    iwt = iwt_ref[...]
    q_chunk = (q0 + lax.broadcasted_iota(jnp.int32, (tk, tq), 1)) >> 6
    k_iota = lax.broadcasted_iota(jnp.int32, (tk, tq), 0)

    mask_ref[...] = jnp.zeros_like(mask_ref)

    def score_block(j, carry):
        k0 = pl.multiple_of(j * tk, tk)
        ik = ik_ref[pl.ds(k0, tk), 0:HEAD_DIM]
        score = jnp.zeros((tk, tq), F32)
        for h in range(IDX_HEADS):
            lg = lax.dot_general(ik, qhs[h], (((1,), (1,)), ((), ())), preferred_element_type=F32)
            score = score + iwt[SM_IW + h:SM_IW + h + 1, :] * jnp.maximum(lg, 0.0)
        score = jnp.where(score == 0.0, 0.0, score)
        bits = lax.bitcast_convert_type(score, jnp.int32)
        key = bits ^ ((bits >> 31) & jnp.int32(0x7FFFFFFF))
        vis = ((k0 + k_iota) >> 6) <= q_chunk
        key = jnp.where(vis, key, jnp.int32(INT_MIN))
        keys_ref[pl.ds(k0, tk), :] = key
        hi_ref[pl.ds(k0, tk), :] = (key >> 16).astype(jnp.int16)
        lo_ref[pl.ds(k0, tk), :] = ((key & jnp.int32(0xFFFF)) - HALF).astype(jnp.int16)
        return carry

    lax.fori_loop(0, nkb, score_block, 0)

    npair = (nkb + 1) >> 1

    @pl.when((nkb & 1) == 1)
    def _():
        k0 = pl.multiple_of(nkb * tk, tk)
        hi_ref[pl.ds(k0, tk), :] = jnp.full((tk, tq), -HALF, jnp.int16)
        lo_ref[pl.ds(k0, tk), :] = jnp.full((tk, tq), -HALF, jnp.int16)

    def count16(ref, pred):
        def body(j, acc):
            k0 = pl.multiple_of(j * (2 * tk), 2 * tk)
            hit = jnp.where(pred(ref[pl.ds(k0, 2 * tk), :]), jnp.int16(1), jnp.int16(0))
            for r in range(2 * tk // 16):
                acc = acc + hit[r * 16:(r + 1) * 16, :]
            return acc
        acc = lax.fori_loop(0, npair, body, jnp.zeros((16, tq), jnp.int16))
        return jnp.sum(acc.astype(jnp.int32), axis=0, keepdims=True)

    def search16(ref, base):
        def bit_step(i, lo):
            cand = lo + (jnp.int32(1) << (15 - i))
            c16 = cand.astype(jnp.int16)
            return jnp.where(base + count16(ref, lambda blk: blk >= c16) >= topk, cand, lo)
        return lax.fori_loop(0, 16, bit_step, jnp.full((1, tq), -HALF, jnp.int32))

    thr_hi = search16(hi_ref, 0)
    thr_hi16 = thr_hi.astype(jnp.int16)
    above = count16(hi_ref, lambda blk: blk > thr_hi16)

    def keep_low_of_threshold_rows(j, carry):
        k0 = pl.multiple_of(j * tk, tk)
        lo_ref[pl.ds(k0, tk), :] = jnp.where(hi_ref[pl.ds(k0, tk), :] == thr_hi16, lo_ref[pl.ds(k0, tk), :],
                                             jnp.int16(-HALF))
        return carry

    lax.fori_loop(0, nkb, keep_low_of_threshold_rows, 0)
    thr_lo = search16(lo_ref, above)
    thr_lo16 = thr_lo.astype(jnp.int16)
    above = above + count16(lo_ref, lambda blk: blk > thr_lo16)
    thr = (thr_hi << 16) + (thr_lo + HALF)
    thr = jnp.maximum(thr, jnp.int32(INT_MIN + 1))
    need = (topk - above).astype(F32)
    tri = tri_ref[...]

    def select_block(j, carry):
        k0 = pl.multiple_of(j * tk, tk)
        key = keys_ref[pl.ds(k0, tk), :]
        eq = key == thr
        incl = jnp.dot(tri, jnp.where(eq, 1.0, 0.0).astype(BF16), preferred_element_type=F32)
        sel = (key > thr) | (eq & ((carry + incl) <= need))
        mask_ref[pl.ds(k0, tk), :] = jnp.where(sel, 1, 0).astype(jnp.int8)
        return carry + incl[tk - 1:tk, :]

    lax.fori_loop(0, nkb, select_block, jnp.zeros((1, tq), F32))


def _dsa_select(qk, small_t, B, S, topk, tq=256):
    T = B * S
    nq = S // tq
    tri = jnp.tril(jnp.ones((tq, tq), F32)).astype(BF16)
    return pl.pallas_call(
        functools.partial(_select_body, topk),
        out_shape=jax.ShapeDtypeStruct((T, S), jnp.int8),
        grid=(B, nq),
        in_specs=[pl.BlockSpec((tq, 256), lambda b, i: (b * nq + i, QK_IQ // 256)),
                  pl.BlockSpec((S, 256), lambda b, i: (b, QK_IK // 256)),
                  pl.BlockSpec((8, tq), lambda b, i: (0, b * nq + i)),
                  pl.BlockSpec((tq, tq), lambda b, i: (0, 0))],
        out_specs=pl.BlockSpec((S, tq), lambda b, i: (b, i)),
        scratch_shapes=[pltpu.VMEM((S, tq), jnp.int32), pltpu.VMEM((S, tq), jnp.int16),
                        pltpu.VMEM((S, tq), jnp.int16)],
        compiler_params=_cparams(("arbitrary", "arbitrary")),
        name="dsa_select",
    )(qk, qk, small_t, tri)


def _merge_body(alpha, od_ref, os_ref, of_ref, gate_ref, h_ref, wbr_ref, wo_ref, g_ref, b_ref, wr_ref,
                h32_ref, h16_ref, aff_ref):
    D = h_ref.shape[1]
    y = None
    for i, o_ref in enumerate((od_ref, os_ref, of_ref)):
        br = jnp.dot(o_ref[...], wbr_ref[i], preferred_element_type=F32)
        t = gate_ref[:, i * D:(i + 1) * D].astype(F32) * br
        y = t if y is None else y + t
    m = jnp.dot(y.astype(BF16), wo_ref[...], preferred_element_type=F32)
    hn = _ln_rows(alpha * h_ref[...] + m, g_ref[...], b_ref[...])
    h32_ref[...] = hn
    h16 = hn.astype(BF16)
    h16_ref[...] = h16
    logits = lax.dot_general(wr_ref[...], h16, (((1,), (1,)), ((), ())), preferred_element_type=F32)
    aff_ref[...] = 1.0 / (1.0 + jnp.exp(-logits))


def _merge(alpha, od, os_, of, gates, h32, wbr, wo, g, b, wr_t, tm=512):
    T, D = h32.shape
    bw = od.shape[1]
    row = lambda i: (i, 0)
    fixed2 = lambda i: (0, 0)
    return pl.pallas_call(
        functools.partial(_merge_body, alpha),
        out_shape=(jax.ShapeDtypeStruct((T, D), F32), jax.ShapeDtypeStruct((T, D), BF16),
                   jax.ShapeDtypeStruct((N_EXPERTS, T), F32)),
        grid=(T // tm,),
        in_specs=[pl.BlockSpec((tm, bw), row), pl.BlockSpec((tm, bw), row), pl.BlockSpec((tm, bw), row),
                  pl.BlockSpec((tm, N_BRANCH * D), row), pl.BlockSpec((tm, D), row),
                  pl.BlockSpec((N_BRANCH, bw, D), lambda i: (0, 0, 0)), pl.BlockSpec((D, D), fixed2),
                  pl.BlockSpec((1, D), fixed2), pl.BlockSpec((1, D), fixed2),
                  pl.BlockSpec((N_EXPERTS, D), fixed2)],
        out_specs=(pl.BlockSpec((tm, D), row), pl.BlockSpec((tm, D), row),
                   pl.BlockSpec((N_EXPERTS, tm), lambda i: (0, i))),
        compiler_params=_cparams(("arbitrary",)),
        name="merge",
    )(od, os_, of, gates, h32, wbr, wo, g.reshape(1, D), b.reshape(1, D), wr_t)


def _route_body(aff_ref, bias_ref, e_ref, w_ref):
    aff = aff_ref[...]
    sel = aff + bias_ref[...]
    sub = lax.broadcasted_iota(jnp.int32, (EXPERTS_PER_GROUP, aff.shape[1]), 0)
    big = jnp.int32(EXPERTS_PER_GROUP)
    best = None
    for gidx in range(N_GROUPS):
        rs = slice(gidx * EXPERTS_PER_GROUP, (gidx + 1) * EXPERTS_PER_GROUP)
        s8 = sel[rs]
        a8 = aff[rs]
        m1 = jnp.max(s8, axis=0, keepdims=True)
        i1 = jnp.min(jnp.where(s8 == m1, sub, big), axis=0, keepdims=True)
        rest = jnp.where(sub == i1, -jnp.inf, s8)
        m2 = jnp.max(rest, axis=0, keepdims=True)
        i2 = jnp.min(jnp.where(rest == m2, sub, big), axis=0, keepdims=True)
        a1 = jnp.sum(jnp.where(sub == i1, a8, 0.0), axis=0, keepdims=True)
        a2 = jnp.sum(jnp.where(sub == i2, a8, 0.0), axis=0, keepdims=True)
        score = m1 + m2
        cand = (score, i1 + gidx * EXPERTS_PER_GROUP, i2 + gidx * EXPERTS_PER_GROUP, a1, a2)
        if best is None:
            best = cand
        else:
            take = score > best[0]
            best = tuple(jnp.where(take, c, o) for c, o in zip(cand, best))
    _, e1, e2, a1, a2 = best
    tot = a1 + a2
    e_ref[0:1, :] = e1
    e_ref[1:2, :] = e2
    w_ref[0:1, :] = a1 / tot
    w_ref[1:2, :] = a2 / tot


def _route(aff_t, b_router, tn=2048):
    E, T = aff_t.shape
    tn = min(tn, T)
    return pl.pallas_call(
        _route_body,
        out_shape=(jax.ShapeDtypeStruct((TOP_K, T), jnp.int32), jax.ShapeDtypeStruct((TOP_K, T), F32)),
        grid=(T // tn,),
        in_specs=[pl.BlockSpec((E, tn), lambda i: (0, i)), pl.BlockSpec((E, 1), lambda i: (0, 0))],
        out_specs=(pl.BlockSpec((TOP_K, tn), lambda i: (0, i)), pl.BlockSpec((TOP_K, tn), lambda i: (0, i))),
        compiler_params=_cparams(("arbitrary",)),
        name="route",
    )(aff_t, b_router.reshape(E, 1))


def _expert_body(be_ref, bc_ref, tok0_ref, tokn_ref, slot_ref, x_hbm, wg_ref, wu_ref, wd_ref, y_hbm,
                 xbuf, ybuf, gsem, ssem):
    i = pl.program_id(0)
    nb = pl.num_programs(0)
    cur = i & 1
    bm = xbuf.shape[1]
    n_real = y_hbm.shape[0] - 2 * bm

    def start_gather(tok_ref, slot):
        for r in range(bm):
            pltpu.make_async_copy(x_hbm.at[pl.ds(tok_ref[0, 0, r], 1), :], xbuf.at[slot, pl.ds(r, 1), :],
                                  gsem.at[slot]).start(priority=r % 2)

    def wait_gather(slot):
        pltpu.make_async_copy(x_hbm.at[pl.ds(0, bm), :], xbuf.at[slot], gsem.at[slot]).wait()

    def wait_scatter(slot):
        pltpu.make_async_copy(ybuf.at[slot], y_hbm.at[pl.ds(0, bm), :], ssem.at[slot]).wait()

    @pl.when(i == 0)
    def _():
        ybuf[...] = jnp.zeros_like(ybuf)
        for s in range(2):
            spare = pltpu.make_async_copy(ybuf.at[s], y_hbm.at[pl.ds(n_real + s * bm, bm), :], ssem.at[s])
            spare.start()
            spare.wait()
        start_gather(tok0_ref, 0)

    @pl.when(i + 1 < nb)
    def _():
        start_gather(tokn_ref, 1 - cur)

    wait_gather(cur)

    @pl.when(i >= 2)
    def _():
        wait_scatter(cur)

    @pl.when(bc_ref[i] > 0)
    def _():
        x = xbuf[cur].astype(BF16)
        gate = jnp.dot(x, wg_ref[0], preferred_element_type=F32)
        up = jnp.dot(x, wu_ref[0], preferred_element_type=F32)
        hmid = (gate * (1.0 / (1.0 + jnp.exp(-gate)))) * up
        ybuf[cur] = jnp.dot(hmid.astype(BF16), wd_ref[0], preferred_element_type=F32)

    for r in range(bm):
        pltpu.make_async_copy(ybuf.at[cur, pl.ds(r, 1), :], y_hbm.at[pl.ds(slot_ref[0, 0, r], 1), :],
                              ssem.at[cur]).start(priority=r % 2)

    @pl.when(i == nb - 1)
    def _():
        wait_scatter(cur)

        @pl.when(nb > 1)
        def _():
            wait_scatter(1 - cur)


def _experts(h32, blk_e, blk_cnt, tok_pad, slot_pad, wg, wu, wd, bm):
    T, D = h32.shape
    nb = blk_e.shape[0]
    de = wg.shape[2]
    smem_blk = lambda f: pl.BlockSpec((1, 1, bm), f, memory_space=pltpu.SMEM)
    return pl.pallas_call(
        _expert_body,
        out_shape=jax.ShapeDtypeStruct((TOP_K * T + 2 * bm, D), F32),
        grid_spec=pltpu.PrefetchScalarGridSpec(
            num_scalar_prefetch=2,
            grid=(nb,),
            in_specs=[smem_blk(lambda i, be, bc: (0, 0, 0)),
                      smem_blk(lambda i, be, bc: (jnp.minimum(i + 1, nb - 1), 0, 0)),
                      smem_blk(lambda i, be, bc: (i, 0, 0)),
                      pl.BlockSpec(memory_space=pl.ANY),
                      pl.BlockSpec((1, D, de), lambda i, be, bc: (be[i], 0, 0)),
                      pl.BlockSpec((1, D, de), lambda i, be, bc: (be[i], 0, 0)),
                      pl.BlockSpec((1, de, D), lambda i, be, bc: (be[i], 0, 0))],
            out_specs=pl.BlockSpec(memory_space=pl.ANY),
            scratch_shapes=[pltpu.VMEM((2, bm, D), F32), pltpu.VMEM((2, bm, D), F32),
                            pltpu.SemaphoreType.DMA((2,)), pltpu.SemaphoreType.DMA((2,))]),
        compiler_params=_cparams(("arbitrary",)),
        name="experts",
    )(blk_e, blk_cnt, tok_pad, tok_pad, slot_pad, h32, wg, wu, wd)


def _dispatch_tables(e_t, T, bm):
    A = TOP_K * T
    flat_e = e_t.T.reshape(A)
    order = jnp.argsort(flat_e, stable=True).astype(jnp.int32)
    counts = jnp.sum(flat_e[None, :] == jnp.arange(N_EXPERTS, dtype=jnp.int32)[:, None], axis=1).astype(jnp.int32)
    start = jnp.cumsum(counts) - counts
    nblk_e = (counts + bm - 1) // bm
    blk_end = jnp.cumsum(nblk_e)
    nb = A // bm + N_EXPERTS
    bidx = jnp.arange(nb, dtype=jnp.int32)
    blk_e = jnp.minimum(jnp.sum(bidx[:, None] >= blk_end[None, :], axis=1), N_EXPERTS - 1).astype(jnp.int32)
    off = (bidx - (blk_end - nblk_e)[blk_e]) * bm
    blk_cnt = jnp.where(bidx < blk_end[-1], jnp.clip(counts[blk_e] - off, 0, bm), 0).astype(jnp.int32)
    r = jnp.arange(bm, dtype=jnp.int32)[None, :]
    real = r < blk_cnt[:, None]
    a = order[jnp.clip((start[blk_e] + off)[:, None] + r, 0, A - 1)]
    tok_pad = jnp.where(real, a // TOP_K, 0)
    slot_pad = jnp.where(real, (a % TOP_K) * T + a // TOP_K, A + (bidx % 2)[:, None] * bm + r)
    return blk_e, blk_cnt, tok_pad.reshape(nb, 1, bm), slot_pad.reshape(nb, 1, bm)


def _combine_body(alpha, h_ref, y0_ref, y1_ref, w_ref, g_ref, b_ref, o32_ref, o16_ref):
    w = w_ref[...]
    f = w[:, 0:1] * y0_ref[...] + w[:, 1:2] * y1_ref[...]
    hn = _ln_rows(alpha * h_ref[...] + f, g_ref[...], b_ref[...])
    o32_ref[...] = hn
    o16_ref[...] = hn.astype(BF16)


def _combine(alpha, h32, y2, w_rows, g, b, tm=512):
    T, D = h32.shape
    nt = T // tm
    row = lambda i: (i, 0)
    fixed = lambda i: (0, 0)
    return pl.pallas_call(
        functools.partial(_combine_body, alpha),
        out_shape=(jax.ShapeDtypeStruct((T, D), F32), jax.ShapeDtypeStruct((T, D), BF16)),
        grid=(nt,),
        in_specs=[pl.BlockSpec((tm, D), row), pl.BlockSpec((tm, D), row),
                  pl.BlockSpec((tm, D), lambda i: (nt + i, 0)),
                  pl.BlockSpec((tm, TOP_K), row), pl.BlockSpec((1, D), fixed), pl.BlockSpec((1, D), fixed)],
        out_specs=(pl.BlockSpec((tm, D), row), pl.BlockSpec((tm, D), row)),
        compiler_params=_cparams(("arbitrary",)),
        name="combine",
    )(h32, y2, y2, w_rows, g.reshape(1, D), b.reshape(1, D))


def _rotate_half_columns(w):
    d = w.shape[0]
    w4 = w.reshape(d, -1, 2, HEAD_DIM // 2)
    return jnp.concatenate([-w4[:, :, 1:2], w4[:, :, 0:1]], axis=2).reshape(w.shape)


def _prep_in_weights(w_in_l):
    qk_w = DIFF_HEADS * 2 * HEAD_DIM
    sizes = (qk_w, qk_w, DIFF_HEADS * 2 * HEAD_DIM, 512, 512, 512, IDX_HEADS * HEAD_DIM, HEAD_DIM, IDX_HEADS,
             512, 512, 512, FOX_HEADS, N_BRANCH * w_in_l.shape[0])
    offs = [0]
    for s in sizes:
        offs.append(offs[-1] + s)
    dq, dk, dv, sq, sk, sv, iq, ik, iw, fq, fk, fv, ff, g = (w_in_l[:, offs[i]:offs[i + 1]] for i in range(len(sizes)))
    scale = HEAD_DIM ** -0.5
    idx_scale = (IDX_HEADS * HEAD_DIM) ** -0.5
    w_rope = jnp.concatenate([dq * scale, dk, sq * scale, sk, iq, ik, ik, ik, ik], axis=1)
    w_rot = _rotate_half_columns(w_rope)
    w_plain = jnp.concatenate([fq * scale, fk], axis=1)
    w_vt = jnp.concatenate([dv, sv, fv], axis=1).T
    pad = jnp.zeros((w_in_l.shape[0], LANES - IDX_HEADS - FOX_HEADS), w_in_l.dtype)
    w_small = jnp.concatenate([iw * idx_scale, ff, pad], axis=1)
    return (w_rope.astype(BF16), w_rot.astype(BF16), w_plain.astype(BF16), w_vt.astype(BF16), g.astype(BF16),
            w_small.astype(BF16))


def _rope_tables(positions):
    inv = ROPE_THETA ** (-jnp.arange(0, HEAD_DIM, 2, dtype=F32) / HEAD_DIM)
    ang = positions.astype(F32).reshape(-1, 1) * inv
    c, s = jnp.cos(ang), jnp.sin(ang)
    return jnp.tile(c, (1, LANES // (HEAD_DIM // 2))), jnp.tile(s, (1, LANES // (HEAD_DIM // 2)))


def kernel(x, positions, ln_in_g, ln_in_b, w_in, b_forget, diff_lam, diff_norm_g, w_br, w_o, ln1_g, ln1_b,
           w_router, b_router, w_gate, w_up, w_down, ln2_g, ln2_b):
    B, S, D = x.shape
    T = B * S
    depth = w_in.shape[0]
    alpha = (2 * depth) ** 0.25
    topk = min(DSA_TOPK_MAX, S // 4)
    bm = 256
    cc, ss = _rope_tables(positions)
    wr_t = w_router.T.astype(BF16)
    h32, h16 = _layer_norm(x.reshape(T, D), ln_in_g, ln_in_b)
    for l in range(depth):
        lam_init = 0.8 - 0.6 * math.exp(-0.3 * l)
        w_rope, w_rot, w_plain, w_vt, w_g, w_small = _prep_in_weights(w_in[l])
        qk = _project(h16, w_rope, "rope", BF16, tables=(cc, ss), wrot=w_rot)
        fqk = _project(h16, w_plain, "plain", BF16)
        vt = _project(h16, w_vt, "transposed", BF16)
        gates = _project(h16, w_g, "sigmoid", BF16)
        small = _project(h16, w_small, "f32", F32)

        o_diff = _attention("diff", qk, QK_DQ, qk, QK_DK, vt, VT_DV, B, S,
                            (diff_lam[l], diff_norm_g[l]), lam_init=lam_init)
        small_t = _project(h16, w_small.T, "transposed", F32)
        mask_t = _dsa_select(qk, small_t, B, S, topk)
        o_dsa = _attention("dsa", qk, QK_SQ, qk, QK_SK, vt, VT_SV, B, S, (mask_t,))
        bias_row = jnp.zeros((1, LANES), F32).at[0, SM_FF:SM_FF + FOX_HEADS].set(b_forget[l])
        c, crep = _forget_cumsum(small, bias_row, B, S)
        ct = jnp.transpose(c[:, SM_FF:SM_FF + FOX_HEADS].reshape(B, S, FOX_HEADS), (0, 2, 1))
        o_fox = _attention("fox", fqk, FQK_FQ, fqk, FQK_FK, vt, VT_FV, B, S, (ct, crep))

        h32, h16, aff_t = _merge(alpha, o_diff, o_dsa, o_fox, gates, h32, w_br[l].astype(BF16),
                                 w_o[l].astype(BF16), ln1_g[l], ln1_b[l], wr_t)
        e_t, w_t = _route(aff_t, b_router)
        blk_e, blk_cnt, tok_pad, slot_pad = _dispatch_tables(e_t, T, bm)
        y2 = _experts(h32, blk_e, blk_cnt, tok_pad, slot_pad, w_gate[l].astype(BF16), w_up[l].astype(BF16),
                      w_down[l].astype(BF16), bm)
        h32, h16 = _combine(alpha, h32, y2, w_t.T, ln2_g[l], ln2_b[l])
    return h32.reshape(B, S, D)
```

```python
import functools
import math

import jax
import jax.numpy as jnp
from jax import lax
from jax.experimental import pallas as pl
from jax.experimental.pallas import tpu as pltpu

F32 = jnp.float32
BF16 = jnp.bfloat16

CHUNK = 64
HEAD_DIM = 64
ROPE_THETA = 10000.0
LN_EPS = 1e-5
DIFF_HEADS = 4
DSA_HEADS = 8
IDX_HEADS = 4
DSA_TOPK_MAX = 256
FOX_HEADS = 8
N_BRANCH = 3
BRANCH_WIDTH = 512
N_EXPERTS = 32
N_GROUPS = 4
EXPERTS_PER_GROUP = N_EXPERTS // N_GROUPS
TOP_K = 2

LANES = 128
NEG = -1e30
INT_MIN = -2 ** 31
HALF = 1 << 15

QK_DQ, QK_DK, QK_SQ, QK_SK, QK_IQ, QK_IK = 0, 512, 1024, 1536, 2048, 2304
QK_WIDTH = 2560
FQK_FQ, FQK_FK = 0, 512
VT_DV, VT_SV, VT_FV = 0, 512, 1024
SM_IW, SM_FF = 0, 4
VMEM_LIMIT = 56 * 1024 * 1024


def _cparams(sem):
    return pltpu.CompilerParams(dimension_semantics=sem, vmem_limit_bytes=VMEM_LIMIT)


def _ln_rows(x, g, b):
    mu = jnp.mean(x, axis=-1, keepdims=True)
    xc = x - mu
    var = jnp.mean(xc * xc, axis=-1, keepdims=True)
    return xc * lax.rsqrt(var + LN_EPS) * g + b


def _ln_body(x_ref, g_ref, b_ref, o32_ref, o16_ref):
    y = _ln_rows(x_ref[...], g_ref[...], b_ref[...])
    o32_ref[...] = y
    o16_ref[...] = y.astype(BF16)


def _layer_norm(x, g, b, tm=512):
    T, D = x.shape
    return pl.pallas_call(
        _ln_body,
        out_shape=(jax.ShapeDtypeStruct((T, D), F32), jax.ShapeDtypeStruct((T, D), BF16)),
        grid=(T // tm,),
        in_specs=[pl.BlockSpec((tm, D), lambda i: (i, 0)),
                  pl.BlockSpec((1, D), lambda i: (0, 0)),
                  pl.BlockSpec((1, D), lambda i: (0, 0))],
        out_specs=(pl.BlockSpec((tm, D), lambda i: (i, 0)), pl.BlockSpec((tm, D), lambda i: (i, 0))),
        compiler_params=_cparams(("arbitrary",)),
        name="layer_norm",
    )(x, g.reshape(1, D), b.reshape(1, D))


def _proj_body(mode, width, h_ref, w_ref, *rest):
    if mode == "rope":
        wrot_ref, cc_ref, ss_ref, o_ref = rest
    else:
        (o_ref,) = rest
    h = h_ref[...]
    tn = 512 if width % 512 == 0 else width
    for j in range(width // tn):
        cs = slice(j * tn, (j + 1) * tn)
        if mode == "transposed":
            o_ref[cs, :] = lax.dot_general(w_ref[cs, :], h, (((1,), (1,)), ((), ())),
                                           preferred_element_type=F32).astype(o_ref.dtype)
            continue
        z = jnp.dot(h, w_ref[:, cs], preferred_element_type=F32)
        if mode == "rope":
            zr = jnp.dot(h, wrot_ref[:, cs], preferred_element_type=F32)
            cc = cc_ref[...]
            ss = ss_ref[...]
            for i in range(tn // LANES):
                ls = slice(i * LANES, (i + 1) * LANES)
                o_ref[:, j * tn + i * LANES:j * tn + (i + 1) * LANES] = (
                    z[:, ls] * cc + zr[:, ls] * ss).astype(o_ref.dtype)
        elif mode == "sigmoid":
            o_ref[:, cs] = (1.0 / (1.0 + jnp.exp(-z))).astype(o_ref.dtype)
        else:
            o_ref[:, cs] = z.astype(o_ref.dtype)


def _project(h16, w, mode, out_dtype, tables=None, wrot=None, tm=512):
    T, D = h16.shape
    transposed = mode == "transposed"
    width = w.shape[0] if transposed else w.shape[1]
    row = lambda i: (i, 0)
    fixed = lambda i: (0, 0)
    in_specs = [pl.BlockSpec((tm, D), row), pl.BlockSpec(w.shape, fixed)]
    args = [h16, w]
    if mode == "rope":
        cc, ss = tables
        in_specs += [pl.BlockSpec((D, width), fixed), pl.BlockSpec((tm, LANES), row),
                     pl.BlockSpec((tm, LANES), row)]
        args += [wrot, cc, ss]
    return pl.pallas_call(
        functools.partial(_proj_body, mode, width),
        out_shape=jax.ShapeDtypeStruct((width, T) if transposed else (T, width), out_dtype),
        grid=(T // tm,),
        in_specs=in_specs,
        out_specs=pl.BlockSpec((width, tm), lambda i: (0, i)) if transposed else pl.BlockSpec((tm, width), row),
        compiler_params=_cparams(("arbitrary",)),
        name="proj_" + mode,
    )(*args)


def _cumsum_body(x_ref, bias_ref, tri_ref, c_ref, crep_ref, carry_ref):
    @pl.when(pl.program_id(1) == 0)
    def _():
        carry_ref[...] = jnp.zeros_like(carry_ref)

    x = x_ref[...] + bias_ref[...]
    lf = jnp.minimum(x, 0.0) - jnp.log1p(jnp.exp(-jnp.abs(x)))
    hi = lf.astype(BF16)
    r1 = lf - hi.astype(F32)
    mid = r1.astype(BF16)
    lo = (r1 - mid.astype(F32)).astype(BF16)
    tri = tri_ref[...]
    c = (jnp.dot(tri, hi, preferred_element_type=F32) + jnp.dot(tri, mid, preferred_element_type=F32)
         + jnp.dot(tri, lo, preferred_element_type=F32)) + carry_ref[...]
    c_ref[...] = c
    carry_ref[...] = c[-1:, :]
    for h in range(FOX_HEADS):
        crep_ref[:, h * LANES:(h + 1) * LANES] = jnp.broadcast_to(c[:, SM_FF + h:SM_FF + h + 1], (c.shape[0], LANES))


def _forget_cumsum(small, bias_row, B, S, tb=512):
    T = small.shape[0]
    nb = S // tb
    tri = jnp.tril(jnp.ones((tb, tb), F32)).astype(BF16)
    return pl.pallas_call(
        _cumsum_body,
        out_shape=(jax.ShapeDtypeStruct((T, LANES), F32), jax.ShapeDtypeStruct((T, FOX_HEADS * LANES), F32)),
        grid=(B, nb),
        in_specs=[pl.BlockSpec((tb, LANES), lambda b, i: (b * nb + i, 0)),
                  pl.BlockSpec((1, LANES), lambda b, i: (0, 0)),
                  pl.BlockSpec((tb, tb), lambda b, i: (0, 0))],
        out_specs=(pl.BlockSpec((tb, LANES), lambda b, i: (b * nb + i, 0)),
                   pl.BlockSpec((tb, FOX_HEADS * LANES), lambda b, i: (b * nb + i, 0))),
        scratch_shapes=[pltpu.VMEM((1, LANES), F32)],
        compiler_params=_cparams(("arbitrary", "arbitrary")),
        name="forget_cumsum",
    )(small, bias_row, tri)


def _attn_body(mode, nslab, lam_init, fixed_ref, qt_ref, kt_ref, q_ref, k_ref, vt_ref, *rest):
    if fixed_ref:
        qn_ref, km_ref = rest[:2]
        rest = rest[2:]
    if mode == "diff":
        lam_ref, g_ref = rest[:2]
    elif mode == "dsa":
        (mask_ref,) = rest[:1]
    else:
        cq_ref, ck_ref = rest[:2]
    if fixed_ref:
        o_ref, den_ref, m_sc, acc_sc = rest[-4:]
    else:
        o_ref, m_sc, acc_sc = rest[-3:]
    p = pl.program_id(1)
    qi = qt_ref[p]
    ki = kt_ref[p]
    tq = q_ref.shape[0]
    tk = k_ref.shape[0]

    @pl.when(ki == 0)
    def _():
        if fixed_ref:
            km = km_ref[0]
            m_sc[...] = 1.01 * jnp.sqrt(qn_ref[...] * jnp.concatenate([km] * (tq // LANES), axis=1))
        else:
            m_sc[...] = jnp.full_like(m_sc, NEG)
        acc_sc[...] = jnp.zeros_like(acc_sc)

    lane = lax.broadcasted_iota(jnp.int32, (1, LANES), 1)

    def tile_update(diag):
        if mode == "dsa":
            keep_all = mask_ref[...] != 0
        elif diag:
            krow = lax.broadcasted_iota(jnp.int32, (tk, tq), 0)
            qcol = lax.broadcasted_iota(jnp.int32, (tk, tq), 1)
            if mode == "diff":
                keep_all = (krow >> 6) <= (qcol >> 6)
            else:
                keep_all = krow <= qcol
        def scores(idx):
            ls = slice((idx // 2) * LANES, (idx // 2 + 1) * LANES)
            q = q_ref[:, ls]
            half = (lane < HEAD_DIM) if idx % 2 == 0 else (lane >= HEAD_DIM)
            qm = jnp.where(half, q, jnp.zeros_like(q))
            return lax.dot_general(k_ref[:, ls], qm, (((1,), (1,)), ((), ())), preferred_element_type=F32)

        ones_rows = jnp.ones((8, tk), BF16)
        ahead = [scores(i) for i in range(LOOKAHEAD)]
        for idx in range(2 * nslab):
            s = ahead.pop(0)
            if idx + LOOKAHEAD < 2 * nslab:
                ahead.append(scores(idx + LOOKAHEAD))
            vt = jnp.concatenate([vt_ref[(idx // 2) * LANES:(idx // 2 + 1) * LANES, :], ones_rows], axis=0)
            if mode == "fox":
                ck = ck_ref[:, idx * LANES:(idx + 1) * LANES]
                s = s - jnp.concatenate([ck] * (tq // LANES), axis=1)
            if mode == "dsa" or diag:
                s = jnp.where(keep_all, s, NEG)
            m_old = m_sc[idx:idx + 1, :]
            if fixed_ref:
                shift = (cq_ref[0, idx:idx + 1, :] - m_old) if mode == "fox" else -m_old
                pr = jnp.exp(s + shift).astype(BF16)
                acc_sc[idx] = acc_sc[idx] + jnp.dot(vt, pr, preferred_element_type=F32)
                continue
            mx = jnp.max(s, axis=0, keepdims=True)
            if mode == "fox":
                cq = cq_ref[0, idx:idx + 1, :]
                m_new = jnp.maximum(m_old, mx + cq)
                shift = cq - m_new
            else:
                m_new = jnp.maximum(m_old, mx)
                shift = -m_new
            alpha = jnp.exp(m_old - m_new)
            pr = jnp.exp(s + shift).astype(BF16)
            acc_sc[idx] = alpha * acc_sc[idx] + jnp.dot(vt, pr, preferred_element_type=F32)
            m_sc[idx:idx + 1, :] = m_new

    if mode == "dsa":
        tile_update(False)
    else:
        @pl.when(ki == qi)
        def _():
            tile_update(True)

        @pl.when(ki != qi)
        def _():
            tile_update(False)

    @pl.when(ki == qi)
    def _():
        if mode == "diff":
            lm = lam_ref[...]
            lam = (jnp.exp(jnp.sum(lm[0:1] * lm[1:2], keepdims=True))
                   - jnp.exp(jnp.sum(lm[2:3] * lm[3:4], keepdims=True)) + lam_init)
        feat = lax.broadcasted_iota(jnp.int32, (LANES, 1), 0)
        if fixed_ref:
            for idx in range(2 * nslab):
                den_ref[idx:idx + 1, :] = acc_sc[idx, LANES:LANES + 1, :]
        for sl in range(nslab):
            o0 = acc_sc[2 * sl, 0:LANES, :] / acc_sc[2 * sl, LANES:LANES + 1, :]
            o1 = acc_sc[2 * sl + 1, 0:LANES, :] / acc_sc[2 * sl + 1, LANES:LANES + 1, :]
            if mode == "diff":
                o = (o0 - lam * o1).T
                o = o * lax.rsqrt(jnp.mean(o * o, axis=-1, keepdims=True) + LN_EPS)
                o = (o * g_ref[...]) * (1.0 - lam_init)
            else:
                o = jnp.where(feat < HEAD_DIM, o0, o1).T
            o_ref[:, sl * LANES:(sl + 1) * LANES] = o.astype(o_ref.dtype)


def _norms_body(nslab, q_ref, k_ref, qn_ref, km_ref):
    lane = lax.broadcasted_iota(jnp.int32, (1, LANES), 1)
    ones8 = jnp.ones((8, LANES), BF16)
    nt = (((1,), (1,)), ((), ()))
    kmax = []
    for idx in range(2 * nslab):
        ls = slice((idx // 2) * LANES, (idx // 2 + 1) * LANES)
        half = (lane < HEAD_DIM) if idx % 2 == 0 else (lane >= HEAD_DIM)
        q = q_ref[:, ls].astype(F32)
        k = k_ref[:, ls].astype(F32)
        qsq = jnp.where(half, q * q, 0.0).astype(BF16)
        ksq = jnp.where(half, k * k, 0.0).astype(BF16)
        qn_ref[idx:idx + 1, :] = lax.dot_general(ones8, qsq, nt, preferred_element_type=F32)[0:1]
        kn = lax.dot_general(ones8, ksq, nt, preferred_element_type=F32)[0:1]
        kmax.append(jnp.broadcast_to(jnp.max(kn, axis=1, keepdims=True), (1, LANES)))
    kcur = jnp.concatenate(kmax, axis=0)

    @pl.when(pl.program_id(1) == 0)
    def _():
        km_ref[0] = kcur

    @pl.when(pl.program_id(1) > 0)
    def _():
        km_ref[0] = jnp.maximum(km_ref[0], kcur)


def _attn_norms(qarr, qcol, karr, kcol, B, S, tile=512):
    nslab = 4
    width = nslab * LANES
    nq = S // tile
    return pl.pallas_call(
        functools.partial(_norms_body, nslab),
        out_shape=(jax.ShapeDtypeStruct((2 * nslab, B * S), F32), jax.ShapeDtypeStruct((B, 2 * nslab, LANES), F32)),
        grid=(B, nq),
        in_specs=[pl.BlockSpec((tile, width), lambda b, i: (b * nq + i, qcol // width)),
                  pl.BlockSpec((tile, width), lambda b, i: (b * nq + i, kcol // width))],
        out_specs=(pl.BlockSpec((2 * nslab, tile), lambda b, i: (0, b * nq + i)),
                   pl.BlockSpec((1, 2 * nslab, LANES), lambda b, i: (b, 0, 0))),
        compiler_params=_cparams(("arbitrary", "arbitrary")),
        name="attn_norms",
    )(qarr, karr)


LOOKAHEAD = 1
DEN_FLOOR = 1e-30


def _attention(mode, qarr, qcol, karr, kcol, vt, vrow, B, S, extra, lam_init=0.0, tile=512):
    qn, km = _attn_norms(qarr, qcol, karr, kcol, B, S, tile)
    o_fast, den = _attention_call(mode, qarr, qcol, karr, kcol, vt, vrow, B, S, extra, lam_init, tile, (qn, km))
    bad = jnp.logical_not(jnp.all(den > DEN_FLOOR))
    return lax.cond(
        bad,
        lambda: _attention_call(mode, qarr, qcol, karr, kcol, vt, vrow, B, S, extra, lam_init, tile, None),
        lambda: o_fast)


def _attention_call(mode, qarr, qcol, karr, kcol, vt, vrow, B, S, extra, lam_init, tile, norms):
    nslab = 4
    width = nslab * LANES
    nq = S // tile
    pairs = [(qi, ki) for qi in range(nq) for ki in range(qi + 1)]
    qt = jnp.asarray([pq for pq, _ in pairs], jnp.int32)
    kt = jnp.asarray([pk for _, pk in pairs], jnp.int32)
    T = B * S
    fixed_ref = norms is not None
    qmap = lambda b, p, qt, kt: (b * nq + qt[p], qcol // width)
    kmap = lambda b, p, qt, kt: (b * nq + kt[p], kcol // width)
    vmap = lambda b, p, qt, kt: (vrow // width, b * nq + kt[p])
    in_specs = [pl.BlockSpec((tile, width), qmap), pl.BlockSpec((tile, width), kmap),
                pl.BlockSpec((width, tile), vmap)]
    args = [qarr, karr, vt]
    if fixed_ref:
        in_specs += [pl.BlockSpec((2 * nslab, tile), lambda b, p, qt, kt: (0, b * nq + qt[p])),
                     pl.BlockSpec((1, 2 * nslab, LANES), lambda b, p, qt, kt: (b, 0, 0))]
        args += list(norms)
    if mode == "diff":
        lam4, g = extra
        in_specs += [pl.BlockSpec((4, HEAD_DIM), lambda b, p, qt, kt: (0, 0)),
                     pl.BlockSpec((1, LANES), lambda b, p, qt, kt: (0, 0))]
        args += [lam4, g.reshape(1, LANES)]
    elif mode == "dsa":
        (mask_t,) = extra
        in_specs += [pl.BlockSpec((tile, tile), lambda b, p, qt, kt: (b * nq + kt[p], qt[p]))]
        args += [mask_t]
    else:
        ct, crep = extra
        in_specs += [pl.BlockSpec((1, FOX_HEADS, tile), lambda b, p, qt, kt: (b, 0, qt[p])),
                     pl.BlockSpec((tile, FOX_HEADS * LANES), lambda b, p, qt, kt: (b * nq + kt[p], 0))]
        args += [ct, crep]
    out_shape = jax.ShapeDtypeStruct((T, width), BF16)
    out_specs = pl.BlockSpec((tile, width), lambda b, p, qt, kt: (b * nq + qt[p], 0))
    if fixed_ref:
        out_shape = (out_shape, jax.ShapeDtypeStruct((2 * nslab, T), F32))
        out_specs = (out_specs, pl.BlockSpec((2 * nslab, tile), lambda b, p, qt, kt: (0, b * nq + qt[p])))
    return pl.pallas_call(
        functools.partial(_attn_body, mode, nslab, lam_init, fixed_ref),
        out_shape=out_shape,
        grid_spec=pltpu.PrefetchScalarGridSpec(
            num_scalar_prefetch=2,
            grid=(B, len(pairs)),
            in_specs=in_specs,
            out_specs=out_specs,
            scratch_shapes=[pltpu.VMEM((2 * nslab, tile), F32),
                            pltpu.VMEM((2 * nslab, LANES + 8, tile), F32)]),
        compiler_params=_cparams(("arbitrary", "arbitrary")),
        name="attn_" + mode + ("_fixed" if fixed_ref else "_online"),
    )(qt, kt, *args)


def _select_body(topk, iq_ref, ik_ref, iwt_ref, tri_ref, mask_ref, keys_ref, hi_ref, lo_ref):
    qi = pl.program_id(1)
    tq = iq_ref.shape[0]
    tk = tq
    nkb = qi + 1
    q0 = qi * tq
    iq = iq_ref[...]
    qhs = [iq[:, h * HEAD_DIM:(h + 1) * HEAD_DIM] for h in range(IDX_HEADS)]
    iwt = iwt_ref[...]
    q_chunk = (q0 + lax.broadcasted_iota(jnp.int32, (tk, tq), 1)) >> 6
    k_iota = lax.broadcasted_iota(jnp.int32, (tk, tq), 0)

    mask_ref[...] = jnp.zeros_like(mask_ref)

    def score_block(j, carry):
        k0 = pl.multiple_of(j * tk, tk)
        ik = ik_ref[pl.ds(k0, tk), 0:HEAD_DIM]
        score = jnp.zeros((tk, tq), F32)
        for h in range(IDX_HEADS):
            lg = lax.dot_general(ik, qhs[h], (((1,), (1,)), ((), ())), preferred_element_type=F32)
            score = score + iwt[SM_IW + h:SM_IW + h + 1, :] * jnp.maximum(lg, 0.0)
        score = jnp.where(score == 0.0, 0.0, score)
        bits = lax.bitcast_convert_type(score, jnp.int32)
        key = bits ^ ((bits >> 31) & jnp.int32(0x7FFFFFFF))
        vis = ((k0 + k_iota) >> 6) <= q_chunk
        key = jnp.where(vis, key, jnp.int32(INT_MIN))
        keys_ref[pl.ds(k0, tk), :] = key
        hi_ref[pl.ds(k0, tk), :] = (key >> 16).astype(jnp.int16)
        lo_ref[pl.ds(k0, tk), :] = ((key & jnp.int32(0xFFFF)) - HALF).astype(jnp.int16)
        return carry

    lax.fori_loop(0, nkb, score_block, 0)

    npair = (nkb + 1) >> 1

    @pl.when((nkb & 1) == 1)
    def _():
        k0 = pl.multiple_of(nkb * tk, tk)
        hi_ref[pl.ds(k0, tk), :] = jnp.full((tk, tq), -HALF, jnp.int16)
        lo_ref[pl.ds(k0, tk), :] = jnp.full((tk, tq), -HALF, jnp.int16)

    def count16(ref, pred):
        def body(j, acc):
            k0 = pl.multiple_of(j * (2 * tk), 2 * tk)
            hit = jnp.where(pred(ref[pl.ds(k0, 2 * tk), :]), jnp.int16(1), jnp.int16(0))
            for r in range(2 * tk // 16):
                acc = acc + hit[r * 16:(r + 1) * 16, :]
            return acc
        acc = lax.fori_loop(0, npair, body, jnp.zeros((16, tq), jnp.int16))
        return jnp.sum(acc.astype(jnp.int32), axis=0, keepdims=True)

    def search16(ref, base):
        def bit_step(i, lo):
            cand = lo + (jnp.int32(1) << (15 - i))
            c16 = cand.astype(jnp.int16)
            return jnp.where(base + count16(ref, lambda blk: blk >= c16) >= topk, cand, lo)
        return lax.fori_loop(0, 16, bit_step, jnp.full((1, tq), -HALF, jnp.int32))

    thr_hi = search16(hi_ref, 0)
    thr_hi16 = thr_hi.astype(jnp.int16)
    above = count16(hi_ref, lambda blk: blk > thr_hi16)

    def keep_low_of_threshold_rows(j, carry):
        k0 = pl.multiple_of(j * tk, tk)
        lo_ref[pl.ds(k0, tk), :] = jnp.where(hi_ref[pl.ds(k0, tk), :] == thr_hi16, lo_ref[pl.ds(k0, tk), :],
                                             jnp.int16(-HALF))
        return carry

    lax.fori_loop(0, nkb, keep_low_of_threshold_rows, 0)
    thr_lo = search16(lo_ref, above)
    thr_lo16 = thr_lo.astype(jnp.int16)
    above = above + count16(lo_ref, lambda blk: blk > thr_lo16)
    thr = (thr_hi << 16) + (thr_lo + HALF)
    thr = jnp.maximum(thr, jnp.int32(INT_MIN + 1))
    need = (topk - above).astype(F32)
    tri = tri_ref[...]

    def select_block(j, carry):
        k0 = pl.multiple_of(j * tk, tk)
        key = keys_ref[pl.ds(k0, tk), :]
        eq = key == thr
        incl = jnp.dot(tri, jnp.where(eq, 1.0, 0.0).astype(BF16), preferred_element_type=F32)
        sel = (key > thr) | (eq & ((carry + incl) <= need))
        mask_ref[pl.ds(k0, tk), :] = jnp.where(sel, 1, 0).astype(jnp.int8)
        return carry + incl[tk - 1:tk, :]

    lax.fori_loop(0, nkb, select_block, jnp.zeros((1, tq), F32))


def _dsa_select(qk, small_t, B, S, topk, tq=256):
    T = B * S
    nq = S // tq
    tri = jnp.tril(jnp.ones((tq, tq), F32)).astype(BF16)
    return pl.pallas_call(
        functools.partial(_select_body, topk),
        out_shape=jax.ShapeDtypeStruct((T, S), jnp.int8),
        grid=(B, nq),
        in_specs=[pl.BlockSpec((tq, 256), lambda b, i: (b * nq + i, QK_IQ // 256)),
                  pl.BlockSpec((S, 256), lambda b, i: (b, QK_IK // 256)),
                  pl.BlockSpec((8, tq), lambda b, i: (0, b * nq + i)),
                  pl.BlockSpec((tq, tq), lambda b, i: (0, 0))],
        out_specs=pl.BlockSpec((S, tq), lambda b, i: (b, i)),
        scratch_shapes=[pltpu.VMEM((S, tq), jnp.int32), pltpu.VMEM((S, tq), jnp.int16),
                        pltpu.VMEM((S, tq), jnp.int16)],
        compiler_params=_cparams(("arbitrary", "arbitrary")),
        name="dsa_select",
    )(qk, qk, small_t, tri)


def _merge_body(alpha, od_ref, os_ref, of_ref, gate_ref, h_ref, wbr_ref, wo_ref, g_ref, b_ref, wr_ref,
                h32_ref, h16_ref, aff_ref):
    D = h_ref.shape[1]
    y = None
    for i, o_ref in enumerate((od_ref, os_ref, of_ref)):
        br = jnp.dot(o_ref[...], wbr_ref[i], preferred_element_type=F32)
        t = gate_ref[:, i * D:(i + 1) * D].astype(F32) * br
        y = t if y is None else y + t
    m = jnp.dot(y.astype(BF16), wo_ref[...], preferred_element_type=F32)
    hn = _ln_rows(alpha * h_ref[...] + m, g_ref[...], b_ref[...])
    h32_ref[...] = hn
    h16 = hn.astype(BF16)
    h16_ref[...] = h16
    logits = lax.dot_general(wr_ref[...], h16, (((1,), (1,)), ((), ())), preferred_element_type=F32)
    aff_ref[...] = 1.0 / (1.0 + jnp.exp(-logits))


def _merge(alpha, od, os_, of, gates, h32, wbr, wo, g, b, wr_t, tm=512):
    T, D = h32.shape
    bw = od.shape[1]
    row = lambda i: (i, 0)
    fixed2 = lambda i: (0, 0)
    return pl.pallas_call(
        functools.partial(_merge_body, alpha),
        out_shape=(jax.ShapeDtypeStruct((T, D), F32), jax.ShapeDtypeStruct((T, D), BF16),
                   jax.ShapeDtypeStruct((N_EXPERTS, T), F32)),
        grid=(T // tm,),
        in_specs=[pl.BlockSpec((tm, bw), row), pl.BlockSpec((tm, bw), row), pl.BlockSpec((tm, bw), row),
                  pl.BlockSpec((tm, N_BRANCH * D), row), pl.BlockSpec((tm, D), row),
                  pl.BlockSpec((N_BRANCH, bw, D), lambda i: (0, 0, 0)), pl.BlockSpec((D, D), fixed2),
                  pl.BlockSpec((1, D), fixed2), pl.BlockSpec((1, D), fixed2),
                  pl.BlockSpec((N_EXPERTS, D), fixed2)],
        out_specs=(pl.BlockSpec((tm, D), row), pl.BlockSpec((tm, D), row),
                   pl.BlockSpec((N_EXPERTS, tm), lambda i: (0, i))),
        compiler_params=_cparams(("arbitrary",)),
        name="merge",
    )(od, os_, of, gates, h32, wbr, wo, g.reshape(1, D), b.reshape(1, D), wr_t)


def _route_body(aff_ref, bias_ref, e_ref, w_ref):
    aff = aff_ref[...]
    sel = aff + bias_ref[...]
    sub = lax.broadcasted_iota(jnp.int32, (EXPERTS_PER_GROUP, aff.shape[1]), 0)
    big = jnp.int32(EXPERTS_PER_GROUP)
    best = None
    for gidx in range(N_GROUPS):
        rs = slice(gidx * EXPERTS_PER_GROUP, (gidx + 1) * EXPERTS_PER_GROUP)
        s8 = sel[rs]
        a8 = aff[rs]
        m1 = jnp.max(s8, axis=0, keepdims=True)
        i1 = jnp.min(jnp.where(s8 == m1, sub, big), axis=0, keepdims=True)
        rest = jnp.where(sub == i1, -jnp.inf, s8)
        m2 = jnp.max(rest, axis=0, keepdims=True)
        i2 = jnp.min(jnp.where(rest == m2, sub, big), axis=0, keepdims=True)
        a1 = jnp.sum(jnp.where(sub == i1, a8, 0.0), axis=0, keepdims=True)
        a2 = jnp.sum(jnp.where(sub == i2, a8, 0.0), axis=0, keepdims=True)
        score = m1 + m2
        cand = (score, i1 + gidx * EXPERTS_PER_GROUP, i2 + gidx * EXPERTS_PER_GROUP, a1, a2)
        if best is None:
            best = cand
        else:
            take = score > best[0]
            best = tuple(jnp.where(take, c, o) for c, o in zip(cand, best))
    _, e1, e2, a1, a2 = best
    tot = a1 + a2
    e_ref[0:1, :] = e1
    e_ref[1:2, :] = e2
    w_ref[0:1, :] = a1 / tot
    w_ref[1:2, :] = a2 / tot


def _route(aff_t, b_router, tn=2048):
    E, T = aff_t.shape
    tn = min(tn, T)
    return pl.pallas_call(
        _route_body,
        out_shape=(jax.ShapeDtypeStruct((TOP_K, T), jnp.int32), jax.ShapeDtypeStruct((TOP_K, T), F32)),
        grid=(T // tn,),
        in_specs=[pl.BlockSpec((E, tn), lambda i: (0, i)), pl.BlockSpec((E, 1), lambda i: (0, 0))],
        out_specs=(pl.BlockSpec((TOP_K, tn), lambda i: (0, i)), pl.BlockSpec((TOP_K, tn), lambda i: (0, i))),
        compiler_params=_cparams(("arbitrary",)),
        name="route",
    )(aff_t, b_router.reshape(E, 1))


def _expert_body(be_ref, bc_ref, tok0_ref, tokn_ref, slot_ref, x_hbm, wg_ref, wu_ref, wd_ref, y_hbm,
                 xbuf, ybuf, gsem, ssem):
    i = pl.program_id(0)
    nb = pl.num_programs(0)
    cur = i & 1
    bm = xbuf.shape[1]
    n_real = y_hbm.shape[0] - 2 * bm

    def start_gather(tok_ref, slot):
        for r in range(bm):
            pltpu.make_async_copy(x_hbm.at[pl.ds(tok_ref[0, 0, r], 1), :], xbuf.at[slot, pl.ds(r, 1), :],
                                  gsem.at[slot]).start(priority=r % 2)

    def wait_gather(slot):
        pltpu.make_async_copy(x_hbm.at[pl.ds(0, bm), :], xbuf.at[slot], gsem.at[slot]).wait()

    def wait_scatter(slot):
        pltpu.make_async_copy(ybuf.at[slot], y_hbm.at[pl.ds(0, bm), :], ssem.at[slot]).wait()

    @pl.when(i == 0)
    def _():
        ybuf[...] = jnp.zeros_like(ybuf)
        for s in range(2):
            spare = pltpu.make_async_copy(ybuf.at[s], y_hbm.at[pl.ds(n_real + s * bm, bm), :], ssem.at[s])
            spare.start()
            spare.wait()
        start_gather(tok0_ref, 0)

    @pl.when(i + 1 < nb)
    def _():
        start_gather(tokn_ref, 1 - cur)

    wait_gather(cur)

    @pl.when(i >= 2)
    def _():
        wait_scatter(cur)

    @pl.when(bc_ref[i] > 0)
    def _():
        x = xbuf[cur].astype(BF16)
        gate = jnp.dot(x, wg_ref[0], preferred_element_type=F32)
        up = jnp.dot(x, wu_ref[0], preferred_element_type=F32)
        hmid = (gate * (1.0 / (1.0 + jnp.exp(-gate)))) * up
        ybuf[cur] = jnp.dot(hmid.astype(BF16), wd_ref[0], preferred_element_type=F32)

    for r in range(bm):
        pltpu.make_async_copy(ybuf.at[cur, pl.ds(r, 1), :], y_hbm.at[pl.ds(slot_ref[0, 0, r], 1), :],
                              ssem.at[cur]).start(priority=r % 2)

    @pl.when(i == nb - 1)
    def _():
        wait_scatter(cur)

        @pl.when(nb > 1)
        def _():
            wait_scatter(1 - cur)


def _experts(h32, blk_e, blk_cnt, tok_pad, slot_pad, wg, wu, wd, bm):
    T, D = h32.shape
    nb = blk_e.shape[0]
    de = wg.shape[2]
    smem_blk = lambda f: pl.BlockSpec((1, 1, bm), f, memory_space=pltpu.SMEM)
    return pl.pallas_call(
        _expert_body,
        out_shape=jax.ShapeDtypeStruct((TOP_K * T + 2 * bm, D), F32),
        grid_spec=pltpu.PrefetchScalarGridSpec(
            num_scalar_prefetch=2,
            grid=(nb,),
            in_specs=[smem_blk(lambda i, be, bc: (0, 0, 0)),
                      smem_blk(lambda i, be, bc: (jnp.minimum(i + 1, nb - 1), 0, 0)),
                      smem_blk(lambda i, be, bc: (i, 0, 0)),
                      pl.BlockSpec(memory_space=pl.ANY),
                      pl.BlockSpec((1, D, de), lambda i, be, bc: (be[i], 0, 0)),
                      pl.BlockSpec((1, D, de), lambda i, be, bc: (be[i], 0, 0)),
                      pl.BlockSpec((1, de, D), lambda i, be, bc: (be[i], 0, 0))],
            out_specs=pl.BlockSpec(memory_space=pl.ANY),
            scratch_shapes=[pltpu.VMEM((2, bm, D), F32), pltpu.VMEM((2, bm, D), F32),
                            pltpu.SemaphoreType.DMA((2,)), pltpu.SemaphoreType.DMA((2,))]),
        compiler_params=_cparams(("arbitrary",)),
        name="experts",
    )(blk_e, blk_cnt, tok_pad, tok_pad, slot_pad, h32, wg, wu, wd)


def _dispatch_tables(e_t, T, bm):
    A = TOP_K * T
    flat_e = e_t.T.reshape(A)
    order = jnp.argsort(flat_e, stable=True).astype(jnp.int32)
    counts = jnp.sum(flat_e[None, :] == jnp.arange(N_EXPERTS, dtype=jnp.int32)[:, None], axis=1).astype(jnp.int32)
    start = jnp.cumsum(counts) - counts
    nblk_e = (counts + bm - 1) // bm
    blk_end = jnp.cumsum(nblk_e)
    nb = A // bm + N_EXPERTS
    bidx = jnp.arange(nb, dtype=jnp.int32)
    blk_e = jnp.minimum(jnp.sum(bidx[:, None] >= blk_end[None, :], axis=1), N_EXPERTS - 1).astype(jnp.int32)
    off = (bidx - (blk_end - nblk_e)[blk_e]) * bm
    blk_cnt = jnp.where(bidx < blk_end[-1], jnp.clip(counts[blk_e] - off, 0, bm), 0).astype(jnp.int32)
    r = jnp.arange(bm, dtype=jnp.int32)[None, :]
    real = r < blk_cnt[:, None]
    a = order[jnp.clip((start[blk_e] + off)[:, None] + r, 0, A - 1)]
    tok_pad = jnp.where(real, a // TOP_K, 0)
    slot_pad = jnp.where(real, (a % TOP_K) * T + a // TOP_K, A + (bidx % 2)[:, None] * bm + r)
    return blk_e, blk_cnt, tok_pad.reshape(nb, 1, bm), slot_pad.reshape(nb, 1, bm)


def _combine_body(alpha, h_ref, y0_ref, y1_ref, w_ref, g_ref, b_ref, o32_ref, o16_ref):
    w = w_ref[...]
    f = w[:, 0:1] * y0_ref[...] + w[:, 1:2] * y1_ref[...]
    hn = _ln_rows(alpha * h_ref[...] + f, g_ref[...], b_ref[...])
    o32_ref[...] = hn
    o16_ref[...] = hn.astype(BF16)


def _combine(alpha, h32, y2, w_rows, g, b, tm=512):
    T, D = h32.shape
    nt = T // tm
    row = lambda i: (i, 0)
    fixed = lambda i: (0, 0)
    return pl.pallas_call(
        functools.partial(_combine_body, alpha),
        out_shape=(jax.ShapeDtypeStruct((T, D), F32), jax.ShapeDtypeStruct((T, D), BF16)),
        grid=(nt,),
        in_specs=[pl.BlockSpec((tm, D), row), pl.BlockSpec((tm, D), row),
                  pl.BlockSpec((tm, D), lambda i: (nt + i, 0)),
                  pl.BlockSpec((tm, TOP_K), row), pl.BlockSpec((1, D), fixed), pl.BlockSpec((1, D), fixed)],
        out_specs=(pl.BlockSpec((tm, D), row), pl.BlockSpec((tm, D), row)),
        compiler_params=_cparams(("arbitrary",)),
        name="combine",
    )(h32, y2, y2, w_rows, g.reshape(1, D), b.reshape(1, D))


def _rotate_half_columns(w):
    d = w.shape[0]
    w4 = w.reshape(d, -1, 2, HEAD_DIM // 2)
    return jnp.concatenate([-w4[:, :, 1:2], w4[:, :, 0:1]], axis=2).reshape(w.shape)


def _prep_in_weights(w_in_l):
    qk_w = DIFF_HEADS * 2 * HEAD_DIM
    sizes = (qk_w, qk_w, DIFF_HEADS * 2 * HEAD_DIM, 512, 512, 512, IDX_HEADS * HEAD_DIM, HEAD_DIM, IDX_HEADS,
             512, 512, 512, FOX_HEADS, N_BRANCH * w_in_l.shape[0])
    offs = [0]
    for s in sizes:
        offs.append(offs[-1] + s)
    dq, dk, dv, sq, sk, sv, iq, ik, iw, fq, fk, fv, ff, g = (w_in_l[:, offs[i]:offs[i + 1]] for i in range(len(sizes)))
    scale = HEAD_DIM ** -0.5
    idx_scale = (IDX_HEADS * HEAD_DIM) ** -0.5
    w_rope = jnp.concatenate([dq * scale, dk, sq * scale, sk, iq, ik, ik, ik, ik], axis=1)
    w_rot = _rotate_half_columns(w_rope)
    w_plain = jnp.concatenate([fq * scale, fk], axis=1)
    w_vt = jnp.concatenate([dv, sv, fv], axis=1).T
    pad = jnp.zeros((w_in_l.shape[0], LANES - IDX_HEADS - FOX_HEADS), w_in_l.dtype)
    w_small = jnp.concatenate([iw * idx_scale, ff, pad], axis=1)
    return (w_rope.astype(BF16), w_rot.astype(BF16), w_plain.astype(BF16), w_vt.astype(BF16), g.astype(BF16),
            w_small.astype(BF16))


def _rope_tables(positions):
    inv = ROPE_THETA ** (-jnp.arange(0, HEAD_DIM, 2, dtype=F32) / HEAD_DIM)
    ang = positions.astype(F32).reshape(-1, 1) * inv
    c, s = jnp.cos(ang), jnp.sin(ang)
    return jnp.tile(c, (1, LANES // (HEAD_DIM // 2))), jnp.tile(s, (1, LANES // (HEAD_DIM // 2)))


def kernel(x, positions, ln_in_g, ln_in_b, w_in, b_forget, diff_lam, diff_norm_g, w_br, w_o, ln1_g, ln1_b,
           w_router, b_router, w_gate, w_up, w_down, ln2_g, ln2_b):
    B, S, D = x.shape
    T = B * S
    depth = w_in.shape[0]
    alpha = (2 * depth) ** 0.25
    topk = min(DSA_TOPK_MAX, S // 4)
    bm = 256
    cc, ss = _rope_tables(positions)
    wr_t = w_router.T.astype(BF16)
    h32, h16 = _layer_norm(x.reshape(T, D), ln_in_g, ln_in_b)
    for l in range(depth):
        lam_init = 0.8 - 0.6 * math.exp(-0.3 * l)
        w_rope, w_rot, w_plain, w_vt, w_g, w_small = _prep_in_weights(w_in[l])
        qk = _project(h16, w_rope, "rope", BF16, tables=(cc, ss), wrot=w_rot)
        fqk = _project(h16, w_plain, "plain", BF16)
        vt = _project(h16, w_vt, "transposed", BF16)
        gates = _project(h16, w_g, "sigmoid", BF16)
        small = _project(h16, w_small, "f32", F32)

        o_diff = _attention("diff", qk, QK_DQ, qk, QK_DK, vt, VT_DV, B, S,
                            (diff_lam[l], diff_norm_g[l]), lam_init=lam_init)
        small_t = _project(h16, w_small.T, "transposed", F32)
        mask_t = _dsa_select(qk, small_t, B, S, topk)
        o_dsa = _attention("dsa", qk, QK_SQ, qk, QK_SK, vt, VT_SV, B, S, (mask_t,))
        bias_row = jnp.zeros((1, LANES), F32).at[0, SM_FF:SM_FF + FOX_HEADS].set(b_forget[l])
        c, crep = _forget_cumsum(small, bias_row, B, S)
        ct = jnp.transpose(c[:, SM_FF:SM_FF + FOX_HEADS].reshape(B, S, FOX_HEADS), (0, 2, 1))
        o_fox = _attention("fox", fqk, FQK_FQ, fqk, FQK_FK, vt, VT_FV, B, S, (ct, crep))

        h32, h16, aff_t = _merge(alpha, o_diff, o_dsa, o_fox, gates, h32, w_br[l].astype(BF16),
                                 w_o[l].astype(BF16), ln1_g[l], ln1_b[l], wr_t)
        e_t, w_t = _route(aff_t, b_router)
        blk_e, blk_cnt, tok_pad, slot_pad = _dispatch_tables(e_t, T, bm)
        y2 = _experts(h32, blk_e, blk_cnt, tok_pad, slot_pad, w_gate[l].astype(BF16), w_up[l].astype(BF16),
                      w_down[l].astype(BF16), bm)
        h32, h16 = _combine(alpha, h32, y2, w_t.T, ln2_g[l], ln2_b[l])
    return h32.reshape(B, S, D)
```

```python
import functools
import math

import jax
import jax.numpy as jnp
from jax import lax
from jax.experimental import pallas as pl
from jax.experimental.pallas import tpu as pltpu

F32 = jnp.float32
BF16 = jnp.bfloat16

CHUNK = 64
HEAD_DIM = 64
ROPE_THETA = 10000.0
LN_EPS = 1e-5
DIFF_HEADS = 4
DSA_HEADS = 8
IDX_HEADS = 4
DSA_TOPK_MAX = 256
FOX_HEADS = 8
N_BRANCH = 3
BRANCH_WIDTH = 512
N_EXPERTS = 32
N_GROUPS = 4
EXPERTS_PER_GROUP = N_EXPERTS // N_GROUPS
TOP_K = 2

LANES = 128
NEG = -1e30
INT_MIN = -2 ** 31
HALF = 1 << 15

QK_DQ, QK_DK, QK_SQ, QK_SK, QK_IQ, QK_IK = 0, 512, 1024, 1536, 2048, 2304
QK_WIDTH = 2560
FQK_FQ, FQK_FK = 0, 512
VT_DV, VT_SV, VT_FV = 0, 512, 1024
SM_IW, SM_FF = 0, 4
VMEM_LIMIT = 56 * 1024 * 1024


def _cparams(sem):
    return pltpu.CompilerParams(dimension_semantics=sem, vmem_limit_bytes=VMEM_LIMIT)


def _ln_rows(x, g, b):
    mu = jnp.mean(x, axis=-1, keepdims=True)
    xc = x - mu
    var = jnp.mean(xc * xc, axis=-1, keepdims=True)
    return xc * lax.rsqrt(var + LN_EPS) * g + b


def _ln_body(x_ref, g_ref, b_ref, o32_ref, o16_ref):
    y = _ln_rows(x_ref[...], g_ref[...], b_ref[...])
    o32_ref[...] = y
    o16_ref[...] = y.astype(BF16)


def _layer_norm(x, g, b, tm=512):
    T, D = x.shape
    return pl.pallas_call(
        _ln_body,
        out_shape=(jax.ShapeDtypeStruct((T, D), F32), jax.ShapeDtypeStruct((T, D), BF16)),
        grid=(T // tm,),
        in_specs=[pl.BlockSpec((tm, D), lambda i: (i, 0)),
                  pl.BlockSpec((1, D), lambda i: (0, 0)),
                  pl.BlockSpec((1, D), lambda i: (0, 0))],
        out_specs=(pl.BlockSpec((tm, D), lambda i: (i, 0)), pl.BlockSpec((tm, D), lambda i: (i, 0))),
        compiler_params=_cparams(("arbitrary",)),
        name="layer_norm",
    )(x, g.reshape(1, D), b.reshape(1, D))


def _proj_body(mode, width, h_ref, w_ref, *rest):
    if mode == "rope":
        wrot_ref, cc_ref, ss_ref, o_ref = rest
    else:
        (o_ref,) = rest
    h = h_ref[...]
    tn = 512 if width % 512 == 0 else width
    for j in range(width // tn):
        cs = slice(j * tn, (j + 1) * tn)
        if mode == "transposed":
            o_ref[cs, :] = lax.dot_general(w_ref[cs, :], h, (((1,), (1,)), ((), ())),
                                           preferred_element_type=F32).astype(o_ref.dtype)
            continue
        z = jnp.dot(h, w_ref[:, cs], preferred_element_type=F32)
        if mode == "rope":
            zr = jnp.dot(h, wrot_ref[:, cs], preferred_element_type=F32)
            cc = cc_ref[...]
            ss = ss_ref[...]
            for i in range(tn // LANES):
                ls = slice(i * LANES, (i + 1) * LANES)
                o_ref[:, j * tn + i * LANES:j * tn + (i + 1) * LANES] = (
                    z[:, ls] * cc + zr[:, ls] * ss).astype(o_ref.dtype)
        elif mode == "sigmoid":
            o_ref[:, cs] = (1.0 / (1.0 + jnp.exp(-z))).astype(o_ref.dtype)
        else:
            o_ref[:, cs] = z.astype(o_ref.dtype)


def _project(h16, w, mode, out_dtype, tables=None, wrot=None, tm=512):
    T, D = h16.shape
    transposed = mode == "transposed"
    width = w.shape[0] if transposed else w.shape[1]
    row = lambda i: (i, 0)
    fixed = lambda i: (0, 0)
    in_specs = [pl.BlockSpec((tm, D), row), pl.BlockSpec(w.shape, fixed)]
    args = [h16, w]
    if mode == "rope":
        cc, ss = tables
        in_specs += [pl.BlockSpec((D, width), fixed), pl.BlockSpec((tm, LANES), row),
                     pl.BlockSpec((tm, LANES), row)]
        args += [wrot, cc, ss]
    return pl.pallas_call(
        functools.partial(_proj_body, mode, width),
        out_shape=jax.ShapeDtypeStruct((width, T) if transposed else (T, width), out_dtype),
        grid=(T // tm,),
        in_specs=in_specs,
        out_specs=pl.BlockSpec((width, tm), lambda i: (0, i)) if transposed else pl.BlockSpec((tm, width), row),
        compiler_params=_cparams(("arbitrary",)),
        name="proj_" + mode,
    )(*args)


def _cumsum_body(x_ref, bias_ref, tri_ref, c_ref, crep_ref, carry_ref):
    @pl.when(pl.program_id(1) == 0)
    def _():
        carry_ref[...] = jnp.zeros_like(carry_ref)

    x = x_ref[...] + bias_ref[...]
    lf = jnp.minimum(x, 0.0) - jnp.log1p(jnp.exp(-jnp.abs(x)))
    hi = lf.astype(BF16)
    r1 = lf - hi.astype(F32)
    mid = r1.astype(BF16)
    lo = (r1 - mid.astype(F32)).astype(BF16)
    tri = tri_ref[...]
    c = (jnp.dot(tri, hi, preferred_element_type=F32) + jnp.dot(tri, mid, preferred_element_type=F32)
         + jnp.dot(tri, lo, preferred_element_type=F32)) + carry_ref[...]
    c_ref[...] = c
    carry_ref[...] = c[-1:, :]
    for h in range(FOX_HEADS):
        crep_ref[:, h * LANES:(h + 1) * LANES] = jnp.broadcast_to(c[:, SM_FF + h:SM_FF + h + 1], (c.shape[0], LANES))


def _forget_cumsum(small, bias_row, B, S, tb=512):
    T = small.shape[0]
    nb = S // tb
    tri = jnp.tril(jnp.ones((tb, tb), F32)).astype(BF16)
    return pl.pallas_call(
        _cumsum_body,
        out_shape=(jax.ShapeDtypeStruct((T, LANES), F32), jax.ShapeDtypeStruct((T, FOX_HEADS * LANES), F32)),
        grid=(B, nb),
        in_specs=[pl.BlockSpec((tb, LANES), lambda b, i: (b * nb + i, 0)),
                  pl.BlockSpec((1, LANES), lambda b, i: (0, 0)),
                  pl.BlockSpec((tb, tb), lambda b, i: (0, 0))],
        out_specs=(pl.BlockSpec((tb, LANES), lambda b, i: (b * nb + i, 0)),
                   pl.BlockSpec((tb, FOX_HEADS * LANES), lambda b, i: (b * nb + i, 0))),
        scratch_shapes=[pltpu.VMEM((1, LANES), F32)],
        compiler_params=_cparams(("arbitrary", "arbitrary")),
        name="forget_cumsum",
    )(small, bias_row, tri)


def _attn_body(mode, nslab, lam_init, fixed_ref, qt_ref, kt_ref, q_ref, k_ref, vt_ref, *rest):
    if fixed_ref:
        qn_ref, km_ref = rest[:2]
        rest = rest[2:]
    if mode == "diff":
        lam_ref, g_ref = rest[:2]
    elif mode == "dsa":
        (mask_ref,) = rest[:1]
    else:
        cq_ref, ck_ref = rest[:2]
    if fixed_ref:
        o_ref, den_ref, m_sc, acc_sc = rest[-4:]
    else:
        o_ref, m_sc, acc_sc = rest[-3:]
    p = pl.program_id(1)
    qi = qt_ref[p]
    ki = kt_ref[p]
    tq = q_ref.shape[0]
    tk = k_ref.shape[0]

    @pl.when(ki == 0)
    def _():
        if fixed_ref:
            km = km_ref[0]
            m_sc[...] = 1.01 * jnp.sqrt(qn_ref[...] * jnp.concatenate([km] * (tq // LANES), axis=1))
        else:
            m_sc[...] = jnp.full_like(m_sc, NEG)
        acc_sc[...] = jnp.zeros_like(acc_sc)

    lane = lax.broadcasted_iota(jnp.int32, (1, LANES), 1)

    def tile_update(diag):
        if mode == "dsa":
            keep_all = mask_ref[...] != 0
        elif diag:
            krow = lax.broadcasted_iota(jnp.int32, (tk, tq), 0)
            qcol = lax.broadcasted_iota(jnp.int32, (tk, tq), 1)
            if mode == "diff":
                keep_all = (krow >> 6) <= (qcol >> 6)
            else:
                keep_all = krow <= qcol
        def scores(idx):
            ls = slice((idx // 2) * LANES, (idx // 2 + 1) * LANES)
            q = q_ref[:, ls]
            half = (lane < HEAD_DIM) if idx % 2 == 0 else (lane >= HEAD_DIM)
            qm = jnp.where(half, q, jnp.zeros_like(q))
            return lax.dot_general(k_ref[:, ls], qm, (((1,), (1,)), ((), ())), preferred_element_type=F32)

        ones_rows = jnp.ones((8, tk), BF16)
        ahead = [scores(i) for i in range(LOOKAHEAD)]
        for idx in range(2 * nslab):
            s = ahead.pop(0)
            if idx + LOOKAHEAD < 2 * nslab:
                ahead.append(scores(idx + LOOKAHEAD))
            vt = jnp.concatenate([vt_ref[(idx // 2) * LANES:(idx // 2 + 1) * LANES, :], ones_rows], axis=0)
            if mode == "fox":
                ck = ck_ref[:, idx * LANES:(idx + 1) * LANES]
                s = s - jnp.concatenate([ck] * (tq // LANES), axis=1)
            m_old = m_sc[idx:idx + 1, :]
            if fixed_ref:
                ref = (m_old - cq_ref[0, idx:idx + 1, :]) if mode == "fox" else m_old
                pr = jnp.exp(s - ref).astype(BF16)
                if mode == "dsa" or diag:
                    pr = jnp.where(keep_all, pr, jnp.zeros_like(pr))
                acc_sc[idx] = acc_sc[idx] + jnp.dot(vt, pr, preferred_element_type=F32)
                continue
            if mode == "dsa" or diag:
                s = jnp.where(keep_all, s, NEG)
            mx = jnp.max(s, axis=0, keepdims=True)
            if mode == "fox":
                cq = cq_ref[0, idx:idx + 1, :]
                m_new = jnp.maximum(m_old, mx + cq)
                shift = cq - m_new
            else:
                m_new = jnp.maximum(m_old, mx)
                shift = -m_new
            alpha = jnp.exp(m_old - m_new)
            pr = jnp.exp(s + shift).astype(BF16)
            acc_sc[idx] = alpha * acc_sc[idx] + jnp.dot(vt, pr, preferred_element_type=F32)
            m_sc[idx:idx + 1, :] = m_new

    if mode == "dsa":
        tile_update(False)
    else:
        @pl.when(ki == qi)
        def _():
            tile_update(True)

        @pl.when(ki != qi)
        def _():
            tile_update(False)

    @pl.when(ki == qi)
    def _():
        if mode == "diff":
            lm = lam_ref[...]
            lam = (jnp.exp(jnp.sum(lm[0:1] * lm[1:2], keepdims=True))
                   - jnp.exp(jnp.sum(lm[2:3] * lm[3:4], keepdims=True)) + lam_init)
        feat = lax.broadcasted_iota(jnp.int32, (LANES, 1), 0)
        if fixed_ref:
            for idx in range(2 * nslab):
                den_ref[idx:idx + 1, :] = acc_sc[idx, LANES:LANES + 1, :]
        for sl in range(nslab):
            o0 = acc_sc[2 * sl, 0:LANES, :] / acc_sc[2 * sl, LANES:LANES + 1, :]
            o1 = acc_sc[2 * sl + 1, 0:LANES, :] / acc_sc[2 * sl + 1, LANES:LANES + 1, :]
            if mode == "diff":
                o = (o0 - lam * o1).T
                o = o * lax.rsqrt(jnp.mean(o * o, axis=-1, keepdims=True) + LN_EPS)
                o = (o * g_ref[...]) * (1.0 - lam_init)
            else:
                o = jnp.where(feat < HEAD_DIM, o0, o1).T
            o_ref[:, sl * LANES:(sl + 1) * LANES] = o.astype(o_ref.dtype)


def _norms_body(nslab, q_ref, k_ref, qn_ref, km_ref):
    lane = lax.broadcasted_iota(jnp.int32, (1, LANES), 1)
    ones8 = jnp.ones((8, LANES), BF16)
    nt = (((1,), (1,)), ((), ()))
    kmax = []
    for idx in range(2 * nslab):
        ls = slice((idx // 2) * LANES, (idx // 2 + 1) * LANES)
        half = (lane < HEAD_DIM) if idx % 2 == 0 else (lane >= HEAD_DIM)
        q = q_ref[:, ls].astype(F32)
        k = k_ref[:, ls].astype(F32)
        qsq = jnp.where(half, q * q, 0.0).astype(BF16)
        ksq = jnp.where(half, k * k, 0.0).astype(BF16)
        qn_ref[idx:idx + 1, :] = lax.dot_general(ones8, qsq, nt, preferred_element_type=F32)[0:1]
        kn = lax.dot_general(ones8, ksq, nt, preferred_element_type=F32)[0:1]
        kmax.append(jnp.broadcast_to(jnp.max(kn, axis=1, keepdims=True), (1, LANES)))
    kcur = jnp.concatenate(kmax, axis=0)

    @pl.when(pl.program_id(1) == 0)
    def _():
        km_ref[0] = kcur

    @pl.when(pl.program_id(1) > 0)
    def _():
        km_ref[0] = jnp.maximum(km_ref[0], kcur)


def _attn_norms(qarr, qcol, karr, kcol, B, S, tile=512):
    nslab = 4
    width = nslab * LANES
    nq = S // tile
    return pl.pallas_call(
        functools.partial(_norms_body, nslab),
        out_shape=(jax.ShapeDtypeStruct((2 * nslab, B * S), F32), jax.ShapeDtypeStruct((B, 2 * nslab, LANES), F32)),
        grid=(B, nq),
        in_specs=[pl.BlockSpec((tile, width), lambda b, i: (b * nq + i, qcol // width)),
                  pl.BlockSpec((tile, width), lambda b, i: (b * nq + i, kcol // width))],
        out_specs=(pl.BlockSpec((2 * nslab, tile), lambda b, i: (0, b * nq + i)),
                   pl.BlockSpec((1, 2 * nslab, LANES), lambda b, i: (b, 0, 0))),
        compiler_params=_cparams(("arbitrary", "arbitrary")),
        name="attn_norms",
    )(qarr, karr)


LOOKAHEAD = 1
DEN_FLOOR = 1e-30


def _attention(mode, qarr, qcol, karr, kcol, vt, vrow, B, S, extra, lam_init=0.0, tile=512):
    qn, km = _attn_norms(qarr, qcol, karr, kcol, B, S, tile)
    o_fast, den = _attention_call(mode, qarr, qcol, karr, kcol, vt, vrow, B, S, extra, lam_init, tile, (qn, km))
    bad = jnp.logical_not(jnp.all(den > DEN_FLOOR))
    return lax.cond(
        bad,
        lambda: _attention_call(mode, qarr, qcol, karr, kcol, vt, vrow, B, S, extra, lam_init, tile, None),
        lambda: o_fast)


def _attention_call(mode, qarr, qcol, karr, kcol, vt, vrow, B, S, extra, lam_init, tile, norms):
    nslab = 4
    width = nslab * LANES
    nq = S // tile
    pairs = [(qi, ki) for qi in range(nq) for ki in range(qi + 1)]
    qt = jnp.asarray([pq for pq, _ in pairs], jnp.int32)
    kt = jnp.asarray([pk for _, pk in pairs], jnp.int32)
    T = B * S
    fixed_ref = norms is not None
    qmap = lambda b, p, qt, kt: (b * nq + qt[p], qcol // width)
    kmap = lambda b, p, qt, kt: (b * nq + kt[p], kcol // width)
    vmap = lambda b, p, qt, kt: (vrow // width, b * nq + kt[p])
    in_specs = [pl.BlockSpec((tile, width), qmap), pl.BlockSpec((tile, width), kmap),
                pl.BlockSpec((width, tile), vmap)]
    args = [qarr, karr, vt]
    if fixed_ref:
        in_specs += [pl.BlockSpec((2 * nslab, tile), lambda b, p, qt, kt: (0, b * nq + qt[p])),
                     pl.BlockSpec((1, 2 * nslab, LANES), lambda b, p, qt, kt: (b, 0, 0))]
        args += list(norms)
    if mode == "diff":
        lam4, g = extra
        in_specs += [pl.BlockSpec((4, HEAD_DIM), lambda b, p, qt, kt: (0, 0)),
                     pl.BlockSpec((1, LANES), lambda b, p, qt, kt: (0, 0))]
        args += [lam4, g.reshape(1, LANES)]
    elif mode == "dsa":
        (mask_t,) = extra
        in_specs += [pl.BlockSpec((tile, tile), lambda b, p, qt, kt: (b * nq + kt[p], qt[p]))]
        args += [mask_t]
    else:
        ct, crep = extra
        in_specs += [pl.BlockSpec((1, FOX_HEADS, tile), lambda b, p, qt, kt: (b, 0, qt[p])),
                     pl.BlockSpec((tile, FOX_HEADS * LANES), lambda b, p, qt, kt: (b * nq + kt[p], 0))]
        args += [ct, crep]
    out_shape = jax.ShapeDtypeStruct((T, width), BF16)
    out_specs = pl.BlockSpec((tile, width), lambda b, p, qt, kt: (b * nq + qt[p], 0))
    if fixed_ref:
        out_shape = (out_shape, jax.ShapeDtypeStruct((2 * nslab, T), F32))
        out_specs = (out_specs, pl.BlockSpec((2 * nslab, tile), lambda b, p, qt, kt: (0, b * nq + qt[p])))
    return pl.pallas_call(
        functools.partial(_attn_body, mode, nslab, lam_init, fixed_ref),
        out_shape=out_shape,
        grid_spec=pltpu.PrefetchScalarGridSpec(
            num_scalar_prefetch=2,
            grid=(B, len(pairs)),
            in_specs=in_specs,
            out_specs=out_specs,
            scratch_shapes=[pltpu.VMEM((2 * nslab, tile), F32),
                            pltpu.VMEM((2 * nslab, LANES + 8, tile), F32)]),
        compiler_params=_cparams(("arbitrary", "arbitrary")),
        name="attn_" + mode + ("_fixed" if fixed_ref else "_online"),
    )(qt, kt, *args)


def _select_body(topk, iq_ref, ik_ref, iwt_ref, tri_ref, mask_ref, keys_ref, hi_ref, lo_ref):
    qi = pl.program_id(1)
    tq = iq_ref.shape[0]
    tk = tq
    nkb = qi + 1
    q0 = qi * tq
    iq = iq_ref[...]
    qhs = [iq[:, h * HEAD_DIM:(h + 1) * HEAD_DIM] for h in range(IDX_HEADS)]
    iwt = iwt_ref[...]
    q_chunk = (q0 + lax.broadcasted_iota(jnp.int32, (tk, tq), 1)) >> 6
    k_iota = lax.broadcasted_iota(jnp.int32, (tk, tq), 0)

    mask_ref[...] = jnp.zeros_like(mask_ref)

    def score_block(j, carry):
        k0 = pl.multiple_of(j * tk, tk)
        ik = ik_ref[pl.ds(k0, tk), 0:HEAD_DIM]
        score = jnp.zeros((tk, tq), F32)
        for h in range(IDX_HEADS):
            lg = lax.dot_general(ik, qhs[h], (((1,), (1,)), ((), ())), preferred_element_type=F32)
            score = score + iwt[SM_IW + h:SM_IW + h + 1, :] * jnp.maximum(lg, 0.0)
        score = jnp.where(score == 0.0, 0.0, score)
        bits = lax.bitcast_convert_type(score, jnp.int32)
        key = bits ^ ((bits >> 31) & jnp.int32(0x7FFFFFFF))
        vis = ((k0 + k_iota) >> 6) <= q_chunk
        key = jnp.where(vis, key, jnp.int32(INT_MIN))
        keys_ref[pl.ds(k0, tk), :] = key
        hi_ref[pl.ds(k0, tk), :] = (key >> 16).astype(jnp.int16)
        lo_ref[pl.ds(k0, tk), :] = ((key & jnp.int32(0xFFFF)) - HALF).astype(jnp.int16)
        return carry

    lax.fori_loop(0, nkb, score_block, 0)

    npair = (nkb + 1) >> 1

    @pl.when((nkb & 1) == 1)
    def _():
        k0 = pl.multiple_of(nkb * tk, tk)
        hi_ref[pl.ds(k0, tk), :] = jnp.full((tk, tq), -HALF, jnp.int16)
        lo_ref[pl.ds(k0, tk), :] = jnp.full((tk, tq), -HALF, jnp.int16)

    def count16(ref, pred):
        def body(j, acc):
            k0 = pl.multiple_of(j * (2 * tk), 2 * tk)
            hit = jnp.where(pred(ref[pl.ds(k0, 2 * tk), :]), jnp.int16(1), jnp.int16(0))
            for r in range(2 * tk // 16):
                acc = acc + hit[r * 16:(r + 1) * 16, :]
            return acc
        acc = lax.fori_loop(0, npair, body, jnp.zeros((16, tq), jnp.int16))
        return jnp.sum(acc.astype(jnp.int32), axis=0, keepdims=True)

    def search16(ref, base):
        def bit_step(i, lo):
            cand = lo + (jnp.int32(1) << (15 - i))
            c16 = cand.astype(jnp.int16)
            return jnp.where(base + count16(ref, lambda blk: blk >= c16) >= topk, cand, lo)
        return lax.fori_loop(0, 16, bit_step, jnp.full((1, tq), -HALF, jnp.int32))

    thr_hi = search16(hi_ref, 0)
    thr_hi16 = thr_hi.astype(jnp.int16)
    above = count16(hi_ref, lambda blk: blk > thr_hi16)

    def keep_low_of_threshold_rows(j, carry):
        k0 = pl.multiple_of(j * tk, tk)
        lo_ref[pl.ds(k0, tk), :] = jnp.where(hi_ref[pl.ds(k0, tk), :] == thr_hi16, lo_ref[pl.ds(k0, tk), :],
                                             jnp.int16(-HALF))
        return carry

    lax.fori_loop(0, nkb, keep_low_of_threshold_rows, 0)
    thr_lo = search16(lo_ref, above)
    thr_lo16 = thr_lo.astype(jnp.int16)
    above = above + count16(lo_ref, lambda blk: blk > thr_lo16)
    thr = (thr_hi << 16) + (thr_lo + HALF)
    thr = jnp.maximum(thr, jnp.int32(INT_MIN + 1))
    need = (topk - above).astype(F32)
    tri = tri_ref[...]

    def select_block(j, carry):
        k0 = pl.multiple_of(j * tk, tk)
        key = keys_ref[pl.ds(k0, tk), :]
        eq = key == thr
        incl = jnp.dot(tri, jnp.where(eq, 1.0, 0.0).astype(BF16), preferred_element_type=F32)
        sel = (key > thr) | (eq & ((carry + incl) <= need))
        mask_ref[pl.ds(k0, tk), :] = jnp.where(sel, 1, 0).astype(jnp.int8)
        return carry + incl[tk - 1:tk, :]

    lax.fori_loop(0, nkb, select_block, jnp.zeros((1, tq), F32))


def _dsa_select(qk, small_t, B, S, topk, tq=256):
    T = B * S
    nq = S // tq
    tri = jnp.tril(jnp.ones((tq, tq), F32)).astype(BF16)
    return pl.pallas_call(
        functools.partial(_select_body, topk),
        out_shape=jax.ShapeDtypeStruct((T, S), jnp.int8),
        grid=(B, nq),
        in_specs=[pl.BlockSpec((tq, 256), lambda b, i: (b * nq + i, QK_IQ // 256)),
                  pl.BlockSpec((S, 256), lambda b, i: (b, QK_IK // 256)),
                  pl.BlockSpec((8, tq), lambda b, i: (0, b * nq + i)),
                  pl.BlockSpec((tq, tq), lambda b, i: (0, 0))],
        out_specs=pl.BlockSpec((S, tq), lambda b, i: (b, i)),
        scratch_shapes=[pltpu.VMEM((S, tq), jnp.int32), pltpu.VMEM((S, tq), jnp.int16),
                        pltpu.VMEM((S, tq), jnp.int16)],
        compiler_params=_cparams(("arbitrary", "arbitrary")),
        name="dsa_select",
    )(qk, qk, small_t, tri)


def _merge_body(alpha, od_ref, os_ref, of_ref, gate_ref, h_ref, wbr_ref, wo_ref, g_ref, b_ref, wr_ref,
                h32_ref, h16_ref, aff_ref):
    D = h_ref.shape[1]
    y = None
    for i, o_ref in enumerate((od_ref, os_ref, of_ref)):
        br = jnp.dot(o_ref[...], wbr_ref[i], preferred_element_type=F32)
        t = gate_ref[:, i * D:(i + 1) * D].astype(F32) * br
        y = t if y is None else y + t
    m = jnp.dot(y.astype(BF16), wo_ref[...], preferred_element_type=F32)
    hn = _ln_rows(alpha * h_ref[...] + m, g_ref[...], b_ref[...])
    h32_ref[...] = hn
    h16 = hn.astype(BF16)
    h16_ref[...] = h16
    logits = lax.dot_general(wr_ref[...], h16, (((1,), (1,)), ((), ())), preferred_element_type=F32)
    aff_ref[...] = 1.0 / (1.0 + jnp.exp(-logits))


def _merge(alpha, od, os_, of, gates, h32, wbr, wo, g, b, wr_t, tm=512):
    T, D = h32.shape
    bw = od.shape[1]
    row = lambda i: (i, 0)
    fixed2 = lambda i: (0, 0)
    return pl.pallas_call(
        functools.partial(_merge_body, alpha),
        out_shape=(jax.ShapeDtypeStruct((T, D), F32), jax.ShapeDtypeStruct((T, D), BF16),
                   jax.ShapeDtypeStruct((N_EXPERTS, T), F32)),
        grid=(T // tm,),
        in_specs=[pl.BlockSpec((tm, bw), row), pl.BlockSpec((tm, bw), row), pl.BlockSpec((tm, bw), row),
                  pl.BlockSpec((tm, N_BRANCH * D), row), pl.BlockSpec((tm, D), row),
                  pl.BlockSpec((N_BRANCH, bw, D), lambda i: (0, 0, 0)), pl.BlockSpec((D, D), fixed2),
                  pl.BlockSpec((1, D), fixed2), pl.BlockSpec((1, D), fixed2),
                  pl.BlockSpec((N_EXPERTS, D), fixed2)],
        out_specs=(pl.BlockSpec((tm, D), row), pl.BlockSpec((tm, D), row),
                   pl.BlockSpec((N_EXPERTS, tm), lambda i: (0, i))),
        compiler_params=_cparams(("arbitrary",)),
        name="merge",
    )(od, os_, of, gates, h32, wbr, wo, g.reshape(1, D), b.reshape(1, D), wr_t)


def _route_body(aff_ref, bias_ref, e_ref, w_ref):
    aff = aff_ref[...]
    sel = aff + bias_ref[...]
    sub = lax.broadcasted_iota(jnp.int32, (EXPERTS_PER_GROUP, aff.shape[1]), 0)
    big = jnp.int32(EXPERTS_PER_GROUP)
    best = None
    for gidx in range(N_GROUPS):
        rs = slice(gidx * EXPERTS_PER_GROUP, (gidx + 1) * EXPERTS_PER_GROUP)
        s8 = sel[rs]
        a8 = aff[rs]
        m1 = jnp.max(s8, axis=0, keepdims=True)
        i1 = jnp.min(jnp.where(s8 == m1, sub, big), axis=0, keepdims=True)
        rest = jnp.where(sub == i1, -jnp.inf, s8)
        m2 = jnp.max(rest, axis=0, keepdims=True)
        i2 = jnp.min(jnp.where(rest == m2, sub, big), axis=0, keepdims=True)
        a1 = jnp.sum(jnp.where(sub == i1, a8, 0.0), axis=0, keepdims=True)
        a2 = jnp.sum(jnp.where(sub == i2, a8, 0.0), axis=0, keepdims=True)
        score = m1 + m2
        cand = (score, i1 + gidx * EXPERTS_PER_GROUP, i2 + gidx * EXPERTS_PER_GROUP, a1, a2)
        if best is None:
            best = cand
        else:
            take = score > best[0]
            best = tuple(jnp.where(take, c, o) for c, o in zip(cand, best))
    _, e1, e2, a1, a2 = best
    tot = a1 + a2
    e_ref[0:1, :] = e1
    e_ref[1:2, :] = e2
    w_ref[0:1, :] = a1 / tot
    w_ref[1:2, :] = a2 / tot


def _route(aff_t, b_router, tn=2048):
    E, T = aff_t.shape
    tn = min(tn, T)
    return pl.pallas_call(
        _route_body,
        out_shape=(jax.ShapeDtypeStruct((TOP_K, T), jnp.int32), jax.ShapeDtypeStruct((TOP_K, T), F32)),
        grid=(T // tn,),
        in_specs=[pl.BlockSpec((E, tn), lambda i: (0, i)), pl.BlockSpec((E, 1), lambda i: (0, 0))],
        out_specs=(pl.BlockSpec((TOP_K, tn), lambda i: (0, i)), pl.BlockSpec((TOP_K, tn), lambda i: (0, i))),
        compiler_params=_cparams(("arbitrary",)),
        name="route",
    )(aff_t, b_router.reshape(E, 1))


def _expert_body(be_ref, bc_ref, tok0_ref, tokn_ref, slot_ref, x_hbm, wg_ref, wu_ref, wd_ref, y_hbm,
                 xbuf, ybuf, gsem, ssem, wg16, wu16, wd16):
    i = pl.program_id(0)
    nb = pl.num_programs(0)
    cur = i & 1
    bm = xbuf.shape[1]
    n_real = y_hbm.shape[0] - 2 * bm

    def start_gather(tok_ref, slot):
        for r in range(bm):
            pltpu.make_async_copy(x_hbm.at[pl.ds(tok_ref[0, 0, r], 1), :], xbuf.at[slot, pl.ds(r, 1), :],
                                  gsem.at[slot]).start(priority=r % 2)

    def wait_gather(slot):
        pltpu.make_async_copy(x_hbm.at[pl.ds(0, bm), :], xbuf.at[slot], gsem.at[slot]).wait()

    def wait_scatter(slot):
        pltpu.make_async_copy(ybuf.at[slot], y_hbm.at[pl.ds(0, bm), :], ssem.at[slot]).wait()

    @pl.when(i == 0)
    def _():
        ybuf[...] = jnp.zeros_like(ybuf)
        for s in range(2):
            spare = pltpu.make_async_copy(ybuf.at[s], y_hbm.at[pl.ds(n_real + s * bm, bm), :], ssem.at[s])
            spare.start()
            spare.wait()
        start_gather(tok0_ref, 0)

    @pl.when(i + 1 < nb)
    def _():
        start_gather(tokn_ref, 1 - cur)

    wait_gather(cur)

    @pl.when(i >= 2)
    def _():
        wait_scatter(cur)

    @pl.when((i == 0) | (be_ref[i] != be_ref[jnp.maximum(i - 1, 0)]))
    def _():
        wg16[...] = wg_ref[0].astype(BF16)
        wu16[...] = wu_ref[0].astype(BF16)
        wd16[...] = wd_ref[0].astype(BF16)

    @pl.when(bc_ref[i] > 0)
    def _():
        x = xbuf[cur].astype(BF16)
        gate = jnp.dot(x, wg16[...], preferred_element_type=F32)
        up = jnp.dot(x, wu16[...], preferred_element_type=F32)
        hmid = (gate * (1.0 / (1.0 + jnp.exp(-gate)))) * up
        ybuf[cur] = jnp.dot(hmid.astype(BF16), wd16[...], preferred_element_type=F32)

    for r in range(bm):
        pltpu.make_async_copy(ybuf.at[cur, pl.ds(r, 1), :], y_hbm.at[pl.ds(slot_ref[0, 0, r], 1), :],
                              ssem.at[cur]).start(priority=r % 2)

    @pl.when(i == nb - 1)
    def _():
        wait_scatter(cur)

        @pl.when(nb > 1)
        def _():
            wait_scatter(1 - cur)


def _experts(h32, blk_e, blk_cnt, tok_pad, slot_pad, wg, wu, wd, bm):
    T, D = h32.shape
    nb = blk_e.shape[0]
    de = wg.shape[2]
    smem_blk = lambda f: pl.BlockSpec((1, 1, bm), f, memory_space=pltpu.SMEM)
    return pl.pallas_call(
        _expert_body,
        out_shape=jax.ShapeDtypeStruct((TOP_K * T + 2 * bm, D), F32),
        grid_spec=pltpu.PrefetchScalarGridSpec(
            num_scalar_prefetch=2,
            grid=(nb,),
            in_specs=[smem_blk(lambda i, be, bc: (0, 0, 0)),
                      smem_blk(lambda i, be, bc: (jnp.minimum(i + 1, nb - 1), 0, 0)),
                      smem_blk(lambda i, be, bc: (i, 0, 0)),
                      pl.BlockSpec(memory_space=pl.ANY),
                      pl.BlockSpec((1, D, de), lambda i, be, bc: (be[i], 0, 0)),
                      pl.BlockSpec((1, D, de), lambda i, be, bc: (be[i], 0, 0)),
                      pl.BlockSpec((1, de, D), lambda i, be, bc: (be[i], 0, 0))],
            out_specs=pl.BlockSpec(memory_space=pl.ANY),
            scratch_shapes=[pltpu.VMEM((2, bm, D), F32), pltpu.VMEM((2, bm, D), F32),
                            pltpu.SemaphoreType.DMA((2,)), pltpu.SemaphoreType.DMA((2,)),
                            pltpu.VMEM((D, de), BF16), pltpu.VMEM((D, de), BF16), pltpu.VMEM((de, D), BF16)]),
        compiler_params=_cparams(("arbitrary",)),
        name="experts",
    )(blk_e, blk_cnt, tok_pad, tok_pad, slot_pad, h32, wg, wu, wd)


def _dispatch_tables(e_t, T, bm):
    A = TOP_K * T
    flat_e = e_t.T.reshape(A)
    order = jnp.argsort(flat_e, stable=True).astype(jnp.int32)
    counts = jnp.sum(flat_e[None, :] == jnp.arange(N_EXPERTS, dtype=jnp.int32)[:, None], axis=1).astype(jnp.int32)
    start = jnp.cumsum(counts) - counts
    nblk_e = (counts + bm - 1) // bm
    blk_end = jnp.cumsum(nblk_e)
    nb = A // bm + N_EXPERTS
    bidx = jnp.arange(nb, dtype=jnp.int32)
    blk_e = jnp.minimum(jnp.sum(bidx[:, None] >= blk_end[None, :], axis=1), N_EXPERTS - 1).astype(jnp.int32)
    off = (bidx - (blk_end - nblk_e)[blk_e]) * bm
    blk_cnt = jnp.where(bidx < blk_end[-1], jnp.clip(counts[blk_e] - off, 0, bm), 0).astype(jnp.int32)
    r = jnp.arange(bm, dtype=jnp.int32)[None, :]
    real = r < blk_cnt[:, None]
    a = order[jnp.clip((start[blk_e] + off)[:, None] + r, 0, A - 1)]
    tok_pad = jnp.where(real, a // TOP_K, 0)
    slot_pad = jnp.where(real, (a % TOP_K) * T + a // TOP_K, A + (bidx % 2)[:, None] * bm + r)
    return blk_e, blk_cnt, tok_pad.reshape(nb, 1, bm), slot_pad.reshape(nb, 1, bm)


def _combine_body(alpha, h_ref, y0_ref, y1_ref, w_ref, g_ref, b_ref, o32_ref, o16_ref):
    w = w_ref[...]
    f = w[:, 0:1] * y0_ref[...] + w[:, 1:2] * y1_ref[...]
    hn = _ln_rows(alpha * h_ref[...] + f, g_ref[...], b_ref[...])
    o32_ref[...] = hn
    o16_ref[...] = hn.astype(BF16)


def _combine(alpha, h32, y2, w_rows, g, b, tm=512):
    T, D = h32.shape
    nt = T // tm
    row = lambda i: (i, 0)
    fixed = lambda i: (0, 0)
    return pl.pallas_call(
        functools.partial(_combine_body, alpha),
        out_shape=(jax.ShapeDtypeStruct((T, D), F32), jax.ShapeDtypeStruct((T, D), BF16)),
        grid=(nt,),
        in_specs=[pl.BlockSpec((tm, D), row), pl.BlockSpec((tm, D), row),
                  pl.BlockSpec((tm, D), lambda i: (nt + i, 0)),
                  pl.BlockSpec((tm, TOP_K), row), pl.BlockSpec((1, D), fixed), pl.BlockSpec((1, D), fixed)],
        out_specs=(pl.BlockSpec((tm, D), row), pl.BlockSpec((tm, D), row)),
        compiler_params=_cparams(("arbitrary",)),
        name="combine",
    )(h32, y2, y2, w_rows, g.reshape(1, D), b.reshape(1, D))


def _rotate_half_columns(w):
    d = w.shape[0]
    w4 = w.reshape(d, -1, 2, HEAD_DIM // 2)
    return jnp.concatenate([-w4[:, :, 1:2], w4[:, :, 0:1]], axis=2).reshape(w.shape)


def _prep_in_weights(w_in_l):
    qk_w = DIFF_HEADS * 2 * HEAD_DIM
    sizes = (qk_w, qk_w, DIFF_HEADS * 2 * HEAD_DIM, 512, 512, 512, IDX_HEADS * HEAD_DIM, HEAD_DIM, IDX_HEADS,
             512, 512, 512, FOX_HEADS, N_BRANCH * w_in_l.shape[0])
    offs = [0]
    for s in sizes:
        offs.append(offs[-1] + s)
    dq, dk, dv, sq, sk, sv, iq, ik, iw, fq, fk, fv, ff, g = (w_in_l[:, offs[i]:offs[i + 1]] for i in range(len(sizes)))
    scale = HEAD_DIM ** -0.5
    idx_scale = (IDX_HEADS * HEAD_DIM) ** -0.5
    w_rope = jnp.concatenate([dq * scale, dk, sq * scale, sk, iq, ik, ik, ik, ik], axis=1)
    w_rot = _rotate_half_columns(w_rope)
    w_plain = jnp.concatenate([fq * scale, fk], axis=1)
    w_vt = jnp.concatenate([dv, sv, fv], axis=1).T
    pad = jnp.zeros((w_in_l.shape[0], LANES - IDX_HEADS - FOX_HEADS), w_in_l.dtype)
    w_small = jnp.concatenate([iw * idx_scale, ff, pad], axis=1)
    return (w_rope.astype(BF16), w_rot.astype(BF16), w_plain.astype(BF16), w_vt.astype(BF16), g.astype(BF16),
            w_small.astype(BF16))


def _rope_tables(positions):
    inv = ROPE_THETA ** (-jnp.arange(0, HEAD_DIM, 2, dtype=F32) / HEAD_DIM)
    ang = positions.astype(F32).reshape(-1, 1) * inv
    c, s = jnp.cos(ang), jnp.sin(ang)
    return jnp.tile(c, (1, LANES // (HEAD_DIM // 2))), jnp.tile(s, (1, LANES // (HEAD_DIM // 2)))


def kernel(x, positions, ln_in_g, ln_in_b, w_in, b_forget, diff_lam, diff_norm_g, w_br, w_o, ln1_g, ln1_b,
           w_router, b_router, w_gate, w_up, w_down, ln2_g, ln2_b):
    B, S, D = x.shape
    T = B * S
    depth = w_in.shape[0]
    alpha = (2 * depth) ** 0.25
    topk = min(DSA_TOPK_MAX, S // 4)
    bm = 256
    cc, ss = _rope_tables(positions)
    wr_t = w_router.T.astype(BF16)
    h32, h16 = _layer_norm(x.reshape(T, D), ln_in_g, ln_in_b)
    for l in range(depth):
        lam_init = 0.8 - 0.6 * math.exp(-0.3 * l)
        w_rope, w_rot, w_plain, w_vt, w_g, w_small = _prep_in_weights(w_in[l])
        qk = _project(h16, w_rope, "rope", BF16, tables=(cc, ss), wrot=w_rot)
        fqk = _project(h16, w_plain, "plain", BF16)
        vt = _project(h16, w_vt, "transposed", BF16)
        gates = _project(h16, w_g, "sigmoid", BF16)
        small = _project(h16, w_small, "f32", F32)

        o_diff = _attention("diff", qk, QK_DQ, qk, QK_DK, vt, VT_DV, B, S,
                            (diff_lam[l], diff_norm_g[l]), lam_init=lam_init)
        small_t = _project(h16, w_small.T, "transposed", F32)
        mask_t = _dsa_select(qk, small_t, B, S, topk)
        o_dsa = _attention("dsa", qk, QK_SQ, qk, QK_SK, vt, VT_SV, B, S, (mask_t,))
        bias_row = jnp.zeros((1, LANES), F32).at[0, SM_FF:SM_FF + FOX_HEADS].set(b_forget[l])
        c, crep = _forget_cumsum(small, bias_row, B, S)
        ct = jnp.transpose(c[:, SM_FF:SM_FF + FOX_HEADS].reshape(B, S, FOX_HEADS), (0, 2, 1))
        o_fox = _attention("fox", fqk, FQK_FQ, fqk, FQK_FK, vt, VT_FV, B, S, (ct, crep))

        h32, h16, aff_t = _merge(alpha, o_diff, o_dsa, o_fox, gates, h32, w_br[l].astype(BF16),
                                 w_o[l].astype(BF16), ln1_g[l], ln1_b[l], wr_t)
        e_t, w_t = _route(aff_t, b_router)
        blk_e, blk_cnt, tok_pad, slot_pad = _dispatch_tables(e_t, T, bm)
        y2 = _experts(h32, blk_e + l * N_EXPERTS, blk_cnt, tok_pad, slot_pad,
                      w_gate.reshape((-1,) + w_gate.shape[2:]), w_up.reshape((-1,) + w_up.shape[2:]),
                      w_down.reshape((-1,) + w_down.shape[2:]), bm)
        h32, h16 = _combine(alpha, h32, y2, w_t.T, ln2_g[l], ln2_b[l])
    return h32.reshape(B, S, D)
```

```python
import functools
import math

import jax
import jax.numpy as jnp
from jax import lax
from jax.experimental import pallas as pl
from jax.experimental.pallas import tpu as pltpu

F32 = jnp.float32
BF16 = jnp.bfloat16

CHUNK = 64
HEAD_DIM = 64
ROPE_THETA = 10000.0
LN_EPS = 1e-5
DIFF_HEADS = 4
DSA_HEADS = 8
IDX_HEADS = 4
DSA_TOPK_MAX = 256
FOX_HEADS = 8
N_BRANCH = 3
BRANCH_WIDTH = 512
N_EXPERTS = 32
N_GROUPS = 4
EXPERTS_PER_GROUP = N_EXPERTS // N_GROUPS
TOP_K = 2

LANES = 128
NEG = -1e30
INT_MIN = -2 ** 31
HALF = 1 << 15

QK_DQ, QK_DK, QK_SQ, QK_SK, QK_IQ, QK_IK = 0, 512, 1024, 1536, 2048, 2304
QK_WIDTH = 2560
FQK_FQ, FQK_FK = 0, 512
VT_DV, VT_SV, VT_FV = 0, 512, 1024
SM_IW, SM_FF = 0, 4
VMEM_LIMIT = 56 * 1024 * 1024


def _cparams(sem):
    return pltpu.CompilerParams(dimension_semantics=sem, vmem_limit_bytes=VMEM_LIMIT)


def _ln_rows(x, g, b):
    mu = jnp.mean(x, axis=-1, keepdims=True)
    xc = x - mu
    var = jnp.mean(xc * xc, axis=-1, keepdims=True)
    return xc * lax.rsqrt(var + LN_EPS) * g + b


def _ln_body(x_ref, g_ref, b_ref, o32_ref, o16_ref):
    y = _ln_rows(x_ref[...], g_ref[...], b_ref[...])
    o32_ref[...] = y
    o16_ref[...] = y.astype(BF16)


def _layer_norm(x, g, b, tm=512):
    T, D = x.shape
    return pl.pallas_call(
        _ln_body,
        out_shape=(jax.ShapeDtypeStruct((T, D), F32), jax.ShapeDtypeStruct((T, D), BF16)),
        grid=(T // tm,),
        in_specs=[pl.BlockSpec((tm, D), lambda i: (i, 0)),
                  pl.BlockSpec((1, D), lambda i: (0, 0)),
                  pl.BlockSpec((1, D), lambda i: (0, 0))],
        out_specs=(pl.BlockSpec((tm, D), lambda i: (i, 0)), pl.BlockSpec((tm, D), lambda i: (i, 0))),
        compiler_params=_cparams(("arbitrary",)),
        name="layer_norm",
    )(x, g.reshape(1, D), b.reshape(1, D))


def _proj_body(mode, width, n_norm, steps_per_batch, h_ref, w_ref, *rest):
    if mode == "rope":
        cc_ref, ss_ref, e_ref, o_ref, nmax_ref = rest
    elif mode == "plain":
        e_ref, o_ref, nmax_ref = rest
    else:
        (o_ref,) = rest
    h = h_ref[...]
    tn = 512 if width % 512 == 0 else width
    if n_norm:
        @pl.when(pl.program_id(0) % steps_per_batch == 0)
        def _():
            nmax_ref[...] = jnp.zeros_like(nmax_ref)
    for j in range(width // tn):
        cs = slice(j * tn, (j + 1) * tn)
        if mode == "transposed":
            o_ref[cs, :] = lax.dot_general(w_ref[cs, :], h, (((1,), (1,)), ((), ())),
                                           preferred_element_type=F32).astype(o_ref.dtype)
            continue
        z = jnp.dot(h, w_ref[:, cs], preferred_element_type=F32)
        if j < n_norm:
            n2 = jnp.dot((z * z).astype(BF16), e_ref[...], preferred_element_type=F32)
            nmax_ref[0, j:j + 1, :] = jnp.maximum(nmax_ref[0, j:j + 1, :], jnp.max(n2, axis=0, keepdims=True))
        if mode == "rope":
            cc = cc_ref[...]
            ss = ss_ref[...]
            for i in range(tn // LANES):
                zs = z[:, i * LANES:(i + 1) * LANES]
                o_ref[:, j * tn + i * LANES:j * tn + (i + 1) * LANES] = (
                    zs * cc + pltpu.roll(zs, LANES // 2, 1) * ss).astype(o_ref.dtype)
        elif mode == "sigmoid":
            o_ref[:, cs] = (1.0 / (1.0 + jnp.exp(-z))).astype(o_ref.dtype)
        else:
            o_ref[:, cs] = z.astype(o_ref.dtype)


def _subhead_indicator():
    col = jnp.arange(4 * LANES)
    g = 2 * (col // LANES) + ((col % LANES) // (HEAD_DIM // 2)) % 2
    return (g[:, None] == jnp.arange(LANES)[None, :]).astype(BF16)


def _project(h16, w, mode, out_dtype, B=1, tables=None, n_norm=0, tm=512):
    T, D = h16.shape
    transposed = mode == "transposed"
    width = w.shape[0] if transposed else w.shape[1]
    steps_per_batch = T // B // tm
    row = lambda i: (i, 0)
    fixed = lambda i: (0, 0)
    in_specs = [pl.BlockSpec((tm, D), row), pl.BlockSpec(w.shape, fixed)]
    args = [h16, w]
    if mode == "rope":
        cc, ss = tables
        in_specs += [pl.BlockSpec((tm, LANES), row), pl.BlockSpec((tm, LANES), row)]
        args += [cc, ss]
    out_shape = jax.ShapeDtypeStruct((width, T) if transposed else (T, width), out_dtype)
    out_specs = pl.BlockSpec((width, tm), lambda i: (0, i)) if transposed else pl.BlockSpec((tm, width), row)
    if n_norm:
        in_specs += [pl.BlockSpec((4 * LANES, LANES), fixed)]
        args += [_subhead_indicator()]
        out_shape = (out_shape, jax.ShapeDtypeStruct((B, 8, LANES), F32))
        out_specs = (out_specs, pl.BlockSpec((1, 8, LANES), lambda i: (i // steps_per_batch, 0, 0)))
    return pl.pallas_call(
        functools.partial(_proj_body, mode, width, n_norm, steps_per_batch),
        out_shape=out_shape,
        grid=(T // tm,),
        in_specs=in_specs,
        out_specs=out_specs,
        compiler_params=_cparams(("arbitrary",)),
        name="proj_" + mode,
    )(*args)


def _cumsum_body(x_ref, bias_ref, tri_ref, c_ref, crep_ref, carry_ref):
    @pl.when(pl.program_id(1) == 0)
    def _():
        carry_ref[...] = jnp.zeros_like(carry_ref)

    x = x_ref[...] + bias_ref[...]
    lf = jnp.minimum(x, 0.0) - jnp.log1p(jnp.exp(-jnp.abs(x)))
    hi = lf.astype(BF16)
    r1 = lf - hi.astype(F32)
    mid = r1.astype(BF16)
    lo = (r1 - mid.astype(F32)).astype(BF16)
    tri = tri_ref[...]
    c = (jnp.dot(tri, hi, preferred_element_type=F32) + jnp.dot(tri, mid, preferred_element_type=F32)
         + jnp.dot(tri, lo, preferred_element_type=F32)) + carry_ref[...]
    c_ref[...] = c
    carry_ref[...] = c[-1:, :]
    for h in range(FOX_HEADS):
        crep_ref[:, h * LANES:(h + 1) * LANES] = jnp.broadcast_to(c[:, SM_FF + h:SM_FF + h + 1], (c.shape[0], LANES))


def _forget_cumsum(small, bias_row, B, S, tb=512):
    T = small.shape[0]
    nb = S // tb
    tri = jnp.tril(jnp.ones((tb, tb), F32)).astype(BF16)
    return pl.pallas_call(
        _cumsum_body,
        out_shape=(jax.ShapeDtypeStruct((T, LANES), F32), jax.ShapeDtypeStruct((T, FOX_HEADS * LANES), F32)),
        grid=(B, nb),
        in_specs=[pl.BlockSpec((tb, LANES), lambda b, i: (b * nb + i, 0)),
                  pl.BlockSpec((1, LANES), lambda b, i: (0, 0)),
                  pl.BlockSpec((tb, tb), lambda b, i: (0, 0))],
        out_specs=(pl.BlockSpec((tb, LANES), lambda b, i: (b * nb + i, 0)),
                   pl.BlockSpec((tb, FOX_HEADS * LANES), lambda b, i: (b * nb + i, 0))),
        scratch_shapes=[pltpu.VMEM((1, LANES), F32)],
        compiler_params=_cparams(("arbitrary", "arbitrary")),
        name="forget_cumsum",
    )(small, bias_row, tri)


def _attn_body(mode, nslab, lam_init, fixed_ref, qt_ref, kt_ref, q_ref, k_ref, vt_ref, *rest):
    if fixed_ref:
        qn_ref, kn_ref = rest[:2]
        rest = rest[2:]
    if mode == "diff":
        lam_ref, g_ref = rest[:2]
    elif mode == "dsa":
        (mask_ref,) = rest[:1]
    else:
        cq_ref, ck_ref = rest[:2]
    if fixed_ref:
        o_ref, den_ref, m_sc, acc_sc = rest[-4:]
    else:
        o_ref, m_sc, acc_sc = rest[-3:]
    p = pl.program_id(1)
    qi = qt_ref[p]
    ki = kt_ref[p]
    tq = q_ref.shape[0]
    tk = k_ref.shape[0]

    @pl.when(ki == 0)
    def _():
        if fixed_ref:
            q_row, k_row = fixed_ref
            bound = 1.01 * jnp.sqrt(qn_ref[0, q_row:q_row + 1, :] * kn_ref[0, k_row:k_row + 1, :])
            sub_lane = lax.broadcasted_iota(jnp.int32, (1, LANES), 1)
            for idx in range(2 * nslab):
                b_idx = jnp.sum(jnp.where(sub_lane == idx, bound, 0.0), axis=1, keepdims=True)
                m_sc[idx:idx + 1, :] = jnp.broadcast_to(b_idx, (1, tq))
        else:
            m_sc[...] = jnp.full_like(m_sc, NEG)
        acc_sc[...] = jnp.zeros_like(acc_sc)

    lane = lax.broadcasted_iota(jnp.int32, (1, LANES), 1)

    def tile_update(diag):
        if mode == "dsa":
            keep_all = mask_ref[...] != 0
        elif diag:
            krow = lax.broadcasted_iota(jnp.int32, (tk, tq), 0)
            qcol = lax.broadcasted_iota(jnp.int32, (tk, tq), 1)
            if mode == "diff":
                keep_all = (krow >> 6) <= (qcol >> 6)
            else:
                keep_all = krow <= qcol
        def scores(idx):
            ls = slice((idx // 2) * LANES, (idx // 2 + 1) * LANES)
            q = q_ref[:, ls]
            qm = jnp.where(_subhead_lanes(lane, idx % 2), q, jnp.zeros_like(q))
            return lax.dot_general(k_ref[:, ls], qm, (((1,), (1,)), ((), ())), preferred_element_type=F32)

        ones_rows = jnp.ones((8, tk), BF16)
        ahead = [scores(i) for i in range(LOOKAHEAD)]
        for idx in range(2 * nslab):
            s = ahead.pop(0)
            if idx + LOOKAHEAD < 2 * nslab:
                ahead.append(scores(idx + LOOKAHEAD))
            vt = jnp.concatenate([vt_ref[(idx // 2) * LANES:(idx // 2 + 1) * LANES, :], ones_rows], axis=0)
            if mode == "fox":
                ck = ck_ref[:, idx * LANES:(idx + 1) * LANES]
                s = s - jnp.concatenate([ck] * (tq // LANES), axis=1)
            m_old = m_sc[idx:idx + 1, :]
            if fixed_ref:
                ref = (m_old - cq_ref[0, idx:idx + 1, :]) if mode == "fox" else m_old
                pr = jnp.exp(s - ref).astype(BF16)
                if mode == "dsa" or diag:
                    pr = jnp.where(keep_all, pr, jnp.zeros_like(pr))
                acc_sc[idx] = acc_sc[idx] + jnp.dot(vt, pr, preferred_element_type=F32)
                continue
            if mode == "dsa" or diag:
                s = jnp.where(keep_all, s, NEG)
            mx = jnp.max(s, axis=0, keepdims=True)
            if mode == "fox":
                cq = cq_ref[0, idx:idx + 1, :]
                m_new = jnp.maximum(m_old, mx + cq)
                shift = cq - m_new
            else:
                m_new = jnp.maximum(m_old, mx)
                shift = -m_new
            alpha = jnp.exp(m_old - m_new)
            pr = jnp.exp(s + shift).astype(BF16)
            acc_sc[idx] = alpha * acc_sc[idx] + jnp.dot(vt, pr, preferred_element_type=F32)
            m_sc[idx:idx + 1, :] = m_new

    if mode == "dsa":
        tile_update(False)
    else:
        @pl.when(ki == qi)
        def _():
            tile_update(True)

        @pl.when(ki != qi)
        def _():
            tile_update(False)

    @pl.when(ki == qi)
    def _():
        if mode == "diff":
            lm = lam_ref[...]
            lam = (jnp.exp(jnp.sum(lm[0:1] * lm[1:2], keepdims=True))
                   - jnp.exp(jnp.sum(lm[2:3] * lm[3:4], keepdims=True)) + lam_init)
        feat = lax.broadcasted_iota(jnp.int32, (LANES, 1), 0)
        if fixed_ref:
            for idx in range(2 * nslab):
                den_ref[idx:idx + 1, :] = acc_sc[idx, LANES:LANES + 1, :]
        for sl in range(nslab):
            o0 = acc_sc[2 * sl, 0:LANES, :] / acc_sc[2 * sl, LANES:LANES + 1, :]
            o1 = acc_sc[2 * sl + 1, 0:LANES, :] / acc_sc[2 * sl + 1, LANES:LANES + 1, :]
            if mode == "diff":
                o = (o0 - lam * o1).T
                o = o * lax.rsqrt(jnp.mean(o * o, axis=-1, keepdims=True) + LN_EPS)
                o = (o * g_ref[...]) * (1.0 - lam_init)
            else:
                o = jnp.where(feat < HEAD_DIM, o0, o1).T
            o_ref[:, sl * LANES:(sl + 1) * LANES] = o.astype(o_ref.dtype)


def _subhead_lanes(lane, sub):
    return ((lane >> 5) & 1) == sub


LOOKAHEAD = 1
DEN_FLOOR = 1e-30


def _attention(mode, qarr, qcol, karr, kcol, vt, vrow, B, S, extra, norms, lam_init=0.0, tile=512):
    o_fast, den = _attention_call(mode, qarr, qcol, karr, kcol, vt, vrow, B, S, extra, lam_init, tile, norms)
    bad = jnp.logical_not(jnp.all(den > DEN_FLOOR))
    return lax.cond(
        bad,
        lambda: _attention_call(mode, qarr, qcol, karr, kcol, vt, vrow, B, S, extra, lam_init, tile, None),
        lambda: o_fast)


def _attention_call(mode, qarr, qcol, karr, kcol, vt, vrow, B, S, extra, lam_init, tile, norms):
    nslab = 4
    width = nslab * LANES
    nq = S // tile
    pairs = [(qi, ki) for qi in range(nq) for ki in range(qi + 1)]
    qt = jnp.asarray([pq for pq, _ in pairs], jnp.int32)
    kt = jnp.asarray([pk for _, pk in pairs], jnp.int32)
    T = B * S
    fixed_ref = (norms[1], norms[3]) if norms is not None else None
    qmap = lambda b, p, qt, kt: (b * nq + qt[p], qcol // width)
    kmap = lambda b, p, qt, kt: (b * nq + kt[p], kcol // width)
    vmap = lambda b, p, qt, kt: (vrow // width, b * nq + kt[p])
    in_specs = [pl.BlockSpec((tile, width), qmap), pl.BlockSpec((tile, width), kmap),
                pl.BlockSpec((width, tile), vmap)]
    args = [qarr, karr, vt]
    if fixed_ref:
        nspec = pl.BlockSpec((1, 8, LANES), lambda b, p, qt, kt: (b, 0, 0))
        in_specs += [nspec, nspec]
        args += [norms[0], norms[2]]
    if mode == "diff":
        lam4, g = extra
        in_specs += [pl.BlockSpec((4, HEAD_DIM), lambda b, p, qt, kt: (0, 0)),
                     pl.BlockSpec((1, LANES), lambda b, p, qt, kt: (0, 0))]
        args += [lam4, g.reshape(1, LANES)]
    elif mode == "dsa":
        (mask_t,) = extra
        in_specs += [pl.BlockSpec((tile, tile), lambda b, p, qt, kt: (b * nq + kt[p], qt[p]))]
        args += [mask_t]
    else:
        ct, crep = extra
        in_specs += [pl.BlockSpec((1, FOX_HEADS, tile), lambda b, p, qt, kt: (b, 0, qt[p])),
                     pl.BlockSpec((tile, FOX_HEADS * LANES), lambda b, p, qt, kt: (b * nq + kt[p], 0))]
        args += [ct, crep]
    out_shape = jax.ShapeDtypeStruct((T, width), BF16)
    out_specs = pl.BlockSpec((tile, width), lambda b, p, qt, kt: (b * nq + qt[p], 0))
    if fixed_ref:
        out_shape = (out_shape, jax.ShapeDtypeStruct((2 * nslab, T), F32))
        out_specs = (out_specs, pl.BlockSpec((2 * nslab, tile), lambda b, p, qt, kt: (0, b * nq + qt[p])))
    return pl.pallas_call(
        functools.partial(_attn_body, mode, nslab, lam_init, fixed_ref),
        out_shape=out_shape,
        grid_spec=pltpu.PrefetchScalarGridSpec(
            num_scalar_prefetch=2,
            grid=(B, len(pairs)),
            in_specs=in_specs,
            out_specs=out_specs,
            scratch_shapes=[pltpu.VMEM((2 * nslab, tile), F32),
                            pltpu.VMEM((2 * nslab, LANES + 8, tile), F32)]),
        compiler_params=_cparams(("arbitrary", "arbitrary")),
        name="attn_" + mode + ("_fixed" if fixed_ref else "_online"),
    )(qt, kt, *args)


def _select_body(topk, iq_ref, ik_ref, iwt_ref, tri_ref, mask_ref, keys_ref, hi_ref, lo_ref):
    qi = pl.program_id(1)
    tq = iq_ref.shape[0]
    tk = tq
    nkb = qi + 1
    q0 = qi * tq
    lane = lax.broadcasted_iota(jnp.int32, (1, LANES), 1)
    qhs = []
    for h in range(IDX_HEADS):
        slab = iq_ref[:, (h // 2) * LANES:(h // 2 + 1) * LANES]
        qhs.append(jnp.where(_subhead_lanes(lane, h % 2), slab, jnp.zeros_like(slab)))
    iwt = iwt_ref[...]
    q_chunk = (q0 + lax.broadcasted_iota(jnp.int32, (tk, tq), 1)) >> 6
    k_iota = lax.broadcasted_iota(jnp.int32, (tk, tq), 0)

    mask_ref[...] = jnp.zeros_like(mask_ref)

    def score_block(j, carry):
        k0 = pl.multiple_of(j * tk, tk)
        ik = ik_ref[pl.ds(k0, tk), 0:LANES]
        score = jnp.zeros((tk, tq), F32)
        for h in range(IDX_HEADS):
            lg = lax.dot_general(ik, qhs[h], (((1,), (1,)), ((), ())), preferred_element_type=F32)
            score = score + iwt[SM_IW + h:SM_IW + h + 1, :] * jnp.maximum(lg, 0.0)
        score = jnp.where(score == 0.0, 0.0, score)
        bits = lax.bitcast_convert_type(score, jnp.int32)
        key = bits ^ ((bits >> 31) & jnp.int32(0x7FFFFFFF))
        vis = ((k0 + k_iota) >> 6) <= q_chunk
        key = jnp.where(vis, key, jnp.int32(INT_MIN))
        keys_ref[pl.ds(k0, tk), :] = key
        hi_ref[pl.ds(k0, tk), :] = (key >> 16).astype(jnp.int16)
        lo_ref[pl.ds(k0, tk), :] = ((key & jnp.int32(0xFFFF)) - HALF).astype(jnp.int16)
        return carry

    lax.fori_loop(0, nkb, score_block, 0)

    npair = (nkb + 1) >> 1

    @pl.when((nkb & 1) == 1)
    def _():
        k0 = pl.multiple_of(nkb * tk, tk)
        hi_ref[pl.ds(k0, tk), :] = jnp.full((tk, tq), -HALF, jnp.int16)
        lo_ref[pl.ds(k0, tk), :] = jnp.full((tk, tq), -HALF, jnp.int16)

    def count16(ref, pred):
        def body(j, acc):
            k0 = pl.multiple_of(j * (2 * tk), 2 * tk)
            hit = jnp.where(pred(ref[pl.ds(k0, 2 * tk), :]), jnp.int16(1), jnp.int16(0))
            for r in range(2 * tk // 16):
                acc = acc + hit[r * 16:(r + 1) * 16, :]
            return acc
        acc = lax.fori_loop(0, npair, body, jnp.zeros((16, tq), jnp.int16))
        return jnp.sum(acc.astype(jnp.int32), axis=0, keepdims=True)

    def search16(ref, base):
        def bit_step(i, lo):
            cand = lo + (jnp.int32(1) << (15 - i))
            c16 = cand.astype(jnp.int16)
            return jnp.where(base + count16(ref, lambda blk: blk >= c16) >= topk, cand, lo)
        return lax.fori_loop(0, 16, bit_step, jnp.full((1, tq), -HALF, jnp.int32))

    thr_hi = search16(hi_ref, 0)
    thr_hi16 = thr_hi.astype(jnp.int16)
    above = count16(hi_ref, lambda blk: blk > thr_hi16)

    def keep_low_of_threshold_rows(j, carry):
        k0 = pl.multiple_of(j * tk, tk)
        lo_ref[pl.ds(k0, tk), :] = jnp.where(hi_ref[pl.ds(k0, tk), :] == thr_hi16, lo_ref[pl.ds(k0, tk), :],
                                             jnp.int16(-HALF))
        return carry

    lax.fori_loop(0, nkb, keep_low_of_threshold_rows, 0)
    thr_lo = search16(lo_ref, above)
    thr_lo16 = thr_lo.astype(jnp.int16)
    above = above + count16(lo_ref, lambda blk: blk > thr_lo16)
    thr = (thr_hi << 16) + (thr_lo + HALF)
    thr = jnp.maximum(thr, jnp.int32(INT_MIN + 1))
    need = (topk - above).astype(F32)
    tri = tri_ref[...]

    def select_block(j, carry):
        k0 = pl.multiple_of(j * tk, tk)
        key = keys_ref[pl.ds(k0, tk), :]
        eq = key == thr
        incl = jnp.dot(tri, jnp.where(eq, 1.0, 0.0).astype(BF16), preferred_element_type=F32)
        sel = (key > thr) | (eq & ((carry + incl) <= need))
        mask_ref[pl.ds(k0, tk), :] = jnp.where(sel, 1, 0).astype(jnp.int8)
        return carry + incl[tk - 1:tk, :]

    lax.fori_loop(0, nkb, select_block, jnp.zeros((1, tq), F32))


def _dsa_select(qk, small_t, B, S, topk, tq=256):
    T = B * S
    nq = S // tq
    tri = jnp.tril(jnp.ones((tq, tq), F32)).astype(BF16)
    return pl.pallas_call(
        functools.partial(_select_body, topk),
        out_shape=jax.ShapeDtypeStruct((T, S), jnp.int8),
        grid=(B, nq),
        in_specs=[pl.BlockSpec((tq, 256), lambda b, i: (b * nq + i, QK_IQ // 256)),
                  pl.BlockSpec((S, 256), lambda b, i: (b, QK_IK // 256)),
                  pl.BlockSpec((8, tq), lambda b, i: (0, b * nq + i)),
                  pl.BlockSpec((tq, tq), lambda b, i: (0, 0))],
        out_specs=pl.BlockSpec((S, tq), lambda b, i: (b, i)),
        scratch_shapes=[pltpu.VMEM((S, tq), jnp.int32), pltpu.VMEM((S, tq), jnp.int16),
                        pltpu.VMEM((S, tq), jnp.int16)],
        compiler_params=_cparams(("arbitrary", "arbitrary")),
        name="dsa_select",
    )(qk, qk, small_t, tri)


def _merge_body(alpha, od_ref, os_ref, of_ref, gate_ref, h_ref, wbr_ref, wo_ref, g_ref, b_ref, wr_ref,
                h32_ref, h16_ref, aff_ref):
    D = h_ref.shape[1]
    y = None
    for i, o_ref in enumerate((od_ref, os_ref, of_ref)):
        br = jnp.dot(o_ref[...], wbr_ref[i], preferred_element_type=F32)
        t = gate_ref[:, i * D:(i + 1) * D].astype(F32) * br
        y = t if y is None else y + t
    m = jnp.dot(y.astype(BF16), wo_ref[...], preferred_element_type=F32)
    hn = _ln_rows(alpha * h_ref[...] + m, g_ref[...], b_ref[...])
    h32_ref[...] = hn
    h16 = hn.astype(BF16)
    h16_ref[...] = h16
    logits = lax.dot_general(wr_ref[...], h16, (((1,), (1,)), ((), ())), preferred_element_type=F32)
    aff_ref[...] = 1.0 / (1.0 + jnp.exp(-logits))


def _merge(alpha, od, os_, of, gates, h32, wbr, wo, g, b, wr_t, tm=512):
    T, D = h32.shape
    bw = od.shape[1]
    row = lambda i: (i, 0)
    fixed2 = lambda i: (0, 0)
    return pl.pallas_call(
        functools.partial(_merge_body, alpha),
        out_shape=(jax.ShapeDtypeStruct((T, D), F32), jax.ShapeDtypeStruct((T, D), BF16),
                   jax.ShapeDtypeStruct((N_EXPERTS, T), F32)),
        grid=(T // tm,),
        in_specs=[pl.BlockSpec((tm, bw), row), pl.BlockSpec((tm, bw), row), pl.BlockSpec((tm, bw), row),
                  pl.BlockSpec((tm, N_BRANCH * D), row), pl.BlockSpec((tm, D), row),
                  pl.BlockSpec((N_BRANCH, bw, D), lambda i: (0, 0, 0)), pl.BlockSpec((D, D), fixed2),
                  pl.BlockSpec((1, D), fixed2), pl.BlockSpec((1, D), fixed2),
                  pl.BlockSpec((N_EXPERTS, D), fixed2)],
        out_specs=(pl.BlockSpec((tm, D), row), pl.BlockSpec((tm, D), row),
                   pl.BlockSpec((N_EXPERTS, tm), lambda i: (0, i))),
        compiler_params=_cparams(("arbitrary",)),
        name="merge",
    )(od, os_, of, gates, h32, wbr, wo, g.reshape(1, D), b.reshape(1, D), wr_t)


def _route_body(aff_ref, bias_ref, e_ref, w_ref):
    aff = aff_ref[...]
    sel = aff + bias_ref[...]
    sub = lax.broadcasted_iota(jnp.int32, (EXPERTS_PER_GROUP, aff.shape[1]), 0)
    big = jnp.int32(EXPERTS_PER_GROUP)
    best = None
    for gidx in range(N_GROUPS):
        rs = slice(gidx * EXPERTS_PER_GROUP, (gidx + 1) * EXPERTS_PER_GROUP)
        s8 = sel[rs]
        a8 = aff[rs]
        m1 = jnp.max(s8, axis=0, keepdims=True)
        i1 = jnp.min(jnp.where(s8 == m1, sub, big), axis=0, keepdims=True)
        rest = jnp.where(sub == i1, -jnp.inf, s8)
        m2 = jnp.max(rest, axis=0, keepdims=True)
        i2 = jnp.min(jnp.where(rest == m2, sub, big), axis=0, keepdims=True)
        a1 = jnp.sum(jnp.where(sub == i1, a8, 0.0), axis=0, keepdims=True)
        a2 = jnp.sum(jnp.where(sub == i2, a8, 0.0), axis=0, keepdims=True)
        score = m1 + m2
        cand = (score, i1 + gidx * EXPERTS_PER_GROUP, i2 + gidx * EXPERTS_PER_GROUP, a1, a2)
        if best is None:
            best = cand
        else:
            take = score > best[0]
            best = tuple(jnp.where(take, c, o) for c, o in zip(cand, best))
    _, e1, e2, a1, a2 = best
    tot = a1 + a2
    e_ref[0:1, :] = e1
    e_ref[1:2, :] = e2
    w_ref[0:1, :] = a1 / tot
    w_ref[1:2, :] = a2 / tot


def _route(aff_t, b_router, tn=2048):
    E, T = aff_t.shape
    tn = min(tn, T)
    return pl.pallas_call(
        _route_body,
        out_shape=(jax.ShapeDtypeStruct((TOP_K, T), jnp.int32), jax.ShapeDtypeStruct((TOP_K, T), F32)),
        grid=(T // tn,),
        in_specs=[pl.BlockSpec((E, tn), lambda i: (0, i)), pl.BlockSpec((E, 1), lambda i: (0, 0))],
        out_specs=(pl.BlockSpec((TOP_K, tn), lambda i: (0, i)), pl.BlockSpec((TOP_K, tn), lambda i: (0, i))),
        compiler_params=_cparams(("arbitrary",)),
        name="route",
    )(aff_t, b_router.reshape(E, 1))


def _expert_body(be_ref, bc_ref, tok0_ref, tokn_ref, slot_ref, x_hbm, wg_ref, wu_ref, wd_ref, y_hbm,
                 xbuf, ybuf, gsem, ssem, wg16, wu16, wd16):
    i = pl.program_id(0)
    nb = pl.num_programs(0)
    cur = i & 1
    bm = xbuf.shape[1]
    n_real = y_hbm.shape[0] - 2 * bm

    def start_gather(tok_ref, slot):
        for r in range(bm):
            pltpu.make_async_copy(x_hbm.at[pl.ds(tok_ref[0, 0, r], 1), :], xbuf.at[slot, pl.ds(r, 1), :],
                                  gsem.at[slot]).start(priority=r % 2)

    def wait_gather(slot):
        pltpu.make_async_copy(x_hbm.at[pl.ds(0, bm), :], xbuf.at[slot], gsem.at[slot]).wait()

    def wait_scatter(slot):
        pltpu.make_async_copy(ybuf.at[slot], y_hbm.at[pl.ds(0, bm), :], ssem.at[slot]).wait()

    @pl.when(i == 0)
    def _():
        ybuf[...] = jnp.zeros_like(ybuf)
        for s in range(2):
            spare = pltpu.make_async_copy(ybuf.at[s], y_hbm.at[pl.ds(n_real + s * bm, bm), :], ssem.at[s])
            spare.start()
            spare.wait()
        start_gather(tok0_ref, 0)

    @pl.when(i + 1 < nb)
    def _():
        start_gather(tokn_ref, 1 - cur)

    wait_gather(cur)

    @pl.when(i >= 2)
    def _():
        wait_scatter(cur)

    @pl.when((i == 0) | (be_ref[i] != be_ref[jnp.maximum(i - 1, 0)]))
    def _():
        wg16[...] = wg_ref[0].astype(BF16)
        wu16[...] = wu_ref[0].astype(BF16)
        wd16[...] = wd_ref[0].astype(BF16)

    @pl.when(bc_ref[i] > 0)
    def _():
        x = xbuf[cur].astype(BF16)
        gate = jnp.dot(x, wg16[...], preferred_element_type=F32)
        up = jnp.dot(x, wu16[...], preferred_element_type=F32)
        hmid = (gate * (1.0 / (1.0 + jnp.exp(-gate)))) * up
        ybuf[cur] = jnp.dot(hmid.astype(BF16), wd16[...], preferred_element_type=F32)

    for r in range(bm):
        pltpu.make_async_copy(ybuf.at[cur, pl.ds(r, 1), :], y_hbm.at[pl.ds(slot_ref[0, 0, r], 1), :],
                              ssem.at[cur]).start(priority=r % 2)

    @pl.when(i == nb - 1)
    def _():
        wait_scatter(cur)

        @pl.when(nb > 1)
        def _():
            wait_scatter(1 - cur)


def _experts(h32, blk_e, blk_cnt, tok_pad, slot_pad, wg, wu, wd, bm):
    T, D = h32.shape
    nb = blk_e.shape[0]
    de = wg.shape[2]
    smem_blk = lambda f: pl.BlockSpec((1, 1, bm), f, memory_space=pltpu.SMEM)
    return pl.pallas_call(
        _expert_body,
        out_shape=jax.ShapeDtypeStruct((TOP_K * T + 2 * bm, D), F32),
        grid_spec=pltpu.PrefetchScalarGridSpec(
            num_scalar_prefetch=2,
            grid=(nb,),
            in_specs=[smem_blk(lambda i, be, bc: (0, 0, 0)),
                      smem_blk(lambda i, be, bc: (jnp.minimum(i + 1, nb - 1), 0, 0)),
                      smem_blk(lambda i, be, bc: (i, 0, 0)),
                      pl.BlockSpec(memory_space=pl.ANY),
                      pl.BlockSpec((1, D, de), lambda i, be, bc: (be[i], 0, 0)),
                      pl.BlockSpec((1, D, de), lambda i, be, bc: (be[i], 0, 0)),
                      pl.BlockSpec((1, de, D), lambda i, be, bc: (be[i], 0, 0))],
            out_specs=pl.BlockSpec(memory_space=pl.ANY),
            scratch_shapes=[pltpu.VMEM((2, bm, D), F32), pltpu.VMEM((2, bm, D), F32),
                            pltpu.SemaphoreType.DMA((2,)), pltpu.SemaphoreType.DMA((2,)),
                            pltpu.VMEM((D, de), BF16), pltpu.VMEM((D, de), BF16), pltpu.VMEM((de, D), BF16)]),
        compiler_params=_cparams(("arbitrary",)),
        name="experts",
    )(blk_e, blk_cnt, tok_pad, tok_pad, slot_pad, h32, wg, wu, wd)


def _dispatch_tables(e_t, T, bm):
    A = TOP_K * T
    flat_e = e_t.T.reshape(A)
    order = jnp.argsort(flat_e, stable=True).astype(jnp.int32)
    counts = jnp.sum(flat_e[None, :] == jnp.arange(N_EXPERTS, dtype=jnp.int32)[:, None], axis=1).astype(jnp.int32)
    start = jnp.cumsum(counts) - counts
    nblk_e = (counts + bm - 1) // bm
    blk_end = jnp.cumsum(nblk_e)
    nb = A // bm + N_EXPERTS
    bidx = jnp.arange(nb, dtype=jnp.int32)
    blk_e = jnp.minimum(jnp.sum(bidx[:, None] >= blk_end[None, :], axis=1), N_EXPERTS - 1).astype(jnp.int32)
    off = (bidx - (blk_end - nblk_e)[blk_e]) * bm
    blk_cnt = jnp.where(bidx < blk_end[-1], jnp.clip(counts[blk_e] - off, 0, bm), 0).astype(jnp.int32)
    r = jnp.arange(bm, dtype=jnp.int32)[None, :]
    real = r < blk_cnt[:, None]
    a = order[jnp.clip((start[blk_e] + off)[:, None] + r, 0, A - 1)]
    tok_pad = jnp.where(real, a // TOP_K, 0)
    slot_pad = jnp.where(real, (a % TOP_K) * T + a // TOP_K, A + (bidx % 2)[:, None] * bm + r)
    return blk_e, blk_cnt, tok_pad.reshape(nb, 1, bm), slot_pad.reshape(nb, 1, bm)


def _combine_body(alpha, h_ref, y0_ref, y1_ref, w_ref, g_ref, b_ref, o32_ref, o16_ref):
    w = w_ref[...]
    f = w[:, 0:1] * y0_ref[...] + w[:, 1:2] * y1_ref[...]
    hn = _ln_rows(alpha * h_ref[...] + f, g_ref[...], b_ref[...])
    o32_ref[...] = hn
    o16_ref[...] = hn.astype(BF16)


def _combine(alpha, h32, y2, w_rows, g, b, tm=512):
    T, D = h32.shape
    nt = T // tm
    row = lambda i: (i, 0)
    fixed = lambda i: (0, 0)
    return pl.pallas_call(
        functools.partial(_combine_body, alpha),
        out_shape=(jax.ShapeDtypeStruct((T, D), F32), jax.ShapeDtypeStruct((T, D), BF16)),
        grid=(nt,),
        in_specs=[pl.BlockSpec((tm, D), row), pl.BlockSpec((tm, D), row),
                  pl.BlockSpec((tm, D), lambda i: (nt + i, 0)),
                  pl.BlockSpec((tm, TOP_K), row), pl.BlockSpec((1, D), fixed), pl.BlockSpec((1, D), fixed)],
        out_specs=(pl.BlockSpec((tm, D), row), pl.BlockSpec((tm, D), row)),
        compiler_params=_cparams(("arbitrary",)),
        name="combine",
    )(h32, y2, y2, w_rows, g.reshape(1, D), b.reshape(1, D))


def _pair_halves_columns(w):
    d = w.shape[0]
    w5 = w.reshape(d, -1, 2, 2, HEAD_DIM // 2)
    return jnp.transpose(w5, (0, 1, 3, 2, 4)).reshape(w.shape)


def _prep_in_weights(w_in_l):
    qk_w = DIFF_HEADS * 2 * HEAD_DIM
    sizes = (qk_w, qk_w, DIFF_HEADS * 2 * HEAD_DIM, 512, 512, 512, IDX_HEADS * HEAD_DIM, HEAD_DIM, IDX_HEADS,
             512, 512, 512, FOX_HEADS, N_BRANCH * w_in_l.shape[0])
    offs = [0]
    for s in sizes:
        offs.append(offs[-1] + s)
    dq, dk, dv, sq, sk, sv, iq, ik, iw, fq, fk, fv, ff, g = (w_in_l[:, offs[i]:offs[i + 1]] for i in range(len(sizes)))
    scale = HEAD_DIM ** -0.5
    idx_scale = (IDX_HEADS * HEAD_DIM) ** -0.5
    w_rope = _pair_halves_columns(jnp.concatenate([dq * scale, dk, sq * scale, sk, iq, ik, ik, ik, ik], axis=1))
    w_plain = _pair_halves_columns(jnp.concatenate([fq * scale, fk], axis=1))
    w_vt = jnp.concatenate([dv, sv, fv], axis=1).T
    pad = jnp.zeros((w_in_l.shape[0], LANES - IDX_HEADS - FOX_HEADS), w_in_l.dtype)
    w_small = jnp.concatenate([iw * idx_scale, ff, pad], axis=1)
    return (w_rope.astype(BF16), w_plain.astype(BF16), w_vt.astype(BF16), g.astype(BF16), w_small.astype(BF16))


def _rope_tables(positions):
    inv = ROPE_THETA ** (-jnp.arange(0, HEAD_DIM, 2, dtype=F32) / HEAD_DIM)
    ang = positions.astype(F32).reshape(-1, 1) * inv
    c, s = jnp.cos(ang), jnp.sin(ang)
    return jnp.tile(c, (1, 4)), jnp.concatenate([-s, -s, s, s], axis=1)


def kernel(x, positions, ln_in_g, ln_in_b, w_in, b_forget, diff_lam, diff_norm_g, w_br, w_o, ln1_g, ln1_b,
           w_router, b_router, w_gate, w_up, w_down, ln2_g, ln2_b):
    B, S, D = x.shape
    T = B * S
    depth = w_in.shape[0]
    alpha = (2 * depth) ** 0.25
    topk = min(DSA_TOPK_MAX, S // 4)
    bm = 256
    cc, ss = _rope_tables(positions)
    wr_t = w_router.T.astype(BF16)
    h32, h16 = _layer_norm(x.reshape(T, D), ln_in_g, ln_in_b)
    for l in range(depth):
        lam_init = 0.8 - 0.6 * math.exp(-0.3 * l)
        w_rope, w_plain, w_vt, w_g, w_small = _prep_in_weights(w_in[l])
        qk, n_qk = _project(h16, w_rope, "rope", BF16, B=B, tables=(cc, ss), n_norm=4)
        fqk, n_f = _project(h16, w_plain, "plain", BF16, B=B, n_norm=2)
        vt = _project(h16, w_vt, "transposed", BF16)
        gates = _project(h16, w_g, "sigmoid", BF16)
        small = _project(h16, w_small, "f32", F32)

        o_diff = _attention("diff", qk, QK_DQ, qk, QK_DK, vt, VT_DV, B, S,
                            (diff_lam[l], diff_norm_g[l]), (n_qk, 0, n_qk, 1), lam_init=lam_init)
        small_t = _project(h16, w_small.T, "transposed", F32)
        mask_t = _dsa_select(qk, small_t, B, S, topk)
        o_dsa = _attention("dsa", qk, QK_SQ, qk, QK_SK, vt, VT_SV, B, S, (mask_t,), (n_qk, 2, n_qk, 3))
        bias_row = jnp.zeros((1, LANES), F32).at[0, SM_FF:SM_FF + FOX_HEADS].set(b_forget[l])
        c, crep = _forget_cumsum(small, bias_row, B, S)
        ct = jnp.transpose(c[:, SM_FF:SM_FF + FOX_HEADS].reshape(B, S, FOX_HEADS), (0, 2, 1))
        o_fox = _attention("fox", fqk, FQK_FQ, fqk, FQK_FK, vt, VT_FV, B, S, (ct, crep), (n_f, 0, n_f, 1))

        h32, h16, aff_t = _merge(alpha, o_diff, o_dsa, o_fox, gates, h32, w_br[l].astype(BF16),
                                 w_o[l].astype(BF16), ln1_g[l], ln1_b[l], wr_t)
        e_t, w_t = _route(aff_t, b_router)
        blk_e, blk_cnt, tok_pad, slot_pad = _dispatch_tables(e_t, T, bm)
        y2 = _experts(h32, blk_e + l * N_EXPERTS, blk_cnt, tok_pad, slot_pad,
                      w_gate.reshape((-1,) + w_gate.shape[2:]), w_up.reshape((-1,) + w_up.shape[2:]),
                      w_down.reshape((-1,) + w_down.shape[2:]), bm)
        h32, h16 = _combine(alpha, h32, y2, w_t.T, ln2_g[l], ln2_b[l])
    return h32.reshape(B, S, D)
```

```python
import functools
import math

import jax
import jax.numpy as jnp
from jax import lax
from jax.experimental import pallas as pl
from jax.experimental.pallas import tpu as pltpu

F32 = jnp.float32
BF16 = jnp.bfloat16

CHUNK = 64
HEAD_DIM = 64
ROPE_THETA = 10000.0
LN_EPS = 1e-5
DIFF_HEADS = 4
DSA_HEADS = 8
IDX_HEADS = 4
DSA_TOPK_MAX = 256
FOX_HEADS = 8
N_BRANCH = 3
BRANCH_WIDTH = 512
N_EXPERTS = 32
N_GROUPS = 4
EXPERTS_PER_GROUP = N_EXPERTS // N_GROUPS
TOP_K = 2

LANES = 128
NEG = -1e30
INT_MIN = -2 ** 31
HALF = 1 << 15

QK_DQ, QK_DK, QK_SQ, QK_SK, QK_IQ, QK_IK = 0, 512, 1024, 1536, 2048, 2304
QK_WIDTH = 2560
FQK_FQ, FQK_FK = 0, 512
VT_DV, VT_SV, VT_FV = 0, 512, 1024
SM_IW, SM_FF = 0, 4
VMEM_LIMIT = 56 * 1024 * 1024


def _cparams(sem):
    return pltpu.CompilerParams(dimension_semantics=sem, vmem_limit_bytes=VMEM_LIMIT)


def _ln_rows(x, g, b):
    mu = jnp.mean(x, axis=-1, keepdims=True)
    xc = x - mu
    var = jnp.mean(xc * xc, axis=-1, keepdims=True)
    return xc * lax.rsqrt(var + LN_EPS) * g + b


def _ln_body(x_ref, g_ref, b_ref, o32_ref, o16_ref):
    y = _ln_rows(x_ref[...], g_ref[...], b_ref[...])
    o32_ref[...] = y
    o16_ref[...] = y.astype(BF16)


def _layer_norm(x, g, b, tm=512):
    T, D = x.shape
    return pl.pallas_call(
        _ln_body,
        out_shape=(jax.ShapeDtypeStruct((T, D), F32), jax.ShapeDtypeStruct((T, D), BF16)),
        grid=(T // tm,),
        in_specs=[pl.BlockSpec((tm, D), lambda i: (i, 0)),
                  pl.BlockSpec((1, D), lambda i: (0, 0)),
                  pl.BlockSpec((1, D), lambda i: (0, 0))],
        out_specs=(pl.BlockSpec((tm, D), lambda i: (i, 0)), pl.BlockSpec((tm, D), lambda i: (i, 0))),
        compiler_params=_cparams(("arbitrary",)),
        name="layer_norm",
    )(x, g.reshape(1, D), b.reshape(1, D))


def _proj_body(mode, width, n_norm, steps_per_batch, h_ref, w_ref, *rest):
    if mode == "rope":
        cc_ref, ss_ref, e_ref, o_ref, nmax_ref = rest
    elif mode == "plain":
        e_ref, o_ref, nmax_ref = rest
    else:
        (o_ref,) = rest
    h = h_ref[...]
    tn = 512 if width % 512 == 0 else width
    if n_norm:
        @pl.when(pl.program_id(0) % steps_per_batch == 0)
        def _():
            nmax_ref[...] = jnp.zeros_like(nmax_ref)
    for j in range(width // tn):
        cs = slice(j * tn, (j + 1) * tn)
        if mode == "transposed":
            o_ref[cs, :] = lax.dot_general(w_ref[cs, :], h, (((1,), (1,)), ((), ())),
                                           preferred_element_type=F32).astype(o_ref.dtype)
            continue
        z = jnp.dot(h, w_ref[:, cs], preferred_element_type=F32)
        if j < n_norm:
            n2 = jnp.dot((z * z).astype(BF16), e_ref[...], preferred_element_type=F32)
            nmax_ref[0, j:j + 1, :] = jnp.maximum(nmax_ref[0, j:j + 1, :], jnp.max(n2, axis=0, keepdims=True))
        if mode == "rope":
            cc = cc_ref[...]
            ss = ss_ref[...]
            for i in range(tn // LANES):
                zs = z[:, i * LANES:(i + 1) * LANES]
                o_ref[:, j * tn + i * LANES:j * tn + (i + 1) * LANES] = (
                    zs * cc + pltpu.roll(zs, LANES // 2, 1) * ss).astype(o_ref.dtype)
        elif mode == "sigmoid":
            o_ref[:, cs] = (1.0 / (1.0 + jnp.exp(-z))).astype(o_ref.dtype)
        else:
            o_ref[:, cs] = z.astype(o_ref.dtype)


def _subhead_indicator():
    col = jnp.arange(4 * LANES)
    g = 2 * (col // LANES) + ((col % LANES) // (HEAD_DIM // 2)) % 2
    return (g[:, None] == jnp.arange(LANES)[None, :]).astype(BF16)


def _project(h16, w, mode, out_dtype, B=1, tables=None, n_norm=0, tm=512):
    T, D = h16.shape
    transposed = mode == "transposed"
    width = w.shape[0] if transposed else w.shape[1]
    steps_per_batch = T // B // tm
    row = lambda i: (i, 0)
    fixed = lambda i: (0, 0)
    in_specs = [pl.BlockSpec((tm, D), row), pl.BlockSpec(w.shape, fixed)]
    args = [h16, w]
    if mode == "rope":
        cc, ss = tables
        in_specs += [pl.BlockSpec((tm, LANES), row), pl.BlockSpec((tm, LANES), row)]
        args += [cc, ss]
    out_shape = jax.ShapeDtypeStruct((width, T) if transposed else (T, width), out_dtype)
    out_specs = pl.BlockSpec((width, tm), lambda i: (0, i)) if transposed else pl.BlockSpec((tm, width), row)
    if n_norm:
        in_specs += [pl.BlockSpec((4 * LANES, LANES), fixed)]
        args += [_subhead_indicator()]
        out_shape = (out_shape, jax.ShapeDtypeStruct((B, 8, LANES), F32))
        out_specs = (out_specs, pl.BlockSpec((1, 8, LANES), lambda i: (i // steps_per_batch, 0, 0)))
    return pl.pallas_call(
        functools.partial(_proj_body, mode, width, n_norm, steps_per_batch),
        out_shape=out_shape,
        grid=(T // tm,),
        in_specs=in_specs,
        out_specs=out_specs,
        compiler_params=_cparams(("arbitrary",)),
        name="proj_" + mode,
    )(*args)


def _cumsum_body(h_ref, w_ref, bias_ref, tri_ref, c_ref, crep_ref, carry_ref):
    @pl.when(pl.program_id(1) == 0)
    def _():
        carry_ref[...] = jnp.zeros_like(carry_ref)

    x = jnp.dot(h_ref[...], w_ref[...], preferred_element_type=F32) + bias_ref[...]
    lf = jnp.minimum(x, 0.0) - jnp.log1p(jnp.exp(-jnp.abs(x)))
    hi = lf.astype(BF16)
    r1 = lf - hi.astype(F32)
    mid = r1.astype(BF16)
    lo = (r1 - mid.astype(F32)).astype(BF16)
    tri = tri_ref[...]
    c = (jnp.dot(tri, hi, preferred_element_type=F32) + jnp.dot(tri, mid, preferred_element_type=F32)
         + jnp.dot(tri, lo, preferred_element_type=F32)) + carry_ref[...]
    c_ref[...] = c
    carry_ref[...] = c[-1:, :]
    for h in range(FOX_HEADS):
        crep_ref[:, h * LANES:(h + 1) * LANES] = jnp.broadcast_to(c[:, SM_FF + h:SM_FF + h + 1], (c.shape[0], LANES))


def _forget_cumsum(h16, w_small, bias_row, B, S, tb=512):
    T, D = h16.shape
    nb = S // tb
    tri = jnp.tril(jnp.ones((tb, tb), F32)).astype(BF16)
    return pl.pallas_call(
        _cumsum_body,
        out_shape=(jax.ShapeDtypeStruct((T, LANES), F32), jax.ShapeDtypeStruct((T, FOX_HEADS * LANES), F32)),
        grid=(B, nb),
        in_specs=[pl.BlockSpec((tb, D), lambda b, i: (b * nb + i, 0)),
                  pl.BlockSpec((D, LANES), lambda b, i: (0, 0)),
                  pl.BlockSpec((1, LANES), lambda b, i: (0, 0)),
                  pl.BlockSpec((tb, tb), lambda b, i: (0, 0))],
        out_specs=(pl.BlockSpec((tb, LANES), lambda b, i: (b * nb + i, 0)),
                   pl.BlockSpec((tb, FOX_HEADS * LANES), lambda b, i: (b * nb + i, 0))),
        scratch_shapes=[pltpu.VMEM((1, LANES), F32)],
        compiler_params=_cparams(("arbitrary", "arbitrary")),
        name="forget_cumsum",
    )(h16, w_small, bias_row, tri)


def _attn_body(mode, nslab, lam_init, fixed_ref, qt_ref, kt_ref, q_ref, k_ref, vt_ref, *rest):
    if fixed_ref:
        qn_ref, kn_ref = rest[:2]
        rest = rest[2:]
    if mode == "diff":
        lam_ref, g_ref = rest[:2]
    elif mode == "dsa":
        (mask_ref,) = rest[:1]
    else:
        cq_ref, ck_ref = rest[:2]
    if fixed_ref:
        o_ref, den_ref, m_sc, acc_sc = rest[-4:]
    else:
        o_ref, m_sc, acc_sc = rest[-3:]
    p = pl.program_id(1)
    qi = qt_ref[p]
    ki = kt_ref[p]
    tq = q_ref.shape[0]
    tk = k_ref.shape[0]

    @pl.when(ki == 0)
    def _():
        if fixed_ref:
            q_row, k_row = fixed_ref
            bound = 1.01 * jnp.sqrt(qn_ref[0, q_row:q_row + 1, :] * kn_ref[0, k_row:k_row + 1, :])
            sub_lane = lax.broadcasted_iota(jnp.int32, (1, LANES), 1)
            for idx in range(2 * nslab):
                b_idx = jnp.sum(jnp.where(sub_lane == idx, bound, 0.0), axis=1, keepdims=True)
                m_sc[idx:idx + 1, :] = jnp.broadcast_to(b_idx, (1, tq))
        else:
            m_sc[...] = jnp.full_like(m_sc, NEG)
        acc_sc[...] = jnp.zeros_like(acc_sc)

    lane = lax.broadcasted_iota(jnp.int32, (1, LANES), 1)

    def tile_update(diag):
        if mode == "dsa":
            keep_all = mask_ref[...] != 0
        elif diag:
            krow = lax.broadcasted_iota(jnp.int32, (tk, tq), 0)
            qcol = lax.broadcasted_iota(jnp.int32, (tk, tq), 1)
            if mode == "diff":
                keep_all = (krow >> 6) <= (qcol >> 6)
            else:
                keep_all = krow <= qcol
        def scores(idx):
            ls = slice((idx // 2) * LANES, (idx // 2 + 1) * LANES)
            q = q_ref[:, ls]
            qm = jnp.where(_subhead_lanes(lane, idx % 2), q, jnp.zeros_like(q))
            return lax.dot_general(k_ref[:, ls], qm, (((1,), (1,)), ((), ())), preferred_element_type=F32)

        ones_rows = jnp.ones((8, tk), BF16)
        ahead = [scores(i) for i in range(LOOKAHEAD)]
        for idx in range(2 * nslab):
            s = ahead.pop(0)
            if idx + LOOKAHEAD < 2 * nslab:
                ahead.append(scores(idx + LOOKAHEAD))
            vt = jnp.concatenate([vt_ref[(idx // 2) * LANES:(idx // 2 + 1) * LANES, :], ones_rows], axis=0)
            if mode == "fox":
                ck = ck_ref[:, idx * LANES:(idx + 1) * LANES]
                s = s - jnp.concatenate([ck] * (tq // LANES), axis=1)
            m_old = m_sc[idx:idx + 1, :]
            if fixed_ref:
                ref = (m_old - cq_ref[0, idx:idx + 1, :]) if mode == "fox" else m_old
                pr = jnp.exp(s - ref).astype(BF16)
                if mode == "dsa" or diag:
                    pr = jnp.where(keep_all, pr, jnp.zeros_like(pr))
                acc_sc[idx] = acc_sc[idx] + jnp.dot(vt, pr, preferred_element_type=F32)
                continue
            if mode == "dsa" or diag:
                s = jnp.where(keep_all, s, NEG)
            mx = jnp.max(s, axis=0, keepdims=True)
            if mode == "fox":
                cq = cq_ref[0, idx:idx + 1, :]
                m_new = jnp.maximum(m_old, mx + cq)
                shift = cq - m_new
            else:
                m_new = jnp.maximum(m_old, mx)
                shift = -m_new
            alpha = jnp.exp(m_old - m_new)
            pr = jnp.exp(s + shift).astype(BF16)
            acc_sc[idx] = alpha * acc_sc[idx] + jnp.dot(vt, pr, preferred_element_type=F32)
            m_sc[idx:idx + 1, :] = m_new

    if mode == "dsa":
        tile_update(False)
    else:
        @pl.when(ki == qi)
        def _():
            tile_update(True)

        @pl.when(ki != qi)
        def _():
            tile_update(False)

    @pl.when(ki == qi)
    def _():
        if mode == "diff":
            lm = lam_ref[...]
            lam = (jnp.exp(jnp.sum(lm[0:1] * lm[1:2], keepdims=True))
                   - jnp.exp(jnp.sum(lm[2:3] * lm[3:4], keepdims=True)) + lam_init)
        feat = lax.broadcasted_iota(jnp.int32, (LANES, 1), 0)
        if fixed_ref:
            for idx in range(2 * nslab):
                den_ref[idx:idx + 1, :] = acc_sc[idx, LANES:LANES + 1, :]
        for sl in range(nslab):
            o0 = acc_sc[2 * sl, 0:LANES, :] / acc_sc[2 * sl, LANES:LANES + 1, :]
            o1 = acc_sc[2 * sl + 1, 0:LANES, :] / acc_sc[2 * sl + 1, LANES:LANES + 1, :]
            if mode == "diff":
                o = (o0 - lam * o1).T
                o = o * lax.rsqrt(jnp.mean(o * o, axis=-1, keepdims=True) + LN_EPS)
                o = (o * g_ref[...]) * (1.0 - lam_init)
            else:
                o = jnp.where(feat < HEAD_DIM, o0, o1).T
            o_ref[:, sl * LANES:(sl + 1) * LANES] = o.astype(o_ref.dtype)


def _subhead_lanes(lane, sub):
    return ((lane >> 5) & 1) == sub


LOOKAHEAD = 1
DEN_FLOOR = 1e-30


def _attention(mode, qarr, qcol, karr, kcol, vt, vrow, B, S, extra, norms, lam_init=0.0, tile=512):
    o_fast, den = _attention_call(mode, qarr, qcol, karr, kcol, vt, vrow, B, S, extra, lam_init, tile, norms)
    bad = jnp.logical_not(jnp.all(den > DEN_FLOOR))
    return lax.cond(
        bad,
        lambda: _attention_call(mode, qarr, qcol, karr, kcol, vt, vrow, B, S, extra, lam_init, tile, None),
        lambda: o_fast)


def _attention_call(mode, qarr, qcol, karr, kcol, vt, vrow, B, S, extra, lam_init, tile, norms):
    nslab = 4
    width = nslab * LANES
    nq = S // tile
    pairs = [(qi, ki) for qi in range(nq) for ki in range(qi + 1)]
    qt = jnp.asarray([pq for pq, _ in pairs], jnp.int32)
    kt = jnp.asarray([pk for _, pk in pairs], jnp.int32)
    T = B * S
    fixed_ref = (norms[1], norms[3]) if norms is not None else None
    qmap = lambda b, p, qt, kt: (b * nq + qt[p], qcol // width)
    kmap = lambda b, p, qt, kt: (b * nq + kt[p], kcol // width)
    vmap = lambda b, p, qt, kt: (vrow // width, b * nq + kt[p])
    in_specs = [pl.BlockSpec((tile, width), qmap), pl.BlockSpec((tile, width), kmap),
                pl.BlockSpec((width, tile), vmap)]
    args = [qarr, karr, vt]
    if fixed_ref:
        nspec = pl.BlockSpec((1, 8, LANES), lambda b, p, qt, kt: (b, 0, 0))
        in_specs += [nspec, nspec]
        args += [norms[0], norms[2]]
    if mode == "diff":
        lam4, g = extra
        in_specs += [pl.BlockSpec((4, HEAD_DIM), lambda b, p, qt, kt: (0, 0)),
                     pl.BlockSpec((1, LANES), lambda b, p, qt, kt: (0, 0))]
        args += [lam4, g.reshape(1, LANES)]
    elif mode == "dsa":
        (mask_t,) = extra
        in_specs += [pl.BlockSpec((tile, tile), lambda b, p, qt, kt: (b * nq + kt[p], qt[p]))]
        args += [mask_t]
    else:
        ct, crep = extra
        in_specs += [pl.BlockSpec((1, FOX_HEADS, tile), lambda b, p, qt, kt: (b, 0, qt[p])),
                     pl.BlockSpec((tile, FOX_HEADS * LANES), lambda b, p, qt, kt: (b * nq + kt[p], 0))]
        args += [ct, crep]
    out_shape = jax.ShapeDtypeStruct((T, width), BF16)
    out_specs = pl.BlockSpec((tile, width), lambda b, p, qt, kt: (b * nq + qt[p], 0))
    if fixed_ref:
        out_shape = (out_shape, jax.ShapeDtypeStruct((2 * nslab, T), F32))
        out_specs = (out_specs, pl.BlockSpec((2 * nslab, tile), lambda b, p, qt, kt: (0, b * nq + qt[p])))
    return pl.pallas_call(
        functools.partial(_attn_body, mode, nslab, lam_init, fixed_ref),
        out_shape=out_shape,
        grid_spec=pltpu.PrefetchScalarGridSpec(
            num_scalar_prefetch=2,
            grid=(B, len(pairs)),
            in_specs=in_specs,
            out_specs=out_specs,
            scratch_shapes=[pltpu.VMEM((2 * nslab, tile), F32),
                            pltpu.VMEM((2 * nslab, LANES + 8, tile), F32)]),
        compiler_params=_cparams(("arbitrary", "arbitrary")),
        name="attn_" + mode + ("_fixed" if fixed_ref else "_online"),
    )(qt, kt, *args)


def _select_body(topk, iq_ref, ik_ref, h_ref, wiw_ref, tri_ref, mask_ref, keys_ref, hi_ref, lo_ref):
    qi = pl.program_id(1)
    tq = iq_ref.shape[0]
    tk = tq
    nkb = qi + 1
    q0 = qi * tq
    lane = lax.broadcasted_iota(jnp.int32, (1, LANES), 1)
    qhs = []
    for h in range(IDX_HEADS):
        slab = iq_ref[:, (h // 2) * LANES:(h // 2 + 1) * LANES]
        qhs.append(jnp.where(_subhead_lanes(lane, h % 2), slab, jnp.zeros_like(slab)))
    iwt = lax.dot_general(wiw_ref[...], h_ref[...], (((1,), (1,)), ((), ())), preferred_element_type=F32)
    q_chunk = (q0 + lax.broadcasted_iota(jnp.int32, (tk, tq), 1)) >> 6
    k_iota = lax.broadcasted_iota(jnp.int32, (tk, tq), 0)

    mask_ref[...] = jnp.zeros_like(mask_ref)

    def score_block(j, carry):
        k0 = pl.multiple_of(j * tk, tk)
        ik = ik_ref[pl.ds(k0, tk), 0:LANES]
        score = jnp.zeros((tk, tq), F32)
        for h in range(IDX_HEADS):
            lg = lax.dot_general(ik, qhs[h], (((1,), (1,)), ((), ())), preferred_element_type=F32)
            score = score + iwt[SM_IW + h:SM_IW + h + 1, :] * jnp.maximum(lg, 0.0)
        score = jnp.where(score == 0.0, 0.0, score)
        bits = lax.bitcast_convert_type(score, jnp.int32)
        key = bits ^ ((bits >> 31) & jnp.int32(0x7FFFFFFF))
        vis = ((k0 + k_iota) >> 6) <= q_chunk
        key = jnp.where(vis, key, jnp.int32(INT_MIN))
        keys_ref[pl.ds(k0, tk), :] = key
        hi_ref[pl.ds(k0, tk), :] = (key >> 16).astype(jnp.int16)
        lo_ref[pl.ds(k0, tk), :] = ((key & jnp.int32(0xFFFF)) - HALF).astype(jnp.int16)
        return carry

    lax.fori_loop(0, nkb, score_block, 0)

    npair = (nkb + 1) >> 1

    @pl.when((nkb & 1) == 1)
    def _():
        k0 = pl.multiple_of(nkb * tk, tk)
        hi_ref[pl.ds(k0, tk), :] = jnp.full((tk, tq), -HALF, jnp.int16)
        lo_ref[pl.ds(k0, tk), :] = jnp.full((tk, tq), -HALF, jnp.int16)

    def count16(ref, pred):
        def body(j, acc):
            k0 = pl.multiple_of(j * (2 * tk), 2 * tk)
            hit = jnp.where(pred(ref[pl.ds(k0, 2 * tk), :]), jnp.int16(1), jnp.int16(0))
            for r in range(2 * tk // 16):
                acc = acc + hit[r * 16:(r + 1) * 16, :]
            return acc
        acc = lax.fori_loop(0, npair, body, jnp.zeros((16, tq), jnp.int16))
        return jnp.sum(acc.astype(jnp.int32), axis=0, keepdims=True)

    def search16(ref, base):
        def bit_step(i, lo):
            cand = lo + (jnp.int32(1) << (15 - i))
            c16 = cand.astype(jnp.int16)
            return jnp.where(base + count16(ref, lambda blk: blk >= c16) >= topk, cand, lo)
        return lax.fori_loop(0, 16, bit_step, jnp.full((1, tq), -HALF, jnp.int32))

    thr_hi = search16(hi_ref, 0)
    thr_hi16 = thr_hi.astype(jnp.int16)
    above = count16(hi_ref, lambda blk: blk > thr_hi16)

    def keep_low_of_threshold_rows(j, carry):
        k0 = pl.multiple_of(j * tk, tk)
        lo_ref[pl.ds(k0, tk), :] = jnp.where(hi_ref[pl.ds(k0, tk), :] == thr_hi16, lo_ref[pl.ds(k0, tk), :],
                                             jnp.int16(-HALF))
        return carry

    lax.fori_loop(0, nkb, keep_low_of_threshold_rows, 0)
    thr_lo = search16(lo_ref, above)
    thr_lo16 = thr_lo.astype(jnp.int16)
    above = above + count16(lo_ref, lambda blk: blk > thr_lo16)
    thr = (thr_hi << 16) + (thr_lo + HALF)
    thr = jnp.maximum(thr, jnp.int32(INT_MIN + 1))
    need = (topk - above).astype(F32)
    tri = tri_ref[...]

    def select_block(j, carry):
        k0 = pl.multiple_of(j * tk, tk)
        key = keys_ref[pl.ds(k0, tk), :]
        eq = key == thr
        incl = jnp.dot(tri, jnp.where(eq, 1.0, 0.0).astype(BF16), preferred_element_type=F32)
        sel = (key > thr) | (eq & ((carry + incl) <= need))
        mask_ref[pl.ds(k0, tk), :] = jnp.where(sel, 1, 0).astype(jnp.int8)
        return carry + incl[tk - 1:tk, :]

    lax.fori_loop(0, nkb, select_block, jnp.zeros((1, tq), F32))


def _dsa_select(qk, h16, w_iw_t, B, S, topk, tq=256):
    T = B * S
    D = h16.shape[1]
    nq = S // tq
    tri = jnp.tril(jnp.ones((tq, tq), F32)).astype(BF16)
    return pl.pallas_call(
        functools.partial(_select_body, topk),
        out_shape=jax.ShapeDtypeStruct((T, S), jnp.int8),
        grid=(B, nq),
        in_specs=[pl.BlockSpec((tq, 256), lambda b, i: (b * nq + i, QK_IQ // 256)),
                  pl.BlockSpec((S, 256), lambda b, i: (b, QK_IK // 256)),
                  pl.BlockSpec((tq, D), lambda b, i: (b * nq + i, 0)),
                  pl.BlockSpec((8, D), lambda b, i: (0, 0)),
                  pl.BlockSpec((tq, tq), lambda b, i: (0, 0))],
        out_specs=pl.BlockSpec((S, tq), lambda b, i: (b, i)),
        scratch_shapes=[pltpu.VMEM((S, tq), jnp.int32), pltpu.VMEM((S, tq), jnp.int16),
                        pltpu.VMEM((S, tq), jnp.int16)],
        compiler_params=_cparams(("arbitrary", "arbitrary")),
        name="dsa_select",
    )(qk, qk, h16, w_iw_t, tri)


def _merge_body(alpha, od_ref, os_ref, of_ref, gate_ref, h_ref, wbr_ref, wo_ref, g_ref, b_ref, wr_ref,
                h32_ref, h16_ref, aff_ref):
    D = h_ref.shape[1]
    y = None
    for i, o_ref in enumerate((od_ref, os_ref, of_ref)):
        br = jnp.dot(o_ref[...], wbr_ref[i], preferred_element_type=F32)
        t = gate_ref[:, i * D:(i + 1) * D].astype(F32) * br
        y = t if y is None else y + t
    m = jnp.dot(y.astype(BF16), wo_ref[...], preferred_element_type=F32)
    hn = _ln_rows(alpha * h_ref[...] + m, g_ref[...], b_ref[...])
    h32_ref[...] = hn
    h16 = hn.astype(BF16)
    h16_ref[...] = h16
    logits = lax.dot_general(wr_ref[...], h16, (((1,), (1,)), ((), ())), preferred_element_type=F32)
    aff_ref[...] = 1.0 / (1.0 + jnp.exp(-logits))


def _merge(alpha, od, os_, of, gates, h32, wbr, wo, g, b, wr_t, tm=512):
    T, D = h32.shape
    bw = od.shape[1]
    row = lambda i: (i, 0)
    fixed2 = lambda i: (0, 0)
    return pl.pallas_call(
        functools.partial(_merge_body, alpha),
        out_shape=(jax.ShapeDtypeStruct((T, D), F32), jax.ShapeDtypeStruct((T, D), BF16),
                   jax.ShapeDtypeStruct((N_EXPERTS, T), F32)),
        grid=(T // tm,),
        in_specs=[pl.BlockSpec((tm, bw), row), pl.BlockSpec((tm, bw), row), pl.BlockSpec((tm, bw), row),
                  pl.BlockSpec((tm, N_BRANCH * D), row), pl.BlockSpec((tm, D), row),
                  pl.BlockSpec((N_BRANCH, bw, D), lambda i: (0, 0, 0)), pl.BlockSpec((D, D), fixed2),
                  pl.BlockSpec((1, D), fixed2), pl.BlockSpec((1, D), fixed2),
                  pl.BlockSpec((N_EXPERTS, D), fixed2)],
        out_specs=(pl.BlockSpec((tm, D), row), pl.BlockSpec((tm, D), row),
                   pl.BlockSpec((N_EXPERTS, tm), lambda i: (0, i))),
        compiler_params=_cparams(("arbitrary",)),
        name="merge",
    )(od, os_, of, gates, h32, wbr, wo, g.reshape(1, D), b.reshape(1, D), wr_t)


def _route_body(aff_ref, bias_ref, e_ref, w_ref):
    aff = aff_ref[...]
    sel = aff + bias_ref[...]
    sub = lax.broadcasted_iota(jnp.int32, (EXPERTS_PER_GROUP, aff.shape[1]), 0)
    big = jnp.int32(EXPERTS_PER_GROUP)
    best = None
    for gidx in range(N_GROUPS):
        rs = slice(gidx * EXPERTS_PER_GROUP, (gidx + 1) * EXPERTS_PER_GROUP)
        s8 = sel[rs]
        a8 = aff[rs]
        m1 = jnp.max(s8, axis=0, keepdims=True)
        i1 = jnp.min(jnp.where(s8 == m1, sub, big), axis=0, keepdims=True)
        rest = jnp.where(sub == i1, -jnp.inf, s8)
        m2 = jnp.max(rest, axis=0, keepdims=True)
        i2 = jnp.min(jnp.where(rest == m2, sub, big), axis=0, keepdims=True)
        a1 = jnp.sum(jnp.where(sub == i1, a8, 0.0), axis=0, keepdims=True)
        a2 = jnp.sum(jnp.where(sub == i2, a8, 0.0), axis=0, keepdims=True)
        score = m1 + m2
        cand = (score, i1 + gidx * EXPERTS_PER_GROUP, i2 + gidx * EXPERTS_PER_GROUP, a1, a2)
        if best is None:
            best = cand
        else:
            take = score > best[0]
            best = tuple(jnp.where(take, c, o) for c, o in zip(cand, best))
    _, e1, e2, a1, a2 = best
    tot = a1 + a2
    e_ref[0:1, :] = e1
    e_ref[1:2, :] = e2
    w_ref[0:1, :] = a1 / tot
    w_ref[1:2, :] = a2 / tot


def _route(aff_t, b_router, tn=2048):
    E, T = aff_t.shape
    tn = min(tn, T)
    return pl.pallas_call(
        _route_body,
        out_shape=(jax.ShapeDtypeStruct((TOP_K, T), jnp.int32), jax.ShapeDtypeStruct((TOP_K, T), F32)),
        grid=(T // tn,),
        in_specs=[pl.BlockSpec((E, tn), lambda i: (0, i)), pl.BlockSpec((E, 1), lambda i: (0, 0))],
        out_specs=(pl.BlockSpec((TOP_K, tn), lambda i: (0, i)), pl.BlockSpec((TOP_K, tn), lambda i: (0, i))),
        compiler_params=_cparams(("arbitrary",)),
        name="route",
    )(aff_t, b_router.reshape(E, 1))


def _expert_body(be_ref, bc_ref, tok0_ref, tokn_ref, slot_ref, x_hbm, wg_ref, wu_ref, wd_ref, y_hbm,
                 xbuf, ybuf, gsem, ssem, wg16, wu16, wd16):
    i = pl.program_id(0)
    nb = pl.num_programs(0)
    cur = i & 1
    bm = xbuf.shape[1]
    n_real = y_hbm.shape[0] - 2 * bm

    def start_gather(tok_ref, slot):
        for r in range(bm):
            pltpu.make_async_copy(x_hbm.at[pl.ds(tok_ref[0, 0, r], 1), :], xbuf.at[slot, pl.ds(r, 1), :],
                                  gsem.at[slot]).start(priority=r % 2)

    def wait_gather(slot):
        pltpu.make_async_copy(x_hbm.at[pl.ds(0, bm), :], xbuf.at[slot], gsem.at[slot]).wait()

    def wait_scatter(slot):
        pltpu.make_async_copy(ybuf.at[slot], y_hbm.at[pl.ds(0, bm), :], ssem.at[slot]).wait()

    @pl.when(i == 0)
    def _():
        ybuf[...] = jnp.zeros_like(ybuf)
        for s in range(2):
            spare = pltpu.make_async_copy(ybuf.at[s], y_hbm.at[pl.ds(n_real + s * bm, bm), :], ssem.at[s])
            spare.start()
            spare.wait()
        start_gather(tok0_ref, 0)

    @pl.when(i + 1 < nb)
    def _():
        start_gather(tokn_ref, 1 - cur)

    wait_gather(cur)

    @pl.when(i >= 2)
    def _():
        wait_scatter(cur)

    @pl.when((i == 0) | (be_ref[i] != be_ref[jnp.maximum(i - 1, 0)]))
    def _():
        wg16[...] = wg_ref[0].astype(BF16)
        wu16[...] = wu_ref[0].astype(BF16)
        wd16[...] = wd_ref[0].astype(BF16)

    @pl.when(bc_ref[i] > 0)
    def _():
        x = xbuf[cur].astype(BF16)
        gate = jnp.dot(x, wg16[...], preferred_element_type=F32)
        up = jnp.dot(x, wu16[...], preferred_element_type=F32)
        hmid = (gate * (1.0 / (1.0 + jnp.exp(-gate)))) * up
        ybuf[cur] = jnp.dot(hmid.astype(BF16), wd16[...], preferred_element_type=F32)

    for r in range(bm):
        pltpu.make_async_copy(ybuf.at[cur, pl.ds(r, 1), :], y_hbm.at[pl.ds(slot_ref[0, 0, r], 1), :],
                              ssem.at[cur]).start(priority=r % 2)

    @pl.when(i == nb - 1)
    def _():
        wait_scatter(cur)

        @pl.when(nb > 1)
        def _():
            wait_scatter(1 - cur)


def _experts(h32, blk_e, blk_cnt, tok_pad, slot_pad, wg, wu, wd, bm):
    T, D = h32.shape
    nb = blk_e.shape[0]
    de = wg.shape[2]
    smem_blk = lambda f: pl.BlockSpec((1, 1, bm), f, memory_space=pltpu.SMEM)
    return pl.pallas_call(
        _expert_body,
        out_shape=jax.ShapeDtypeStruct((TOP_K * T + 2 * bm, D), F32),
        grid_spec=pltpu.PrefetchScalarGridSpec(
            num_scalar_prefetch=2,
            grid=(nb,),
            in_specs=[smem_blk(lambda i, be, bc: (0, 0, 0)),
                      smem_blk(lambda i, be, bc: (jnp.minimum(i + 1, nb - 1), 0, 0)),
                      smem_blk(lambda i, be, bc: (i, 0, 0)),
                      pl.BlockSpec(memory_space=pl.ANY),
                      pl.BlockSpec((1, D, de), lambda i, be, bc: (be[i], 0, 0)),
                      pl.BlockSpec((1, D, de), lambda i, be, bc: (be[i], 0, 0)),
                      pl.BlockSpec((1, de, D), lambda i, be, bc: (be[i], 0, 0))],
            out_specs=pl.BlockSpec(memory_space=pl.ANY),
            scratch_shapes=[pltpu.VMEM((2, bm, D), F32), pltpu.VMEM((2, bm, D), F32),
                            pltpu.SemaphoreType.DMA((2,)), pltpu.SemaphoreType.DMA((2,)),
                            pltpu.VMEM((D, de), BF16), pltpu.VMEM((D, de), BF16), pltpu.VMEM((de, D), BF16)]),
        compiler_params=_cparams(("arbitrary",)),
        name="experts",
    )(blk_e, blk_cnt, tok_pad, tok_pad, slot_pad, h32, wg, wu, wd)


def _dispatch_tables(e_t, T, bm):
    A = TOP_K * T
    flat_e = e_t.T.reshape(A)
    order = jnp.argsort(flat_e, stable=True).astype(jnp.int32)
    counts = jnp.sum(flat_e[None, :] == jnp.arange(N_EXPERTS, dtype=jnp.int32)[:, None], axis=1).astype(jnp.int32)
    start = jnp.cumsum(counts) - counts
    nblk_e = (counts + bm - 1) // bm
    blk_end = jnp.cumsum(nblk_e)
    nb = A // bm + N_EXPERTS
    bidx = jnp.arange(nb, dtype=jnp.int32)
    blk_e = jnp.minimum(jnp.sum(bidx[:, None] >= blk_end[None, :], axis=1), N_EXPERTS - 1).astype(jnp.int32)
    off = (bidx - (blk_end - nblk_e)[blk_e]) * bm
    blk_cnt = jnp.where(bidx < blk_end[-1], jnp.clip(counts[blk_e] - off, 0, bm), 0).astype(jnp.int32)
    r = jnp.arange(bm, dtype=jnp.int32)[None, :]
    real = r < blk_cnt[:, None]
    a = order[jnp.clip((start[blk_e] + off)[:, None] + r, 0, A - 1)]
    tok_pad = jnp.where(real, a // TOP_K, 0)
    slot_pad = jnp.where(real, (a % TOP_K) * T + a // TOP_K, A + (bidx % 2)[:, None] * bm + r)
    return blk_e, blk_cnt, tok_pad.reshape(nb, 1, bm), slot_pad.reshape(nb, 1, bm)


def _combine_body(alpha, h_ref, y0_ref, y1_ref, w_ref, g_ref, b_ref, o32_ref, o16_ref):
    w = w_ref[...]
    f = w[:, 0:1] * y0_ref[...] + w[:, 1:2] * y1_ref[...]
    hn = _ln_rows(alpha * h_ref[...] + f, g_ref[...], b_ref[...])
    o32_ref[...] = hn
    o16_ref[...] = hn.astype(BF16)


def _combine(alpha, h32, y2, w_rows, g, b, tm=512):
    T, D = h32.shape
    nt = T // tm
    row = lambda i: (i, 0)
    fixed = lambda i: (0, 0)
    return pl.pallas_call(
        functools.partial(_combine_body, alpha),
        out_shape=(jax.ShapeDtypeStruct((T, D), F32), jax.ShapeDtypeStruct((T, D), BF16)),
        grid=(nt,),
        in_specs=[pl.BlockSpec((tm, D), row), pl.BlockSpec((tm, D), row),
                  pl.BlockSpec((tm, D), lambda i: (nt + i, 0)),
                  pl.BlockSpec((tm, TOP_K), row), pl.BlockSpec((1, D), fixed), pl.BlockSpec((1, D), fixed)],
        out_specs=(pl.BlockSpec((tm, D), row), pl.BlockSpec((tm, D), row)),
        compiler_params=_cparams(("arbitrary",)),
        name="combine",
    )(h32, y2, y2, w_rows, g.reshape(1, D), b.reshape(1, D))


def _pair_halves_columns(w):
    d = w.shape[0]
    w5 = w.reshape(d, -1, 2, 2, HEAD_DIM // 2)
    return jnp.transpose(w5, (0, 1, 3, 2, 4)).reshape(w.shape)


def _prep_in_weights(w_in_l):
    qk_w = DIFF_HEADS * 2 * HEAD_DIM
    sizes = (qk_w, qk_w, DIFF_HEADS * 2 * HEAD_DIM, 512, 512, 512, IDX_HEADS * HEAD_DIM, HEAD_DIM, IDX_HEADS,
             512, 512, 512, FOX_HEADS, N_BRANCH * w_in_l.shape[0])
    offs = [0]
    for s in sizes:
        offs.append(offs[-1] + s)
    dq, dk, dv, sq, sk, sv, iq, ik, iw, fq, fk, fv, ff, g = (w_in_l[:, offs[i]:offs[i + 1]] for i in range(len(sizes)))
    scale = HEAD_DIM ** -0.5
    idx_scale = (IDX_HEADS * HEAD_DIM) ** -0.5
    w_rope = _pair_halves_columns(jnp.concatenate([dq * scale, dk, sq * scale, sk, iq, ik, ik, ik, ik], axis=1))
    w_plain = _pair_halves_columns(jnp.concatenate([fq * scale, fk], axis=1))
    w_vt = jnp.concatenate([dv, sv, fv], axis=1).T
    pad = jnp.zeros((w_in_l.shape[0], LANES - IDX_HEADS - FOX_HEADS), w_in_l.dtype)
    w_small = jnp.concatenate([iw * idx_scale, ff, pad], axis=1)
    return (w_rope.astype(BF16), w_plain.astype(BF16), w_vt.astype(BF16), g.astype(BF16), w_small.astype(BF16))


def _rope_tables(positions):
    inv = ROPE_THETA ** (-jnp.arange(0, HEAD_DIM, 2, dtype=F32) / HEAD_DIM)
    ang = positions.astype(F32).reshape(-1, 1) * inv
    c, s = jnp.cos(ang), jnp.sin(ang)
    return jnp.tile(c, (1, 4)), jnp.concatenate([-s, -s, s, s], axis=1)


def kernel(x, positions, ln_in_g, ln_in_b, w_in, b_forget, diff_lam, diff_norm_g, w_br, w_o, ln1_g, ln1_b,
           w_router, b_router, w_gate, w_up, w_down, ln2_g, ln2_b):
    B, S, D = x.shape
    T = B * S
    depth = w_in.shape[0]
    alpha = (2 * depth) ** 0.25
    topk = min(DSA_TOPK_MAX, S // 4)
    bm = 256
    cc, ss = _rope_tables(positions)
    wr_t = w_router.T.astype(BF16)
    h32, h16 = _layer_norm(x.reshape(T, D), ln_in_g, ln_in_b)
    for l in range(depth):
        lam_init = 0.8 - 0.6 * math.exp(-0.3 * l)
        w_rope, w_plain, w_vt, w_g, w_small = _prep_in_weights(w_in[l])
        qk, n_qk = _project(h16, w_rope, "rope", BF16, B=B, tables=(cc, ss), n_norm=4)
        fqk, n_f = _project(h16, w_plain, "plain", BF16, B=B, n_norm=2)
        vt = _project(h16, w_vt, "transposed", BF16)
        gates = _project(h16, w_g, "sigmoid", BF16)

        o_diff = _attention("diff", qk, QK_DQ, qk, QK_DK, vt, VT_DV, B, S,
                            (diff_lam[l], diff_norm_g[l]), (n_qk, 0, n_qk, 1), lam_init=lam_init)
        mask_t = _dsa_select(qk, h16, w_small[:, 0:8].T, B, S, topk)
        o_dsa = _attention("dsa", qk, QK_SQ, qk, QK_SK, vt, VT_SV, B, S, (mask_t,), (n_qk, 2, n_qk, 3))
        bias_row = jnp.zeros((1, LANES), F32).at[0, SM_FF:SM_FF + FOX_HEADS].set(b_forget[l])
        c, crep = _forget_cumsum(h16, w_small, bias_row, B, S)
        ct = jnp.transpose(c[:, SM_FF:SM_FF + FOX_HEADS].reshape(B, S, FOX_HEADS), (0, 2, 1))
        o_fox = _attention("fox", fqk, FQK_FQ, fqk, FQK_FK, vt, VT_FV, B, S, (ct, crep), (n_f, 0, n_f, 1))

        h32, h16, aff_t = _merge(alpha, o_diff, o_dsa, o_fox, gates, h32, w_br[l].astype(BF16),
                                 w_o[l].astype(BF16), ln1_g[l], ln1_b[l], wr_t)
        e_t, w_t = _route(aff_t, b_router)
        blk_e, blk_cnt, tok_pad, slot_pad = _dispatch_tables(e_t, T, bm)
        y2 = _experts(h32, blk_e + l * N_EXPERTS, blk_cnt, tok_pad, slot_pad,
                      w_gate.reshape((-1,) + w_gate.shape[2:]), w_up.reshape((-1,) + w_up.shape[2:]),
                      w_down.reshape((-1,) + w_down.shape[2:]), bm)
        h32, h16 = _combine(alpha, h32, y2, w_t.T, ln2_g[l], ln2_b[l])
    return h32.reshape(B, S, D)
```

```python
import functools
import math

import jax
import jax.numpy as jnp
from jax import lax
from jax.experimental import pallas as pl
from jax.experimental.pallas import tpu as pltpu

F32 = jnp.float32
BF16 = jnp.bfloat16

CHUNK = 64
HEAD_DIM = 64
ROPE_THETA = 10000.0
LN_EPS = 1e-5
DIFF_HEADS = 4
DSA_HEADS = 8
IDX_HEADS = 4
DSA_TOPK_MAX = 256
FOX_HEADS = 8
N_BRANCH = 3
BRANCH_WIDTH = 512
N_EXPERTS = 32
N_GROUPS = 4
EXPERTS_PER_GROUP = N_EXPERTS // N_GROUPS
TOP_K = 2

LANES = 128
CHUNK_SHIFT = CHUNK.bit_length() - 1
NEG = -1e30
INT_MIN = -2 ** 31
HALF = 1 << 15
REF_MARGIN = 1.01

QK_DQ, QK_DK, QK_SQ, QK_SK, QK_IQ, QK_IK = 0, 512, 1024, 1536, 2048, 2304
QK_WIDTH = 2560
FQK_FQ, FQK_FK = 0, 512
VT_DV, VT_SV, VT_FV = 0, 512, 1024
SM_IW, SM_FF = 0, 4
VMEM_LIMIT = 56 * 1024 * 1024

ROW_TILE = 512
ATTN_TILE = 512
SELECT_TILE = 256
EXPERT_ROWS = 256


def _cparams(sem):
    return pltpu.CompilerParams(dimension_semantics=sem, vmem_limit_bytes=VMEM_LIMIT)


def _ln_rows(x, g, b):
    mu = jnp.mean(x, axis=-1, keepdims=True)
    xc = x - mu
    var = jnp.mean(xc * xc, axis=-1, keepdims=True)
    return xc * lax.rsqrt(var + LN_EPS) * g + b


def _ln_body(x_ref, g_ref, b_ref, o32_ref, o16_ref):
    y = _ln_rows(x_ref[...], g_ref[...], b_ref[...])
    o32_ref[...] = y
    o16_ref[...] = y.astype(BF16)


def _layer_norm(x, g, b, tm=ROW_TILE):
    T, D = x.shape
    return pl.pallas_call(
        _ln_body,
        out_shape=(jax.ShapeDtypeStruct((T, D), F32), jax.ShapeDtypeStruct((T, D), BF16)),
        grid=(T // tm,),
        in_specs=[pl.BlockSpec((tm, D), lambda i: (i, 0)),
                  pl.BlockSpec((1, D), lambda i: (0, 0)),
                  pl.BlockSpec((1, D), lambda i: (0, 0))],
        out_specs=(pl.BlockSpec((tm, D), lambda i: (i, 0)), pl.BlockSpec((tm, D), lambda i: (i, 0))),
        compiler_params=_cparams(("arbitrary",)),
        name="layer_norm",
    )(x, g.reshape(1, D), b.reshape(1, D))


def _proj_body(mode, width, n_norm, steps_per_batch, h_ref, w_ref, *rest):
    if mode == "rope":
        cc_ref, ss_ref, e_ref, o_ref, nmax_ref = rest
    elif mode == "plain":
        e_ref, o_ref, nmax_ref = rest
    else:
        (o_ref,) = rest
    h = h_ref[...]
    tn = 512 if width % 512 == 0 else width
    if n_norm:
        @pl.when(pl.program_id(0) % steps_per_batch == 0)
        def _():
            nmax_ref[...] = jnp.zeros_like(nmax_ref)
    for j in range(width // tn):
        cs = slice(j * tn, (j + 1) * tn)
        if mode == "transposed":
            o_ref[cs, :] = lax.dot_general(w_ref[cs, :], h, (((1,), (1,)), ((), ())),
                                           preferred_element_type=F32).astype(o_ref.dtype)
            continue
        z = jnp.dot(h, w_ref[:, cs], preferred_element_type=F32)
        if j < n_norm:
            n2 = jnp.dot((z * z).astype(BF16), e_ref[...], preferred_element_type=F32)
            nmax_ref[0, j:j + 1, :] = jnp.maximum(nmax_ref[0, j:j + 1, :], jnp.max(n2, axis=0, keepdims=True))
        if mode == "rope":
            cc = cc_ref[...]
            ss = ss_ref[...]
            for i in range(tn // LANES):
                zs = z[:, i * LANES:(i + 1) * LANES]
                o_ref[:, j * tn + i * LANES:j * tn + (i + 1) * LANES] = (
                    zs * cc + pltpu.roll(zs, LANES // 2, 1) * ss).astype(o_ref.dtype)
        elif mode == "sigmoid":
            o_ref[:, cs] = (1.0 / (1.0 + jnp.exp(-z))).astype(o_ref.dtype)
        else:
            o_ref[:, cs] = z.astype(o_ref.dtype)


def _subhead_indicator():
    col = jnp.arange(4 * LANES)
    g = 2 * (col // LANES) + ((col % LANES) // (HEAD_DIM // 2)) % 2
    return (g[:, None] == jnp.arange(LANES)[None, :]).astype(BF16)


def _project(h16, w, mode, out_dtype, B=1, tables=None, n_norm=0, tm=ROW_TILE):
    T, D = h16.shape
    transposed = mode == "transposed"
    width = w.shape[0] if transposed else w.shape[1]
    steps_per_batch = T // B // tm
    row = lambda i: (i, 0)
    fixed = lambda i: (0, 0)
    in_specs = [pl.BlockSpec((tm, D), row), pl.BlockSpec(w.shape, fixed)]
    args = [h16, w]
    if mode == "rope":
        cc, ss = tables
        in_specs += [pl.BlockSpec((tm, LANES), row), pl.BlockSpec((tm, LANES), row)]
        args += [cc, ss]
    out_shape = jax.ShapeDtypeStruct((width, T) if transposed else (T, width), out_dtype)
    out_specs = pl.BlockSpec((width, tm), lambda i: (0, i)) if transposed else pl.BlockSpec((tm, width), row)
    if n_norm:
        in_specs += [pl.BlockSpec((4 * LANES, LANES), fixed)]
        args += [_subhead_indicator()]
        out_shape = (out_shape, jax.ShapeDtypeStruct((B, 8, LANES), F32))
        out_specs = (out_specs, pl.BlockSpec((1, 8, LANES), lambda i: (i // steps_per_batch, 0, 0)))
    return pl.pallas_call(
        functools.partial(_proj_body, mode, width, n_norm, steps_per_batch),
        out_shape=out_shape,
        grid=(T // tm,),
        in_specs=in_specs,
        out_specs=out_specs,
        compiler_params=_cparams(("arbitrary",)),
        name="proj_" + mode,
    )(*args)


def _cumsum_body(h_ref, w_ref, bias_ref, tri_ref, c_ref, crep_ref, carry_ref):
    @pl.when(pl.program_id(1) == 0)
    def _():
        carry_ref[...] = jnp.zeros_like(carry_ref)

    x = jnp.dot(h_ref[...], w_ref[...], preferred_element_type=F32) + bias_ref[...]
    lf = jnp.minimum(x, 0.0) - jnp.log1p(jnp.exp(-jnp.abs(x)))
    hi = lf.astype(BF16)
    r1 = lf - hi.astype(F32)
    mid = r1.astype(BF16)
    lo = (r1 - mid.astype(F32)).astype(BF16)
    tri = tri_ref[...]
    c = (jnp.dot(tri, hi, preferred_element_type=F32) + jnp.dot(tri, mid, preferred_element_type=F32)
         + jnp.dot(tri, lo, preferred_element_type=F32)) + carry_ref[...]
    c_ref[...] = c
    carry_ref[...] = c[-1:, :]
    for h in range(FOX_HEADS):
        crep_ref[:, h * LANES:(h + 1) * LANES] = jnp.broadcast_to(c[:, SM_FF + h:SM_FF + h + 1], (c.shape[0], LANES))


def _forget_cumsum(h16, w_small, bias_row, B, S, tb=ROW_TILE):
    T, D = h16.shape
    nb = S // tb
    tri = jnp.tril(jnp.ones((tb, tb), F32)).astype(BF16)
    return pl.pallas_call(
        _cumsum_body,
        out_shape=(jax.ShapeDtypeStruct((T, LANES), F32), jax.ShapeDtypeStruct((T, FOX_HEADS * LANES), F32)),
        grid=(B, nb),
        in_specs=[pl.BlockSpec((tb, D), lambda b, i: (b * nb + i, 0)),
                  pl.BlockSpec((D, LANES), lambda b, i: (0, 0)),
                  pl.BlockSpec((1, LANES), lambda b, i: (0, 0)),
                  pl.BlockSpec((tb, tb), lambda b, i: (0, 0))],
        out_specs=(pl.BlockSpec((tb, LANES), lambda b, i: (b * nb + i, 0)),
                   pl.BlockSpec((tb, FOX_HEADS * LANES), lambda b, i: (b * nb + i, 0))),
        scratch_shapes=[pltpu.VMEM((1, LANES), F32)],
        compiler_params=_cparams(("arbitrary", "arbitrary")),
        name="forget_cumsum",
    )(h16, w_small, bias_row, tri)


def _attn_body(mode, nslab, lam_init, fixed_ref, qt_ref, kt_ref, q_ref, k_ref, vt_ref, *rest):
    if fixed_ref:
        qn_ref, kn_ref = rest[:2]
        rest = rest[2:]
    if mode == "diff":
        lam_ref, g_ref = rest[:2]
    elif mode == "dsa":
        (mask_ref,) = rest[:1]
    else:
        cq_ref, ck_ref = rest[:2]
    if fixed_ref:
        o_ref, den_ref, m_sc, acc_sc = rest[-4:]
    else:
        o_ref, m_sc, acc_sc = rest[-3:]
    p = pl.program_id(1)
    qi = qt_ref[p]
    ki = kt_ref[p]
    tq = q_ref.shape[0]
    tk = k_ref.shape[0]

    @pl.when(ki == 0)
    def _():
        if fixed_ref:
            q_row, k_row = fixed_ref
            bound = REF_MARGIN * jnp.sqrt(qn_ref[0, q_row:q_row + 1, :] * kn_ref[0, k_row:k_row + 1, :])
            sub_lane = lax.broadcasted_iota(jnp.int32, (1, LANES), 1)
            for idx in range(2 * nslab):
                b_idx = jnp.sum(jnp.where(sub_lane == idx, bound, 0.0), axis=1, keepdims=True)
                m_sc[idx:idx + 1, :] = jnp.broadcast_to(b_idx, (1, tq))
        else:
            m_sc[...] = jnp.full_like(m_sc, NEG)
        acc_sc[...] = jnp.zeros_like(acc_sc)

    lane = lax.broadcasted_iota(jnp.int32, (1, LANES), 1)

    def tile_update(diag):
        if mode == "dsa":
            keep_all = mask_ref[...] != 0
        elif diag:
            krow = lax.broadcasted_iota(jnp.int32, (tk, tq), 0)
            qcol = lax.broadcasted_iota(jnp.int32, (tk, tq), 1)
            if mode == "diff":
                keep_all = (krow >> CHUNK_SHIFT) <= (qcol >> CHUNK_SHIFT)
            else:
                keep_all = krow <= qcol
        def scores(idx):
            ls = slice((idx // 2) * LANES, (idx // 2 + 1) * LANES)
            q = q_ref[:, ls]
            qm = jnp.where(_subhead_lanes(lane, idx % 2), q, jnp.zeros_like(q))
            return lax.dot_general(k_ref[:, ls], qm, (((1,), (1,)), ((), ())), preferred_element_type=F32)

        ones_rows = jnp.ones((8, tk), BF16)
        ahead = [scores(i) for i in range(LOOKAHEAD)]
        for idx in range(2 * nslab):
            s = ahead.pop(0)
            if idx + LOOKAHEAD < 2 * nslab:
                ahead.append(scores(idx + LOOKAHEAD))
            vt = jnp.concatenate([vt_ref[(idx // 2) * LANES:(idx // 2 + 1) * LANES, :], ones_rows], axis=0)
            if mode == "fox":
                ck = ck_ref[:, idx * LANES:(idx + 1) * LANES]
                s = s - jnp.concatenate([ck] * (tq // LANES), axis=1)
            m_old = m_sc[idx:idx + 1, :]
            if fixed_ref:
                ref = (m_old - cq_ref[0, idx:idx + 1, :]) if mode == "fox" else m_old
                pr = jnp.exp(s - ref).astype(BF16)
                if mode == "dsa" or diag:
                    pr = jnp.where(keep_all, pr, jnp.zeros_like(pr))
                acc_sc[idx] = acc_sc[idx] + jnp.dot(vt, pr, preferred_element_type=F32)
                continue
            if mode == "dsa" or diag:
                s = jnp.where(keep_all, s, NEG)
            mx = jnp.max(s, axis=0, keepdims=True)
            if mode == "fox":
                cq = cq_ref[0, idx:idx + 1, :]
                m_new = jnp.maximum(m_old, mx + cq)
                shift = cq - m_new
            else:
                m_new = jnp.maximum(m_old, mx)
                shift = -m_new
            alpha = jnp.exp(m_old - m_new)
            pr = jnp.exp(s + shift).astype(BF16)
            acc_sc[idx] = alpha * acc_sc[idx] + jnp.dot(vt, pr, preferred_element_type=F32)
            m_sc[idx:idx + 1, :] = m_new

    if mode == "dsa":
        tile_update(False)
    else:
        @pl.when(ki == qi)
        def _():
            tile_update(True)

        @pl.when(ki != qi)
        def _():
            tile_update(False)

    @pl.when(ki == qi)
    def _():
        if mode == "diff":
            lm = lam_ref[...]
            lam = (jnp.exp(jnp.sum(lm[0:1] * lm[1:2], keepdims=True))
                   - jnp.exp(jnp.sum(lm[2:3] * lm[3:4], keepdims=True)) + lam_init)
        feat = lax.broadcasted_iota(jnp.int32, (LANES, 1), 0)
        if fixed_ref:
            for idx in range(2 * nslab):
                den_ref[idx:idx + 1, :] = acc_sc[idx, LANES:LANES + 1, :]
        for sl in range(nslab):
            o0 = acc_sc[2 * sl, 0:LANES, :] / acc_sc[2 * sl, LANES:LANES + 1, :]
            o1 = acc_sc[2 * sl + 1, 0:LANES, :] / acc_sc[2 * sl + 1, LANES:LANES + 1, :]
            if mode == "diff":
                o = (o0 - lam * o1).T
                o = o * lax.rsqrt(jnp.mean(o * o, axis=-1, keepdims=True) + LN_EPS)
                o = (o * g_ref[...]) * (1.0 - lam_init)
            else:
                o = jnp.where(feat < HEAD_DIM, o0, o1).T
            o_ref[:, sl * LANES:(sl + 1) * LANES] = o.astype(o_ref.dtype)


def _subhead_lanes(lane, sub):
    return ((lane >> 5) & 1) == sub


LOOKAHEAD = 1
DEN_FLOOR = 1e-30


def _attention(mode, qarr, qcol, karr, kcol, vt, vrow, B, S, extra, norms, lam_init=0.0, tile=ATTN_TILE):
    o_fast, den = _attention_call(mode, qarr, qcol, karr, kcol, vt, vrow, B, S, extra, lam_init, tile, norms)
    bad = jnp.logical_not(jnp.all(den > DEN_FLOOR))
    return lax.cond(
        bad,
        lambda: _attention_call(mode, qarr, qcol, karr, kcol, vt, vrow, B, S, extra, lam_init, tile, None),
        lambda: o_fast)


def _attention_call(mode, qarr, qcol, karr, kcol, vt, vrow, B, S, extra, lam_init, tile, norms):
    nslab = 4
    width = nslab * LANES
    nq = S // tile
    pairs = [(qi, ki) for qi in range(nq) for ki in range(qi + 1)]
    qt = jnp.asarray([pq for pq, _ in pairs], jnp.int32)
    kt = jnp.asarray([pk for _, pk in pairs], jnp.int32)
    T = B * S
    fixed_ref = (norms[1], norms[3]) if norms is not None else None
    qmap = lambda b, p, qt, kt: (b * nq + qt[p], qcol // width)
    kmap = lambda b, p, qt, kt: (b * nq + kt[p], kcol // width)
    vmap = lambda b, p, qt, kt: (vrow // width, b * nq + kt[p])
    in_specs = [pl.BlockSpec((tile, width), qmap), pl.BlockSpec((tile, width), kmap),
                pl.BlockSpec((width, tile), vmap)]
    args = [qarr, karr, vt]
    if fixed_ref:
        nspec = pl.BlockSpec((1, 8, LANES), lambda b, p, qt, kt: (b, 0, 0))
        in_specs += [nspec, nspec]
        args += [norms[0], norms[2]]
    if mode == "diff":
        lam4, g = extra
        in_specs += [pl.BlockSpec((4, HEAD_DIM), lambda b, p, qt, kt: (0, 0)),
                     pl.BlockSpec((1, LANES), lambda b, p, qt, kt: (0, 0))]
        args += [lam4, g.reshape(1, LANES)]
    elif mode == "dsa":
        (mask_t,) = extra
        in_specs += [pl.BlockSpec((tile, tile), lambda b, p, qt, kt: (b * nq + kt[p], qt[p]))]
        args += [mask_t]
    else:
        ct, crep = extra
        in_specs += [pl.BlockSpec((1, FOX_HEADS, tile), lambda b, p, qt, kt: (b, 0, qt[p])),
                     pl.BlockSpec((tile, FOX_HEADS * LANES), lambda b, p, qt, kt: (b * nq + kt[p], 0))]
        args += [ct, crep]
    out_shape = jax.ShapeDtypeStruct((T, width), BF16)
    out_specs = pl.BlockSpec((tile, width), lambda b, p, qt, kt: (b * nq + qt[p], 0))
    if fixed_ref:
        out_shape = (out_shape, jax.ShapeDtypeStruct((2 * nslab, T), F32))
        out_specs = (out_specs, pl.BlockSpec((2 * nslab, tile), lambda b, p, qt, kt: (0, b * nq + qt[p])))
    return pl.pallas_call(
        functools.partial(_attn_body, mode, nslab, lam_init, fixed_ref),
        out_shape=out_shape,
        grid_spec=pltpu.PrefetchScalarGridSpec(
            num_scalar_prefetch=2,
            grid=(B, len(pairs)),
            in_specs=in_specs,
            out_specs=out_specs,
            scratch_shapes=[pltpu.VMEM((2 * nslab, tile), F32),
                            pltpu.VMEM((2 * nslab, LANES + 8, tile), F32)]),
        compiler_params=_cparams(("arbitrary", "arbitrary")),
        name="attn_" + mode + ("_fixed" if fixed_ref else "_online"),
    )(qt, kt, *args)


def _select_body(topk, iq_ref, ik_ref, h_ref, wiw_ref, tri_ref, mask_ref, keys_ref, hi_ref, lo_ref):
    qi = pl.program_id(1)
    tq = iq_ref.shape[0]
    tk = tq
    nkb = qi + 1
    q0 = qi * tq
    lane = lax.broadcasted_iota(jnp.int32, (1, LANES), 1)
    qhs = []
    for h in range(IDX_HEADS):
        slab = iq_ref[:, (h // 2) * LANES:(h // 2 + 1) * LANES]
        qhs.append(jnp.where(_subhead_lanes(lane, h % 2), slab, jnp.zeros_like(slab)))
    iwt = lax.dot_general(wiw_ref[...], h_ref[...], (((1,), (1,)), ((), ())), preferred_element_type=F32)
    q_chunk = (q0 + lax.broadcasted_iota(jnp.int32, (tk, tq), 1)) >> CHUNK_SHIFT
    k_iota = lax.broadcasted_iota(jnp.int32, (tk, tq), 0)

    mask_ref[...] = jnp.zeros_like(mask_ref)

    def score_block(j, carry):
        k0 = pl.multiple_of(j * tk, tk)
        ik = ik_ref[pl.ds(k0, tk), 0:LANES]
        score = jnp.zeros((tk, tq), F32)
        for h in range(IDX_HEADS):
            lg = lax.dot_general(ik, qhs[h], (((1,), (1,)), ((), ())), preferred_element_type=F32)
            score = score + iwt[SM_IW + h:SM_IW + h + 1, :] * jnp.maximum(lg, 0.0)
        score = jnp.where(score == 0.0, 0.0, score)
        bits = lax.bitcast_convert_type(score, jnp.int32)
        key = bits ^ ((bits >> 31) & jnp.int32(0x7FFFFFFF))
        vis = ((k0 + k_iota) >> CHUNK_SHIFT) <= q_chunk
        key = jnp.where(vis, key, jnp.int32(INT_MIN))
        keys_ref[pl.ds(k0, tk), :] = key
        hi_ref[pl.ds(k0, tk), :] = (key >> 16).astype(jnp.int16)
        lo_ref[pl.ds(k0, tk), :] = ((key & jnp.int32(0xFFFF)) - HALF).astype(jnp.int16)
        return carry

    lax.fori_loop(0, nkb, score_block, 0)

    npair = (nkb + 1) >> 1

    @pl.when((nkb & 1) == 1)
    def _():
        k0 = pl.multiple_of(nkb * tk, tk)
        hi_ref[pl.ds(k0, tk), :] = jnp.full((tk, tq), -HALF, jnp.int16)
        lo_ref[pl.ds(k0, tk), :] = jnp.full((tk, tq), -HALF, jnp.int16)

    def count16(ref, pred):
        def body(j, acc):
            k0 = pl.multiple_of(j * (2 * tk), 2 * tk)
            hit = jnp.where(pred(ref[pl.ds(k0, 2 * tk), :]), jnp.int16(1), jnp.int16(0))
            for r in range(2 * tk // 16):
                acc = acc + hit[r * 16:(r + 1) * 16, :]
            return acc
        acc = lax.fori_loop(0, npair, body, jnp.zeros((16, tq), jnp.int16))
        return jnp.sum(acc.astype(jnp.int32), axis=0, keepdims=True)

    def search16(ref, base):
        def bit_step(i, lo):
            cand = lo + (jnp.int32(1) << (15 - i))
            c16 = cand.astype(jnp.int16)
            return jnp.where(base + count16(ref, lambda blk: blk >= c16) >= topk, cand, lo)
        return lax.fori_loop(0, 16, bit_step, jnp.full((1, tq), -HALF, jnp.int32))

    thr_hi = search16(hi_ref, 0)
    thr_hi16 = thr_hi.astype(jnp.int16)
    above = count16(hi_ref, lambda blk: blk > thr_hi16)

    def keep_low_of_threshold_rows(j, carry):
        k0 = pl.multiple_of(j * tk, tk)
        lo_ref[pl.ds(k0, tk), :] = jnp.where(hi_ref[pl.ds(k0, tk), :] == thr_hi16, lo_ref[pl.ds(k0, tk), :],
                                             jnp.int16(-HALF))
        return carry

    lax.fori_loop(0, nkb, keep_low_of_threshold_rows, 0)
    thr_lo = search16(lo_ref, above)
    thr_lo16 = thr_lo.astype(jnp.int16)
    above = above + count16(lo_ref, lambda blk: blk > thr_lo16)
    thr = (thr_hi << 16) + (thr_lo + HALF)
    thr = jnp.maximum(thr, jnp.int32(INT_MIN + 1))
    need = (topk - above).astype(F32)
    tri = tri_ref[...]

    def select_block(j, carry):
        k0 = pl.multiple_of(j * tk, tk)
        key = keys_ref[pl.ds(k0, tk), :]
        eq = key == thr
        incl = jnp.dot(tri, jnp.where(eq, 1.0, 0.0).astype(BF16), preferred_element_type=F32)
        sel = (key > thr) | (eq & ((carry + incl) <= need))
        mask_ref[pl.ds(k0, tk), :] = jnp.where(sel, 1, 0).astype(jnp.int8)
        return carry + incl[tk - 1:tk, :]

    lax.fori_loop(0, nkb, select_block, jnp.zeros((1, tq), F32))


def _dsa_select(qk, h16, w_iw_t, B, S, topk, tq=SELECT_TILE):
    T = B * S
    D = h16.shape[1]
    nq = S // tq
    tri = jnp.tril(jnp.ones((tq, tq), F32)).astype(BF16)
    return pl.pallas_call(
        functools.partial(_select_body, topk),
        out_shape=jax.ShapeDtypeStruct((T, S), jnp.int8),
        grid=(B, nq),
        in_specs=[pl.BlockSpec((tq, 256), lambda b, i: (b * nq + i, QK_IQ // 256)),
                  pl.BlockSpec((S, 256), lambda b, i: (b, QK_IK // 256)),
                  pl.BlockSpec((tq, D), lambda b, i: (b * nq + i, 0)),
                  pl.BlockSpec((8, D), lambda b, i: (0, 0)),
                  pl.BlockSpec((tq, tq), lambda b, i: (0, 0))],
        out_specs=pl.BlockSpec((S, tq), lambda b, i: (b, i)),
        scratch_shapes=[pltpu.VMEM((S, tq), jnp.int32), pltpu.VMEM((S, tq), jnp.int16),
                        pltpu.VMEM((S, tq), jnp.int16)],
        compiler_params=_cparams(("arbitrary", "arbitrary")),
        name="dsa_select",
    )(qk, qk, h16, w_iw_t, tri)


def _merge_body(alpha, od_ref, os_ref, of_ref, gate_ref, h_ref, wbr_ref, wo_ref, g_ref, b_ref, wr_ref,
                h32_ref, h16_ref, aff_ref):
    D = h_ref.shape[1]
    y = None
    for i, o_ref in enumerate((od_ref, os_ref, of_ref)):
        br = jnp.dot(o_ref[...], wbr_ref[i], preferred_element_type=F32)
        t = gate_ref[:, i * D:(i + 1) * D].astype(F32) * br
        y = t if y is None else y + t
    m = jnp.dot(y.astype(BF16), wo_ref[...], preferred_element_type=F32)
    hn = _ln_rows(alpha * h_ref[...] + m, g_ref[...], b_ref[...])
    h32_ref[...] = hn
    h16 = hn.astype(BF16)
    h16_ref[...] = h16
    logits = lax.dot_general(wr_ref[...], h16, (((1,), (1,)), ((), ())), preferred_element_type=F32)
    aff_ref[...] = 1.0 / (1.0 + jnp.exp(-logits))


def _merge(alpha, od, os_, of, gates, h32, wbr, wo, g, b, wr_t, tm=ROW_TILE):
    T, D = h32.shape
    bw = od.shape[1]
    row = lambda i: (i, 0)
    fixed2 = lambda i: (0, 0)
    return pl.pallas_call(
        functools.partial(_merge_body, alpha),
        out_shape=(jax.ShapeDtypeStruct((T, D), F32), jax.ShapeDtypeStruct((T, D), BF16),
                   jax.ShapeDtypeStruct((N_EXPERTS, T), F32)),
        grid=(T // tm,),
        in_specs=[pl.BlockSpec((tm, bw), row), pl.BlockSpec((tm, bw), row), pl.BlockSpec((tm, bw), row),
                  pl.BlockSpec((tm, N_BRANCH * D), row), pl.BlockSpec((tm, D), row),
                  pl.BlockSpec((N_BRANCH, bw, D), lambda i: (0, 0, 0)), pl.BlockSpec((D, D), fixed2),
                  pl.BlockSpec((1, D), fixed2), pl.BlockSpec((1, D), fixed2),
                  pl.BlockSpec((N_EXPERTS, D), fixed2)],
        out_specs=(pl.BlockSpec((tm, D), row), pl.BlockSpec((tm, D), row),
                   pl.BlockSpec((N_EXPERTS, tm), lambda i: (0, i))),
        compiler_params=_cparams(("arbitrary",)),
        name="merge",
    )(od, os_, of, gates, h32, wbr, wo, g.reshape(1, D), b.reshape(1, D), wr_t)


def _route_body(aff_ref, bias_ref, e_ref, w_ref):
    aff = aff_ref[...]
    sel = aff + bias_ref[...]
    sub = lax.broadcasted_iota(jnp.int32, (EXPERTS_PER_GROUP, aff.shape[1]), 0)
    big = jnp.int32(EXPERTS_PER_GROUP)
    best = None
    for gidx in range(N_GROUPS):
        rs = slice(gidx * EXPERTS_PER_GROUP, (gidx + 1) * EXPERTS_PER_GROUP)
        s8 = sel[rs]
        a8 = aff[rs]
        m1 = jnp.max(s8, axis=0, keepdims=True)
        i1 = jnp.min(jnp.where(s8 == m1, sub, big), axis=0, keepdims=True)
        rest = jnp.where(sub == i1, -jnp.inf, s8)
        m2 = jnp.max(rest, axis=0, keepdims=True)
        i2 = jnp.min(jnp.where(rest == m2, sub, big), axis=0, keepdims=True)
        a1 = jnp.sum(jnp.where(sub == i1, a8, 0.0), axis=0, keepdims=True)
        a2 = jnp.sum(jnp.where(sub == i2, a8, 0.0), axis=0, keepdims=True)
        score = m1 + m2
        cand = (score, i1 + gidx * EXPERTS_PER_GROUP, i2 + gidx * EXPERTS_PER_GROUP, a1, a2)
        if best is None:
            best = cand
        else:
            take = score > best[0]
            best = tuple(jnp.where(take, c, o) for c, o in zip(cand, best))
    _, e1, e2, a1, a2 = best
    tot = a1 + a2
    e_ref[0:1, :] = e1
    e_ref[1:2, :] = e2
    w_ref[0:1, :] = a1 / tot
    w_ref[1:2, :] = a2 / tot


def _route(aff_t, b_router, tn=2048):
    E, T = aff_t.shape
    tn = min(tn, T)
    return pl.pallas_call(
        _route_body,
        out_shape=(jax.ShapeDtypeStruct((TOP_K, T), jnp.int32), jax.ShapeDtypeStruct((TOP_K, T), F32)),
        grid=(T // tn,),
        in_specs=[pl.BlockSpec((E, tn), lambda i: (0, i)), pl.BlockSpec((E, 1), lambda i: (0, 0))],
        out_specs=(pl.BlockSpec((TOP_K, tn), lambda i: (0, i)), pl.BlockSpec((TOP_K, tn), lambda i: (0, i))),
        compiler_params=_cparams(("arbitrary",)),
        name="route",
    )(aff_t, b_router.reshape(E, 1))


def _expert_body(be_ref, bc_ref, tok0_ref, tokn_ref, slot_ref, x_hbm, wg_ref, wu_ref, wd_ref, y_hbm,
                 xbuf, ybuf, gsem, ssem, wg16, wu16, wd16):
    i = pl.program_id(0)
    nb = pl.num_programs(0)
    cur = i & 1
    bm = xbuf.shape[1]
    n_real = y_hbm.shape[0] - 2 * bm

    def busy(j):
        return (j >= 0) & (j < nb) & (bc_ref[jnp.clip(j, 0, nb - 1)] > 0)

    def start_gather(tok_ref, slot):
        for r in range(bm):
            pltpu.make_async_copy(x_hbm.at[pl.ds(tok_ref[0, 0, r], 1), :], xbuf.at[slot, pl.ds(r, 1), :],
                                  gsem.at[slot]).start(priority=r % 2)

    def wait_gather(slot):
        pltpu.make_async_copy(x_hbm.at[pl.ds(0, bm), :], xbuf.at[slot], gsem.at[slot]).wait()

    def wait_scatter(slot):
        pltpu.make_async_copy(ybuf.at[slot], y_hbm.at[pl.ds(0, bm), :], ssem.at[slot]).wait()

    @pl.when(i == 0)
    def _():
        ybuf[...] = jnp.zeros_like(ybuf)
        for s in range(2):
            spare = pltpu.make_async_copy(ybuf.at[s], y_hbm.at[pl.ds(n_real + s * bm, bm), :], ssem.at[s])
            spare.start()
            spare.wait()

        @pl.when(busy(0))
        def _():
            start_gather(tok0_ref, 0)

    @pl.when(busy(i + 1))
    def _():
        start_gather(tokn_ref, 1 - cur)

    @pl.when(busy(i))
    def _():
        wait_gather(cur)

    @pl.when(busy(i - 2))
    def _():
        wait_scatter(cur)

    @pl.when((i == 0) | (be_ref[i] != be_ref[jnp.maximum(i - 1, 0)]))
    def _():
        wg16[...] = wg_ref[0].astype(BF16)
        wu16[...] = wu_ref[0].astype(BF16)
        wd16[...] = wd_ref[0].astype(BF16)

    @pl.when(busy(i))
    def _():
        x = xbuf[cur].astype(BF16)
        gate = jnp.dot(x, wg16[...], preferred_element_type=F32)
        up = jnp.dot(x, wu16[...], preferred_element_type=F32)
        hmid = (gate * (1.0 / (1.0 + jnp.exp(-gate)))) * up
        ybuf[cur] = jnp.dot(hmid.astype(BF16), wd16[...], preferred_element_type=F32)
        for r in range(bm):
            pltpu.make_async_copy(ybuf.at[cur, pl.ds(r, 1), :], y_hbm.at[pl.ds(slot_ref[0, 0, r], 1), :],
                                  ssem.at[cur]).start(priority=r % 2)

    @pl.when(i == nb - 1)
    def _():
        @pl.when(busy(i))
        def _():
            wait_scatter(cur)

        @pl.when(busy(i - 1))
        def _():
            wait_scatter(1 - cur)


def _experts(h32, blk_e, blk_cnt, tok_pad, slot_pad, wg, wu, wd, bm):
    T, D = h32.shape
    nb = blk_e.shape[0]
    de = wg.shape[2]
    smem_blk = lambda f: pl.BlockSpec((1, 1, bm), f, memory_space=pltpu.SMEM)
    return pl.pallas_call(
        _expert_body,
        out_shape=jax.ShapeDtypeStruct((TOP_K * T + 2 * bm, D), F32),
        grid_spec=pltpu.PrefetchScalarGridSpec(
            num_scalar_prefetch=2,
            grid=(nb,),
            in_specs=[smem_blk(lambda i, be, bc: (0, 0, 0)),
                      smem_blk(lambda i, be, bc: (jnp.minimum(i + 1, nb - 1), 0, 0)),
                      smem_blk(lambda i, be, bc: (i, 0, 0)),
                      pl.BlockSpec(memory_space=pl.ANY),
                      pl.BlockSpec((1, D, de), lambda i, be, bc: (be[i], 0, 0)),
                      pl.BlockSpec((1, D, de), lambda i, be, bc: (be[i], 0, 0)),
                      pl.BlockSpec((1, de, D), lambda i, be, bc: (be[i], 0, 0))],
            out_specs=pl.BlockSpec(memory_space=pl.ANY),
            scratch_shapes=[pltpu.VMEM((2, bm, D), F32), pltpu.VMEM((2, bm, D), F32),
                            pltpu.SemaphoreType.DMA((2,)), pltpu.SemaphoreType.DMA((2,)),
                            pltpu.VMEM((D, de), BF16), pltpu.VMEM((D, de), BF16), pltpu.VMEM((de, D), BF16)]),
        compiler_params=_cparams(("arbitrary",)),
        name="experts",
    )(blk_e, blk_cnt, tok_pad, tok_pad, slot_pad, h32, wg, wu, wd)


def _dispatch_tables(e_t, T, bm):
    A = TOP_K * T
    flat_e = e_t.T.reshape(A)
    order = jnp.argsort(flat_e, stable=True).astype(jnp.int32)
    counts = jnp.sum(flat_e[None, :] == jnp.arange(N_EXPERTS, dtype=jnp.int32)[:, None], axis=1).astype(jnp.int32)
    start = jnp.cumsum(counts) - counts
    nblk_e = (counts + bm - 1) // bm
    blk_end = jnp.cumsum(nblk_e)
    nb = A // bm + N_EXPERTS
    bidx = jnp.arange(nb, dtype=jnp.int32)
    blk_e = jnp.minimum(jnp.sum(bidx[:, None] >= blk_end[None, :], axis=1), N_EXPERTS - 1).astype(jnp.int32)
    off = (bidx - (blk_end - nblk_e)[blk_e]) * bm
    blk_cnt = jnp.where(bidx < blk_end[-1], jnp.clip(counts[blk_e] - off, 0, bm), 0).astype(jnp.int32)
    r = jnp.arange(bm, dtype=jnp.int32)[None, :]
    real = r < blk_cnt[:, None]
    a = order[jnp.clip((start[blk_e] + off)[:, None] + r, 0, A - 1)]
    tok_pad = jnp.where(real, a // TOP_K, 0)
    slot_pad = jnp.where(real, (a % TOP_K) * T + a // TOP_K, A + (bidx % 2)[:, None] * bm + r)
    return blk_e, blk_cnt, tok_pad.reshape(nb, 1, bm), slot_pad.reshape(nb, 1, bm)


def _combine_body(alpha, h_ref, y0_ref, y1_ref, w_ref, g_ref, b_ref, o32_ref, o16_ref):
    w = w_ref[...]
    f = w[:, 0:1] * y0_ref[...] + w[:, 1:2] * y1_ref[...]
    hn = _ln_rows(alpha * h_ref[...] + f, g_ref[...], b_ref[...])
    o32_ref[...] = hn
    o16_ref[...] = hn.astype(BF16)


def _combine(alpha, h32, y2, w_rows, g, b, tm=ROW_TILE):
    T, D = h32.shape
    nt = T // tm
    row = lambda i: (i, 0)
    fixed = lambda i: (0, 0)
    return pl.pallas_call(
        functools.partial(_combine_body, alpha),
        out_shape=(jax.ShapeDtypeStruct((T, D), F32), jax.ShapeDtypeStruct((T, D), BF16)),
        grid=(nt,),
        in_specs=[pl.BlockSpec((tm, D), row), pl.BlockSpec((tm, D), row),
                  pl.BlockSpec((tm, D), lambda i: (nt + i, 0)),
                  pl.BlockSpec((tm, TOP_K), row), pl.BlockSpec((1, D), fixed), pl.BlockSpec((1, D), fixed)],
        out_specs=(pl.BlockSpec((tm, D), row), pl.BlockSpec((tm, D), row)),
        compiler_params=_cparams(("arbitrary",)),
        name="combine",
    )(h32, y2, y2, w_rows, g.reshape(1, D), b.reshape(1, D))


def _pair_halves_columns(w):
    d = w.shape[0]
    w5 = w.reshape(d, -1, 2, 2, HEAD_DIM // 2)
    return jnp.transpose(w5, (0, 1, 3, 2, 4)).reshape(w.shape)


def _prep_in_weights(w_in_l):
    qk_w = DIFF_HEADS * 2 * HEAD_DIM
    sizes = (qk_w, qk_w, DIFF_HEADS * 2 * HEAD_DIM, 512, 512, 512, IDX_HEADS * HEAD_DIM, HEAD_DIM, IDX_HEADS,
             512, 512, 512, FOX_HEADS, N_BRANCH * w_in_l.shape[0])
    offs = [0]
    for s in sizes:
        offs.append(offs[-1] + s)
    dq, dk, dv, sq, sk, sv, iq, ik, iw, fq, fk, fv, ff, g = (w_in_l[:, offs[i]:offs[i + 1]] for i in range(len(sizes)))
    scale = HEAD_DIM ** -0.5
    idx_scale = (IDX_HEADS * HEAD_DIM) ** -0.5
    w_rope = _pair_halves_columns(jnp.concatenate([dq * scale, dk, sq * scale, sk, iq, ik, ik, ik, ik], axis=1))
    w_plain = _pair_halves_columns(jnp.concatenate([fq * scale, fk], axis=1))
    w_vt = jnp.concatenate([dv, sv, fv], axis=1).T
    pad = jnp.zeros((w_in_l.shape[0], LANES - IDX_HEADS - FOX_HEADS), w_in_l.dtype)
    w_small = jnp.concatenate([iw * idx_scale, ff, pad], axis=1)
    return (w_rope.astype(BF16), w_plain.astype(BF16), w_vt.astype(BF16), g.astype(BF16), w_small.astype(BF16))


def _rope_tables(positions):
    inv = ROPE_THETA ** (-jnp.arange(0, HEAD_DIM, 2, dtype=F32) / HEAD_DIM)
    ang = positions.astype(F32).reshape(-1, 1) * inv
    c, s = jnp.cos(ang), jnp.sin(ang)
    return jnp.tile(c, (1, 4)), jnp.concatenate([-s, -s, s, s], axis=1)


def kernel(x, positions, ln_in_g, ln_in_b, w_in, b_forget, diff_lam, diff_norm_g, w_br, w_o, ln1_g, ln1_b,
           w_router, b_router, w_gate, w_up, w_down, ln2_g, ln2_b):
    B, S, D = x.shape
    T = B * S
    depth = w_in.shape[0]
    alpha = (2 * depth) ** 0.25
    topk = min(DSA_TOPK_MAX, S // 4)
    bm = EXPERT_ROWS
    cc, ss = _rope_tables(positions)
    wr_t = w_router.T.astype(BF16)
    h32, h16 = _layer_norm(x.reshape(T, D), ln_in_g, ln_in_b)
    for l in range(depth):
        lam_init = 0.8 - 0.6 * math.exp(-0.3 * l)
        w_rope, w_plain, w_vt, w_g, w_small = _prep_in_weights(w_in[l])
        qk, n_qk = _project(h16, w_rope, "rope", BF16, B=B, tables=(cc, ss), n_norm=4)
        fqk, n_f = _project(h16, w_plain, "plain", BF16, B=B, n_norm=2)
        vt = _project(h16, w_vt, "transposed", BF16)
        gates = _project(h16, w_g, "sigmoid", BF16)

        o_diff = _attention("diff", qk, QK_DQ, qk, QK_DK, vt, VT_DV, B, S,
                            (diff_lam[l], diff_norm_g[l]), (n_qk, 0, n_qk, 1), lam_init=lam_init)
        mask_t = _dsa_select(qk, h16, w_small[:, 0:8].T, B, S, topk)
        o_dsa = _attention("dsa", qk, QK_SQ, qk, QK_SK, vt, VT_SV, B, S, (mask_t,), (n_qk, 2, n_qk, 3))
        bias_row = jnp.zeros((1, LANES), F32).at[0, SM_FF:SM_FF + FOX_HEADS].set(b_forget[l])
        c, crep = _forget_cumsum(h16, w_small, bias_row, B, S)
        ct = jnp.transpose(c[:, SM_FF:SM_FF + FOX_HEADS].reshape(B, S, FOX_HEADS), (0, 2, 1))
        o_fox = _attention("fox", fqk, FQK_FQ, fqk, FQK_FK, vt, VT_FV, B, S, (ct, crep), (n_f, 0, n_f, 1))

        h32, h16, aff_t = _merge(alpha, o_diff, o_dsa, o_fox, gates, h32, w_br[l].astype(BF16),
                                 w_o[l].astype(BF16), ln1_g[l], ln1_b[l], wr_t)
        e_t, w_t = _route(aff_t, b_router)
        blk_e, blk_cnt, tok_pad, slot_pad = _dispatch_tables(e_t, T, bm)
        y2 = _experts(h32, blk_e + l * N_EXPERTS, blk_cnt, tok_pad, slot_pad,
                      w_gate.reshape((-1,) + w_gate.shape[2:]), w_up.reshape((-1,) + w_up.shape[2:]),
                      w_down.reshape((-1,) + w_down.shape[2:]), bm)
        h32, h16 = _combine(alpha, h32, y2, w_t.T, ln2_g[l], ln2_b[l])
    return h32.reshape(B, S, D)
```

```python
import functools
import math

import jax
import jax.numpy as jnp
from jax import lax
from jax.experimental import pallas as pl
from jax.experimental.pallas import tpu as pltpu

F32 = jnp.float32
BF16 = jnp.bfloat16

CHUNK = 64
HEAD_DIM = 64
ROPE_THETA = 10000.0
LN_EPS = 1e-5
DIFF_HEADS = 4
DSA_HEADS = 8
IDX_HEADS = 4
DSA_TOPK_MAX = 256
FOX_HEADS = 8
N_BRANCH = 3
BRANCH_WIDTH = 512
N_EXPERTS = 32
N_GROUPS = 4
EXPERTS_PER_GROUP = N_EXPERTS // N_GROUPS
TOP_K = 2

LANES = 128
CHUNK_SHIFT = CHUNK.bit_length() - 1
NEG = -1e30
INT_MIN = -2 ** 31
HALF = 1 << 15
REF_MARGIN = 1.01

QK_DQ, QK_DK, QK_SQ, QK_SK, QK_IQ, QK_IK = 0, 512, 1024, 1536, 2048, 2304
QK_WIDTH = 2560
FQK_FQ, FQK_FK = 0, 512
VT_DV, VT_SV, VT_FV = 0, 512, 1024
SM_IW, SM_FF = 0, 4
VMEM_LIMIT = 56 * 1024 * 1024

ROW_TILE = 512
ATTN_TILE = 512
SELECT_TILE = 512
EXPERT_ROWS = 256


def _cparams(sem):
    return pltpu.CompilerParams(dimension_semantics=sem, vmem_limit_bytes=VMEM_LIMIT)


def _ln_rows(x, g, b):
    mu = jnp.mean(x, axis=-1, keepdims=True)
    xc = x - mu
    var = jnp.mean(xc * xc, axis=-1, keepdims=True)
    return xc * lax.rsqrt(var + LN_EPS) * g + b


def _ln_body(x_ref, g_ref, b_ref, o32_ref, o16_ref):
    y = _ln_rows(x_ref[...], g_ref[...], b_ref[...])
    o32_ref[...] = y
    o16_ref[...] = y.astype(BF16)


def _layer_norm(x, g, b, tm=ROW_TILE):
    T, D = x.shape
    return pl.pallas_call(
        _ln_body,
        out_shape=(jax.ShapeDtypeStruct((T, D), F32), jax.ShapeDtypeStruct((T, D), BF16)),
        grid=(T // tm,),
        in_specs=[pl.BlockSpec((tm, D), lambda i: (i, 0)),
                  pl.BlockSpec((1, D), lambda i: (0, 0)),
                  pl.BlockSpec((1, D), lambda i: (0, 0))],
        out_specs=(pl.BlockSpec((tm, D), lambda i: (i, 0)), pl.BlockSpec((tm, D), lambda i: (i, 0))),
        compiler_params=_cparams(("arbitrary",)),
        name="layer_norm",
    )(x, g.reshape(1, D), b.reshape(1, D))


def _proj_body(mode, width, n_norm, steps_per_batch, h_ref, w_ref, *rest):
    if mode == "rope":
        cc_ref, ss_ref, e_ref, o_ref, nmax_ref = rest
    elif mode == "plain":
        e_ref, o_ref, nmax_ref = rest
    else:
        (o_ref,) = rest
    h = h_ref[...]
    tn = 512 if width % 512 == 0 else width
    if n_norm:
        @pl.when(pl.program_id(0) % steps_per_batch == 0)
        def _():
            nmax_ref[...] = jnp.zeros_like(nmax_ref)
    for j in range(width // tn):
        cs = slice(j * tn, (j + 1) * tn)
        if mode == "transposed":
            o_ref[cs, :] = lax.dot_general(w_ref[cs, :], h, (((1,), (1,)), ((), ())),
                                           preferred_element_type=F32).astype(o_ref.dtype)
            continue
        z = jnp.dot(h, w_ref[:, cs], preferred_element_type=F32)
        if j < n_norm:
            n2 = jnp.dot((z * z).astype(BF16), e_ref[...], preferred_element_type=F32)
            nmax_ref[0, j:j + 1, :] = jnp.maximum(nmax_ref[0, j:j + 1, :], jnp.max(n2, axis=0, keepdims=True))
        if mode == "rope":
            cc = cc_ref[...]
            ss = ss_ref[...]
            for i in range(tn // LANES):
                zs = z[:, i * LANES:(i + 1) * LANES]
                o_ref[:, j * tn + i * LANES:j * tn + (i + 1) * LANES] = (
                    zs * cc + pltpu.roll(zs, LANES // 2, 1) * ss).astype(o_ref.dtype)
        elif mode == "sigmoid":
            o_ref[:, cs] = (1.0 / (1.0 + jnp.exp(-z))).astype(o_ref.dtype)
        else:
            o_ref[:, cs] = z.astype(o_ref.dtype)


def _subhead_indicator():
    col = jnp.arange(4 * LANES)
    g = 2 * (col // LANES) + ((col % LANES) // (HEAD_DIM // 2)) % 2
    return (g[:, None] == jnp.arange(LANES)[None, :]).astype(BF16)


def _project(h16, w, mode, out_dtype, B=1, tables=None, n_norm=0, tm=ROW_TILE):
    T, D = h16.shape
    transposed = mode == "transposed"
    width = w.shape[0] if transposed else w.shape[1]
    steps_per_batch = T // B // tm
    row = lambda i: (i, 0)
    fixed = lambda i: (0, 0)
    in_specs = [pl.BlockSpec((tm, D), row), pl.BlockSpec(w.shape, fixed)]
    args = [h16, w]
    if mode == "rope":
        cc, ss = tables
        in_specs += [pl.BlockSpec((tm, LANES), row), pl.BlockSpec((tm, LANES), row)]
        args += [cc, ss]
    out_shape = jax.ShapeDtypeStruct((width, T) if transposed else (T, width), out_dtype)
    out_specs = pl.BlockSpec((width, tm), lambda i: (0, i)) if transposed else pl.BlockSpec((tm, width), row)
    if n_norm:
        in_specs += [pl.BlockSpec((4 * LANES, LANES), fixed)]
        args += [_subhead_indicator()]
        out_shape = (out_shape, jax.ShapeDtypeStruct((B, 8, LANES), F32))
        out_specs = (out_specs, pl.BlockSpec((1, 8, LANES), lambda i: (i // steps_per_batch, 0, 0)))
    return pl.pallas_call(
        functools.partial(_proj_body, mode, width, n_norm, steps_per_batch),
        out_shape=out_shape,
        grid=(T // tm,),
        in_specs=in_specs,
        out_specs=out_specs,
        compiler_params=_cparams(("arbitrary",)),
        name="proj_" + mode,
    )(*args)


def _cumsum_body(h_ref, w_ref, bias_ref, tri_ref, c_ref, crep_ref, carry_ref):
    @pl.when(pl.program_id(1) == 0)
    def _():
        carry_ref[...] = jnp.zeros_like(carry_ref)

    x = jnp.dot(h_ref[...], w_ref[...], preferred_element_type=F32) + bias_ref[...]
    lf = jnp.minimum(x, 0.0) - jnp.log1p(jnp.exp(-jnp.abs(x)))
    hi = lf.astype(BF16)
    r1 = lf - hi.astype(F32)
    mid = r1.astype(BF16)
    lo = (r1 - mid.astype(F32)).astype(BF16)
    tri = tri_ref[...]
    c = (jnp.dot(tri, hi, preferred_element_type=F32) + jnp.dot(tri, mid, preferred_element_type=F32)
         + jnp.dot(tri, lo, preferred_element_type=F32)) + carry_ref[...]
    c_ref[...] = c
    carry_ref[...] = c[-1:, :]
    for h in range(FOX_HEADS):
        crep_ref[:, h * LANES:(h + 1) * LANES] = jnp.broadcast_to(c[:, SM_FF + h:SM_FF + h + 1], (c.shape[0], LANES))


def _forget_cumsum(h16, w_small, bias_row, B, S, tb=ROW_TILE):
    T, D = h16.shape
    nb = S // tb
    tri = jnp.tril(jnp.ones((tb, tb), F32)).astype(BF16)
    return pl.pallas_call(
        _cumsum_body,
        out_shape=(jax.ShapeDtypeStruct((T, LANES), F32), jax.ShapeDtypeStruct((T, FOX_HEADS * LANES), F32)),
        grid=(B, nb),
        in_specs=[pl.BlockSpec((tb, D), lambda b, i: (b * nb + i, 0)),
                  pl.BlockSpec((D, LANES), lambda b, i: (0, 0)),
                  pl.BlockSpec((1, LANES), lambda b, i: (0, 0)),
                  pl.BlockSpec((tb, tb), lambda b, i: (0, 0))],
        out_specs=(pl.BlockSpec((tb, LANES), lambda b, i: (b * nb + i, 0)),
                   pl.BlockSpec((tb, FOX_HEADS * LANES), lambda b, i: (b * nb + i, 0))),
        scratch_shapes=[pltpu.VMEM((1, LANES), F32)],
        compiler_params=_cparams(("arbitrary", "arbitrary")),
        name="forget_cumsum",
    )(h16, w_small, bias_row, tri)


def _attn_body(mode, nslab, lam_init, fixed_ref, qt_ref, kt_ref, q_ref, k_ref, vt_ref, *rest):
    if fixed_ref:
        qn_ref, kn_ref = rest[:2]
        rest = rest[2:]
    if mode == "diff":
        lam_ref, g_ref = rest[:2]
    elif mode == "dsa":
        (mask_ref,) = rest[:1]
    else:
        cq_ref, ck_ref = rest[:2]
    if fixed_ref:
        o_ref, den_ref, m_sc, acc_sc = rest[-4:]
    else:
        o_ref, m_sc, acc_sc = rest[-3:]
    p = pl.program_id(1)
    qi = qt_ref[p]
    ki = kt_ref[p]
    tq = q_ref.shape[0]
    tk = k_ref.shape[0]

    @pl.when(ki == 0)
    def _():
        if fixed_ref:
            q_row, k_row = fixed_ref
            bound = REF_MARGIN * jnp.sqrt(qn_ref[0, q_row:q_row + 1, :] * kn_ref[0, k_row:k_row + 1, :])
            sub_lane = lax.broadcasted_iota(jnp.int32, (1, LANES), 1)
            for idx in range(2 * nslab):
                b_idx = jnp.sum(jnp.where(sub_lane == idx, bound, 0.0), axis=1, keepdims=True)
                m_sc[idx:idx + 1, :] = jnp.broadcast_to(b_idx, (1, tq))
        else:
            m_sc[...] = jnp.full_like(m_sc, NEG)
        acc_sc[...] = jnp.zeros_like(acc_sc)

    lane = lax.broadcasted_iota(jnp.int32, (1, LANES), 1)

    def tile_update(diag):
        if mode == "dsa":
            keep_all = mask_ref[...] != 0
        elif diag:
            krow = lax.broadcasted_iota(jnp.int32, (tk, tq), 0)
            qcol = lax.broadcasted_iota(jnp.int32, (tk, tq), 1)
            if mode == "diff":
                keep_all = (krow >> CHUNK_SHIFT) <= (qcol >> CHUNK_SHIFT)
            else:
                keep_all = krow <= qcol
        def scores(idx):
            ls = slice((idx // 2) * LANES, (idx // 2 + 1) * LANES)
            q = q_ref[:, ls]
            qm = jnp.where(_subhead_lanes(lane, idx % 2), q, jnp.zeros_like(q))
            return lax.dot_general(k_ref[:, ls], qm, (((1,), (1,)), ((), ())), preferred_element_type=F32)

        ones_rows = jnp.ones((8, tk), BF16)
        ahead = [scores(i) for i in range(LOOKAHEAD)]
        for idx in range(2 * nslab):
            s = ahead.pop(0)
            if idx + LOOKAHEAD < 2 * nslab:
                ahead.append(scores(idx + LOOKAHEAD))
            vt = jnp.concatenate([vt_ref[(idx // 2) * LANES:(idx // 2 + 1) * LANES, :], ones_rows], axis=0)
            if mode == "fox":
                ck = ck_ref[:, idx * LANES:(idx + 1) * LANES]
                s = s - jnp.concatenate([ck] * (tq // LANES), axis=1)
            m_old = m_sc[idx:idx + 1, :]
            if fixed_ref:
                ref = (m_old - cq_ref[0, idx:idx + 1, :]) if mode == "fox" else m_old
                pr = jnp.exp(s - ref).astype(BF16)
                if mode == "dsa" or diag:
                    pr = jnp.where(keep_all, pr, jnp.zeros_like(pr))
                acc_sc[idx] = acc_sc[idx] + jnp.dot(vt, pr, preferred_element_type=F32)
                continue
            if mode == "dsa" or diag:
                s = jnp.where(keep_all, s, NEG)
            mx = jnp.max(s, axis=0, keepdims=True)
            if mode == "fox":
                cq = cq_ref[0, idx:idx + 1, :]
                m_new = jnp.maximum(m_old, mx + cq)
                shift = cq - m_new
            else:
                m_new = jnp.maximum(m_old, mx)
                shift = -m_new
            alpha = jnp.exp(m_old - m_new)
            pr = jnp.exp(s + shift).astype(BF16)
            acc_sc[idx] = alpha * acc_sc[idx] + jnp.dot(vt, pr, preferred_element_type=F32)
            m_sc[idx:idx + 1, :] = m_new

    if mode == "dsa":
        tile_update(False)
    else:
        @pl.when(ki == qi)
        def _():
            tile_update(True)

        @pl.when(ki != qi)
        def _():
            tile_update(False)

    @pl.when(ki == qi)
    def _():
        if mode == "diff":
            lm = lam_ref[...]
            lam = (jnp.exp(jnp.sum(lm[0:1] * lm[1:2], keepdims=True))
                   - jnp.exp(jnp.sum(lm[2:3] * lm[3:4], keepdims=True)) + lam_init)
        feat = lax.broadcasted_iota(jnp.int32, (LANES, 1), 0)
        if fixed_ref:
            for idx in range(2 * nslab):
                den_ref[idx:idx + 1, :] = acc_sc[idx, LANES:LANES + 1, :]
        for sl in range(nslab):
            o0 = acc_sc[2 * sl, 0:LANES, :] / acc_sc[2 * sl, LANES:LANES + 1, :]
            o1 = acc_sc[2 * sl + 1, 0:LANES, :] / acc_sc[2 * sl + 1, LANES:LANES + 1, :]
            if mode == "diff":
                o = (o0 - lam * o1).T
                o = o * lax.rsqrt(jnp.mean(o * o, axis=-1, keepdims=True) + LN_EPS)
                o = (o * g_ref[...]) * (1.0 - lam_init)
            else:
                o = jnp.where(feat < HEAD_DIM, o0, o1).T
            o_ref[:, sl * LANES:(sl + 1) * LANES] = o.astype(o_ref.dtype)


def _subhead_lanes(lane, sub):
    return ((lane >> 5) & 1) == sub


LOOKAHEAD = 1
DEN_FLOOR = 1e-30


def _attention(mode, qarr, qcol, karr, kcol, vt, vrow, B, S, extra, norms, lam_init=0.0, tile=ATTN_TILE):
    o_fast, den = _attention_call(mode, qarr, qcol, karr, kcol, vt, vrow, B, S, extra, lam_init, tile, norms)
    bad = jnp.logical_not(jnp.all(den > DEN_FLOOR))
    return lax.cond(
        bad,
        lambda: _attention_call(mode, qarr, qcol, karr, kcol, vt, vrow, B, S, extra, lam_init, tile, None),
        lambda: o_fast)


def _attention_call(mode, qarr, qcol, karr, kcol, vt, vrow, B, S, extra, lam_init, tile, norms):
    nslab = 4
    width = nslab * LANES
    nq = S // tile
    pairs = [(qi, ki) for qi in range(nq) for ki in range(qi + 1)]
    qt = jnp.asarray([pq for pq, _ in pairs], jnp.int32)
    kt = jnp.asarray([pk for _, pk in pairs], jnp.int32)
    T = B * S
    fixed_ref = (norms[1], norms[3]) if norms is not None else None
    qmap = lambda b, p, qt, kt: (b * nq + qt[p], qcol // width)
    kmap = lambda b, p, qt, kt: (b * nq + kt[p], kcol // width)
    vmap = lambda b, p, qt, kt: (vrow // width, b * nq + kt[p])
    in_specs = [pl.BlockSpec((tile, width), qmap), pl.BlockSpec((tile, width), kmap),
                pl.BlockSpec((width, tile), vmap)]
    args = [qarr, karr, vt]
    if fixed_ref:
        nspec = pl.BlockSpec((1, 8, LANES), lambda b, p, qt, kt: (b, 0, 0))
        in_specs += [nspec, nspec]
        args += [norms[0], norms[2]]
    if mode == "diff":
        lam4, g = extra
        in_specs += [pl.BlockSpec((4, HEAD_DIM), lambda b, p, qt, kt: (0, 0)),
                     pl.BlockSpec((1, LANES), lambda b, p, qt, kt: (0, 0))]
        args += [lam4, g.reshape(1, LANES)]
    elif mode == "dsa":
        (mask_t,) = extra
        in_specs += [pl.BlockSpec((tile, tile), lambda b, p, qt, kt: (b * nq + kt[p], qt[p]))]
        args += [mask_t]
    else:
        ct, crep = extra
        in_specs += [pl.BlockSpec((1, FOX_HEADS, tile), lambda b, p, qt, kt: (b, 0, qt[p])),
                     pl.BlockSpec((tile, FOX_HEADS * LANES), lambda b, p, qt, kt: (b * nq + kt[p], 0))]
        args += [ct, crep]
    out_shape = jax.ShapeDtypeStruct((T, width), BF16)
    out_specs = pl.BlockSpec((tile, width), lambda b, p, qt, kt: (b * nq + qt[p], 0))
    if fixed_ref:
        out_shape = (out_shape, jax.ShapeDtypeStruct((2 * nslab, T), F32))
        out_specs = (out_specs, pl.BlockSpec((2 * nslab, tile), lambda b, p, qt, kt: (0, b * nq + qt[p])))
    return pl.pallas_call(
        functools.partial(_attn_body, mode, nslab, lam_init, fixed_ref),
        out_shape=out_shape,
        grid_spec=pltpu.PrefetchScalarGridSpec(
            num_scalar_prefetch=2,
            grid=(B, len(pairs)),
            in_specs=in_specs,
            out_specs=out_specs,
            scratch_shapes=[pltpu.VMEM((2 * nslab, tile), F32),
                            pltpu.VMEM((2 * nslab, LANES + 8, tile), F32)]),
        compiler_params=_cparams(("arbitrary", "arbitrary")),
        name="attn_" + mode + ("_fixed" if fixed_ref else "_online"),
    )(qt, kt, *args)


def _select_body(topk, iq_ref, ik_ref, h_ref, wiw_ref, tri_ref, mask_ref, keys_ref, hi_ref, lo_ref):
    qi = pl.program_id(1)
    tq = iq_ref.shape[0]
    tk = tq
    nkb = qi + 1
    q0 = qi * tq
    lane = lax.broadcasted_iota(jnp.int32, (1, LANES), 1)
    qhs = []
    for h in range(IDX_HEADS):
        slab = iq_ref[:, (h // 2) * LANES:(h // 2 + 1) * LANES]
        qhs.append(jnp.where(_subhead_lanes(lane, h % 2), slab, jnp.zeros_like(slab)))
    iwt = lax.dot_general(wiw_ref[...], h_ref[...], (((1,), (1,)), ((), ())), preferred_element_type=F32)
    q_chunk = (q0 + lax.broadcasted_iota(jnp.int32, (tk, tq), 1)) >> CHUNK_SHIFT
    k_iota = lax.broadcasted_iota(jnp.int32, (tk, tq), 0)

    mask_ref[...] = jnp.zeros_like(mask_ref)

    def score_block(j, carry):
        k0 = pl.multiple_of(j * tk, tk)
        ik = ik_ref[pl.ds(k0, tk), 0:LANES]
        score = jnp.zeros((tk, tq), F32)
        for h in range(IDX_HEADS):
            lg = lax.dot_general(ik, qhs[h], (((1,), (1,)), ((), ())), preferred_element_type=F32)
            score = score + iwt[SM_IW + h:SM_IW + h + 1, :] * jnp.maximum(lg, 0.0)
        score = jnp.where(score == 0.0, 0.0, score)
        bits = lax.bitcast_convert_type(score, jnp.int32)
        key = bits ^ ((bits >> 31) & jnp.int32(0x7FFFFFFF))
        vis = ((k0 + k_iota) >> CHUNK_SHIFT) <= q_chunk
        key = jnp.where(vis, key, jnp.int32(INT_MIN))
        keys_ref[pl.ds(k0, tk), :] = key
        hi_ref[pl.ds(k0, tk), :] = (key >> 16).astype(jnp.int16)
        lo_ref[pl.ds(k0, tk), :] = ((key & jnp.int32(0xFFFF)) - HALF).astype(jnp.int16)
        return carry

    lax.fori_loop(0, nkb, score_block, 0)

    npair = (nkb + 1) >> 1

    @pl.when((nkb & 1) == 1)
    def _():
        k0 = pl.multiple_of(nkb * tk, tk)
        hi_ref[pl.ds(k0, tk), :] = jnp.full((tk, tq), -HALF, jnp.int16)
        lo_ref[pl.ds(k0, tk), :] = jnp.full((tk, tq), -HALF, jnp.int16)

    def count16(ref, pred):
        def body(j, acc):
            k0 = pl.multiple_of(j * (2 * tk), 2 * tk)
            hit = jnp.where(pred(ref[pl.ds(k0, 2 * tk), :]), jnp.int16(1), jnp.int16(0))
            for r in range(2 * tk // 16):
                acc = acc + hit[r * 16:(r + 1) * 16, :]
            return acc
        acc = lax.fori_loop(0, npair, body, jnp.zeros((16, tq), jnp.int16))
        return jnp.sum(acc.astype(jnp.int32), axis=0, keepdims=True)

    def search16(ref, base):
        def bit_step(i, lo):
            cand = lo + (jnp.int32(1) << (15 - i))
            c16 = cand.astype(jnp.int16)
            return jnp.where(base + count16(ref, lambda blk: blk >= c16) >= topk, cand, lo)
        return lax.fori_loop(0, 16, bit_step, jnp.full((1, tq), -HALF, jnp.int32))

    thr_hi = search16(hi_ref, 0)
    thr_hi16 = thr_hi.astype(jnp.int16)
    above = count16(hi_ref, lambda blk: blk > thr_hi16)

    def keep_low_of_threshold_rows(j, carry):
        k0 = pl.multiple_of(j * tk, tk)
        lo_ref[pl.ds(k0, tk), :] = jnp.where(hi_ref[pl.ds(k0, tk), :] == thr_hi16, lo_ref[pl.ds(k0, tk), :],
                                             jnp.int16(-HALF))
        return carry

    lax.fori_loop(0, nkb, keep_low_of_threshold_rows, 0)
    thr_lo = search16(lo_ref, above)
    thr_lo16 = thr_lo.astype(jnp.int16)
    above = above + count16(lo_ref, lambda blk: blk > thr_lo16)
    thr = (thr_hi << 16) + (thr_lo + HALF)
    thr = jnp.maximum(thr, jnp.int32(INT_MIN + 1))
    need = (topk - above).astype(F32)
    tri = tri_ref[...]

    def select_block(j, carry):
        k0 = pl.multiple_of(j * tk, tk)
        key = keys_ref[pl.ds(k0, tk), :]
        eq = key == thr
        incl = jnp.dot(tri, jnp.where(eq, 1.0, 0.0).astype(BF16), preferred_element_type=F32)
        sel = (key > thr) | (eq & ((carry + incl) <= need))
        mask_ref[pl.ds(k0, tk), :] = jnp.where(sel, 1, 0).astype(jnp.int8)
        return carry + incl[tk - 1:tk, :]

    lax.fori_loop(0, nkb, select_block, jnp.zeros((1, tq), F32))


def _dsa_select(qk, h16, w_iw_t, B, S, topk, tq=SELECT_TILE):
    T = B * S
    D = h16.shape[1]
    nq = S // tq
    tri = jnp.tril(jnp.ones((tq, tq), F32)).astype(BF16)
    return pl.pallas_call(
        functools.partial(_select_body, topk),
        out_shape=jax.ShapeDtypeStruct((T, S), jnp.int8),
        grid=(B, nq),
        in_specs=[pl.BlockSpec((tq, 256), lambda b, i: (b * nq + i, QK_IQ // 256)),
                  pl.BlockSpec((S, 256), lambda b, i: (b, QK_IK // 256)),
                  pl.BlockSpec((tq, D), lambda b, i: (b * nq + i, 0)),
                  pl.BlockSpec((8, D), lambda b, i: (0, 0)),
                  pl.BlockSpec((tq, tq), lambda b, i: (0, 0))],
        out_specs=pl.BlockSpec((S, tq), lambda b, i: (b, i)),
        scratch_shapes=[pltpu.VMEM((S, tq), jnp.int32), pltpu.VMEM((S, tq), jnp.int16),
                        pltpu.VMEM((S, tq), jnp.int16)],
        compiler_params=_cparams(("arbitrary", "arbitrary")),
        name="dsa_select",
    )(qk, qk, h16, w_iw_t, tri)


def _merge_body(alpha, od_ref, os_ref, of_ref, gate_ref, h_ref, wbr_ref, wo_ref, g_ref, b_ref, wr_ref,
                h32_ref, h16_ref, aff_ref):
    D = h_ref.shape[1]
    y = None
    for i, o_ref in enumerate((od_ref, os_ref, of_ref)):
        br = jnp.dot(o_ref[...], wbr_ref[i], preferred_element_type=F32)
        t = gate_ref[:, i * D:(i + 1) * D].astype(F32) * br
        y = t if y is None else y + t
    m = jnp.dot(y.astype(BF16), wo_ref[...], preferred_element_type=F32)
    hn = _ln_rows(alpha * h_ref[...] + m, g_ref[...], b_ref[...])
    h32_ref[...] = hn
    h16 = hn.astype(BF16)
    h16_ref[...] = h16
    logits = lax.dot_general(wr_ref[...], h16, (((1,), (1,)), ((), ())), preferred_element_type=F32)
    aff_ref[...] = 1.0 / (1.0 + jnp.exp(-logits))


def _merge(alpha, od, os_, of, gates, h32, wbr, wo, g, b, wr_t, tm=ROW_TILE):
    T, D = h32.shape
    bw = od.shape[1]
    row = lambda i: (i, 0)
    fixed2 = lambda i: (0, 0)
    return pl.pallas_call(
        functools.partial(_merge_body, alpha),
        out_shape=(jax.ShapeDtypeStruct((T, D), F32), jax.ShapeDtypeStruct((T, D), BF16),
                   jax.ShapeDtypeStruct((N_EXPERTS, T), F32)),
        grid=(T // tm,),
        in_specs=[pl.BlockSpec((tm, bw), row), pl.BlockSpec((tm, bw), row), pl.BlockSpec((tm, bw), row),
                  pl.BlockSpec((tm, N_BRANCH * D), row), pl.BlockSpec((tm, D), row),
                  pl.BlockSpec((N_BRANCH, bw, D), lambda i: (0, 0, 0)), pl.BlockSpec((D, D), fixed2),
                  pl.BlockSpec((1, D), fixed2), pl.BlockSpec((1, D), fixed2),
                  pl.BlockSpec((N_EXPERTS, D), fixed2)],
        out_specs=(pl.BlockSpec((tm, D), row), pl.BlockSpec((tm, D), row),
                   pl.BlockSpec((N_EXPERTS, tm), lambda i: (0, i))),
        compiler_params=_cparams(("arbitrary",)),
        name="merge",
    )(od, os_, of, gates, h32, wbr, wo, g.reshape(1, D), b.reshape(1, D), wr_t)


def _route_body(aff_ref, bias_ref, e_ref, w_ref):
    aff = aff_ref[...]
    sel = aff + bias_ref[...]
    sub = lax.broadcasted_iota(jnp.int32, (EXPERTS_PER_GROUP, aff.shape[1]), 0)
    big = jnp.int32(EXPERTS_PER_GROUP)
    best = None
    for gidx in range(N_GROUPS):
        rs = slice(gidx * EXPERTS_PER_GROUP, (gidx + 1) * EXPERTS_PER_GROUP)
        s8 = sel[rs]
        a8 = aff[rs]
        m1 = jnp.max(s8, axis=0, keepdims=True)
        i1 = jnp.min(jnp.where(s8 == m1, sub, big), axis=0, keepdims=True)
        rest = jnp.where(sub == i1, -jnp.inf, s8)
        m2 = jnp.max(rest, axis=0, keepdims=True)
        i2 = jnp.min(jnp.where(rest == m2, sub, big), axis=0, keepdims=True)
        a1 = jnp.sum(jnp.where(sub == i1, a8, 0.0), axis=0, keepdims=True)
        a2 = jnp.sum(jnp.where(sub == i2, a8, 0.0), axis=0, keepdims=True)
        score = m1 + m2
        cand = (score, i1 + gidx * EXPERTS_PER_GROUP, i2 + gidx * EXPERTS_PER_GROUP, a1, a2)
        if best is None:
            best = cand
        else:
            take = score > best[0]
            best = tuple(jnp.where(take, c, o) for c, o in zip(cand, best))
    _, e1, e2, a1, a2 = best
    tot = a1 + a2
    e_ref[0:1, :] = e1
    e_ref[1:2, :] = e2
    w_ref[0:1, :] = a1 / tot
    w_ref[1:2, :] = a2 / tot


def _route(aff_t, b_router, tn=2048):
    E, T = aff_t.shape
    tn = min(tn, T)
    return pl.pallas_call(
        _route_body,
        out_shape=(jax.ShapeDtypeStruct((TOP_K, T), jnp.int32), jax.ShapeDtypeStruct((TOP_K, T), F32)),
        grid=(T // tn,),
        in_specs=[pl.BlockSpec((E, tn), lambda i: (0, i)), pl.BlockSpec((E, 1), lambda i: (0, 0))],
        out_specs=(pl.BlockSpec((TOP_K, tn), lambda i: (0, i)), pl.BlockSpec((TOP_K, tn), lambda i: (0, i))),
        compiler_params=_cparams(("arbitrary",)),
        name="route",
    )(aff_t, b_router.reshape(E, 1))


def _expert_body(be_ref, bc_ref, tok0_ref, tokn_ref, slot_ref, x_hbm, wg_ref, wu_ref, wd_ref, y_hbm,
                 xbuf, ybuf, gsem, ssem, wg16, wu16, wd16):
    i = pl.program_id(0)
    nb = pl.num_programs(0)
    cur = i & 1
    bm = xbuf.shape[1]
    n_real = y_hbm.shape[0] - 2 * bm

    def busy(j):
        return (j >= 0) & (j < nb) & (bc_ref[jnp.clip(j, 0, nb - 1)] > 0)

    def start_gather(tok_ref, slot):
        for r in range(bm):
            pltpu.make_async_copy(x_hbm.at[pl.ds(tok_ref[0, 0, r], 1), :], xbuf.at[slot, pl.ds(r, 1), :],
                                  gsem.at[slot]).start(priority=r % 2)

    def wait_gather(slot):
        pltpu.make_async_copy(x_hbm.at[pl.ds(0, bm), :], xbuf.at[slot], gsem.at[slot]).wait()

    def wait_scatter(slot):
        pltpu.make_async_copy(ybuf.at[slot], y_hbm.at[pl.ds(0, bm), :], ssem.at[slot]).wait()

    @pl.when(i == 0)
    def _():
        ybuf[...] = jnp.zeros_like(ybuf)
        for s in range(2):
            spare = pltpu.make_async_copy(ybuf.at[s], y_hbm.at[pl.ds(n_real + s * bm, bm), :], ssem.at[s])
            spare.start()
            spare.wait()

        @pl.when(busy(0))
        def _():
            start_gather(tok0_ref, 0)

    @pl.when(busy(i + 1))
    def _():
        start_gather(tokn_ref, 1 - cur)

    @pl.when(busy(i))
    def _():
        wait_gather(cur)

    @pl.when(busy(i - 2))
    def _():
        wait_scatter(cur)

    @pl.when((i == 0) | (be_ref[i] != be_ref[jnp.maximum(i - 1, 0)]))
    def _():
        wg16[...] = wg_ref[0].astype(BF16)
        wu16[...] = wu_ref[0].astype(BF16)
        wd16[...] = wd_ref[0].astype(BF16)

    @pl.when(busy(i))
    def _():
        x = xbuf[cur].astype(BF16)
        gate = jnp.dot(x, wg16[...], preferred_element_type=F32)
        up = jnp.dot(x, wu16[...], preferred_element_type=F32)
        hmid = (gate * (1.0 / (1.0 + jnp.exp(-gate)))) * up
        ybuf[cur] = jnp.dot(hmid.astype(BF16), wd16[...], preferred_element_type=F32)
        for r in range(bm):
            pltpu.make_async_copy(ybuf.at[cur, pl.ds(r, 1), :], y_hbm.at[pl.ds(slot_ref[0, 0, r], 1), :],
                                  ssem.at[cur]).start(priority=r % 2)

    @pl.when(i == nb - 1)
    def _():
        @pl.when(busy(i))
        def _():
            wait_scatter(cur)

        @pl.when(busy(i - 1))
        def _():
            wait_scatter(1 - cur)


def _experts(h32, blk_e, blk_cnt, tok_pad, slot_pad, wg, wu, wd, bm):
    T, D = h32.shape
    nb = blk_e.shape[0]
    de = wg.shape[2]
    smem_blk = lambda f: pl.BlockSpec((1, 1, bm), f, memory_space=pltpu.SMEM)
    return pl.pallas_call(
        _expert_body,
        out_shape=jax.ShapeDtypeStruct((TOP_K * T + 2 * bm, D), F32),
        grid_spec=pltpu.PrefetchScalarGridSpec(
            num_scalar_prefetch=2,
            grid=(nb,),
            in_specs=[smem_blk(lambda i, be, bc: (0, 0, 0)),
                      smem_blk(lambda i, be, bc: (jnp.minimum(i + 1, nb - 1), 0, 0)),
                      smem_blk(lambda i, be, bc: (i, 0, 0)),
                      pl.BlockSpec(memory_space=pl.ANY),
                      pl.BlockSpec((1, D, de), lambda i, be, bc: (be[i], 0, 0)),
                      pl.BlockSpec((1, D, de), lambda i, be, bc: (be[i], 0, 0)),
                      pl.BlockSpec((1, de, D), lambda i, be, bc: (be[i], 0, 0))],
            out_specs=pl.BlockSpec(memory_space=pl.ANY),
            scratch_shapes=[pltpu.VMEM((2, bm, D), F32), pltpu.VMEM((2, bm, D), F32),
                            pltpu.SemaphoreType.DMA((2,)), pltpu.SemaphoreType.DMA((2,)),
                            pltpu.VMEM((D, de), BF16), pltpu.VMEM((D, de), BF16), pltpu.VMEM((de, D), BF16)]),
        compiler_params=_cparams(("arbitrary",)),
        name="experts",
    )(blk_e, blk_cnt, tok_pad, tok_pad, slot_pad, h32, wg, wu, wd)


def _dispatch_tables(e_t, T, bm):
    A = TOP_K * T
    flat_e = e_t.T.reshape(A)
    order = jnp.argsort(flat_e, stable=True).astype(jnp.int32)
    counts = jnp.sum(flat_e[None, :] == jnp.arange(N_EXPERTS, dtype=jnp.int32)[:, None], axis=1).astype(jnp.int32)
    start = jnp.cumsum(counts) - counts
    nblk_e = (counts + bm - 1) // bm
    blk_end = jnp.cumsum(nblk_e)
    nb = A // bm + N_EXPERTS
    bidx = jnp.arange(nb, dtype=jnp.int32)
    blk_e = jnp.minimum(jnp.sum(bidx[:, None] >= blk_end[None, :], axis=1), N_EXPERTS - 1).astype(jnp.int32)
    off = (bidx - (blk_end - nblk_e)[blk_e]) * bm
    blk_cnt = jnp.where(bidx < blk_end[-1], jnp.clip(counts[blk_e] - off, 0, bm), 0).astype(jnp.int32)
    r = jnp.arange(bm, dtype=jnp.int32)[None, :]
    real = r < blk_cnt[:, None]
    a = order[jnp.clip((start[blk_e] + off)[:, None] + r, 0, A - 1)]
    tok_pad = jnp.where(real, a // TOP_K, 0)
    slot_pad = jnp.where(real, (a % TOP_K) * T + a // TOP_K, A + (bidx % 2)[:, None] * bm + r)
    return blk_e, blk_cnt, tok_pad.reshape(nb, 1, bm), slot_pad.reshape(nb, 1, bm)


def _combine_body(alpha, h_ref, y0_ref, y1_ref, w_ref, g_ref, b_ref, o32_ref, o16_ref):
    w = w_ref[...]
    f = w[:, 0:1] * y0_ref[...] + w[:, 1:2] * y1_ref[...]
    hn = _ln_rows(alpha * h_ref[...] + f, g_ref[...], b_ref[...])
    o32_ref[...] = hn
    o16_ref[...] = hn.astype(BF16)


def _combine(alpha, h32, y2, w_rows, g, b, tm=ROW_TILE):
    T, D = h32.shape
    nt = T // tm
    row = lambda i: (i, 0)
    fixed = lambda i: (0, 0)
    return pl.pallas_call(
        functools.partial(_combine_body, alpha),
        out_shape=(jax.ShapeDtypeStruct((T, D), F32), jax.ShapeDtypeStruct((T, D), BF16)),
        grid=(nt,),
        in_specs=[pl.BlockSpec((tm, D), row), pl.BlockSpec((tm, D), row),
                  pl.BlockSpec((tm, D), lambda i: (nt + i, 0)),
                  pl.BlockSpec((tm, TOP_K), row), pl.BlockSpec((1, D), fixed), pl.BlockSpec((1, D), fixed)],
        out_specs=(pl.BlockSpec((tm, D), row), pl.BlockSpec((tm, D), row)),
        compiler_params=_cparams(("arbitrary",)),
        name="combine",
    )(h32, y2, y2, w_rows, g.reshape(1, D), b.reshape(1, D))


def _pair_halves_columns(w):
    d = w.shape[0]
    w5 = w.reshape(d, -1, 2, 2, HEAD_DIM // 2)
    return jnp.transpose(w5, (0, 1, 3, 2, 4)).reshape(w.shape)


def _prep_in_weights(w_in_l):
    qk_w = DIFF_HEADS * 2 * HEAD_DIM
    sizes = (qk_w, qk_w, DIFF_HEADS * 2 * HEAD_DIM, 512, 512, 512, IDX_HEADS * HEAD_DIM, HEAD_DIM, IDX_HEADS,
             512, 512, 512, FOX_HEADS, N_BRANCH * w_in_l.shape[0])
    offs = [0]
    for s in sizes:
        offs.append(offs[-1] + s)
    dq, dk, dv, sq, sk, sv, iq, ik, iw, fq, fk, fv, ff, g = (w_in_l[:, offs[i]:offs[i + 1]] for i in range(len(sizes)))
    scale = HEAD_DIM ** -0.5
    idx_scale = (IDX_HEADS * HEAD_DIM) ** -0.5
    w_rope = _pair_halves_columns(jnp.concatenate([dq * scale, dk, sq * scale, sk, iq, ik, ik, ik, ik], axis=1))
    w_plain = _pair_halves_columns(jnp.concatenate([fq * scale, fk], axis=1))
    w_vt = jnp.concatenate([dv, sv, fv], axis=1).T
    pad = jnp.zeros((w_in_l.shape[0], LANES - IDX_HEADS - FOX_HEADS), w_in_l.dtype)
    w_small = jnp.concatenate([iw * idx_scale, ff, pad], axis=1)
    return (w_rope.astype(BF16), w_plain.astype(BF16), w_vt.astype(BF16), g.astype(BF16), w_small.astype(BF16))


def _rope_tables(positions):
    inv = ROPE_THETA ** (-jnp.arange(0, HEAD_DIM, 2, dtype=F32) / HEAD_DIM)
    ang = positions.astype(F32).reshape(-1, 1) * inv
    c, s = jnp.cos(ang), jnp.sin(ang)
    return jnp.tile(c, (1, 4)), jnp.concatenate([-s, -s, s, s], axis=1)


def kernel(x, positions, ln_in_g, ln_in_b, w_in, b_forget, diff_lam, diff_norm_g, w_br, w_o, ln1_g, ln1_b,
           w_router, b_router, w_gate, w_up, w_down, ln2_g, ln2_b):
    B, S, D = x.shape
    T = B * S
    depth = w_in.shape[0]
    alpha = (2 * depth) ** 0.25
    topk = min(DSA_TOPK_MAX, S // 4)
    bm = EXPERT_ROWS
    cc, ss = _rope_tables(positions)
    wr_t = w_router.T.astype(BF16)
    h32, h16 = _layer_norm(x.reshape(T, D), ln_in_g, ln_in_b)
    for l in range(depth):
        lam_init = 0.8 - 0.6 * math.exp(-0.3 * l)
        w_rope, w_plain, w_vt, w_g, w_small = _prep_in_weights(w_in[l])
        qk, n_qk = _project(h16, w_rope, "rope", BF16, B=B, tables=(cc, ss), n_norm=4)
        fqk, n_f = _project(h16, w_plain, "plain", BF16, B=B, n_norm=2)
        vt = _project(h16, w_vt, "transposed", BF16)
        gates = _project(h16, w_g, "sigmoid", BF16)

        o_diff = _attention("diff", qk, QK_DQ, qk, QK_DK, vt, VT_DV, B, S,
                            (diff_lam[l], diff_norm_g[l]), (n_qk, 0, n_qk, 1), lam_init=lam_init)
        mask_t = _dsa_select(qk, h16, w_small[:, 0:8].T, B, S, topk)
        o_dsa = _attention("dsa", qk, QK_SQ, qk, QK_SK, vt, VT_SV, B, S, (mask_t,), (n_qk, 2, n_qk, 3))
        bias_row = jnp.zeros((1, LANES), F32).at[0, SM_FF:SM_FF + FOX_HEADS].set(b_forget[l])
        c, crep = _forget_cumsum(h16, w_small, bias_row, B, S)
        ct = jnp.transpose(c[:, SM_FF:SM_FF + FOX_HEADS].reshape(B, S, FOX_HEADS), (0, 2, 1))
        o_fox = _attention("fox", fqk, FQK_FQ, fqk, FQK_FK, vt, VT_FV, B, S, (ct, crep), (n_f, 0, n_f, 1))

        h32, h16, aff_t = _merge(alpha, o_diff, o_dsa, o_fox, gates, h32, w_br[l].astype(BF16),
                                 w_o[l].astype(BF16), ln1_g[l], ln1_b[l], wr_t)
        e_t, w_t = _route(aff_t, b_router)
        blk_e, blk_cnt, tok_pad, slot_pad = _dispatch_tables(e_t, T, bm)
        y2 = _experts(h32, blk_e + l * N_EXPERTS, blk_cnt, tok_pad, slot_pad,
                      w_gate.reshape((-1,) + w_gate.shape[2:]), w_up.reshape((-1,) + w_up.shape[2:]),
                      w_down.reshape((-1,) + w_down.shape[2:]), bm)
        h32, h16 = _combine(alpha, h32, y2, w_t.T, ln2_g[l], ln2_b[l])
    return h32.reshape(B, S, D)
```
